```python
import jax, jax.numpy as jnp
from jax import lax
import numpy as np

D_MODEL = 2048
BATCH = 8
SEQ = 2048
DEPTH = 1

RET_HEADS = 8
RET_HEAD_DIM = 128
D_RET = RET_HEADS * RET_HEAD_DIM
CHUNK = 128
ROPE_BASE = 10000.0
D_CONV = D_MODEL // 2
CONV_K = 3
N_EXPERTS = 64
TOP_K = 8
N_GROUPS = 8
TOPK_GROUPS = 4
EXPERTS_PER_GROUP = N_EXPERTS // N_GROUPS
D_EXPERT = 512
D_SHARED = 512
ROUTED_SCALE = 2.5
MOE_BLOCK = 128
EPS = 1e-6
IN_SIZES = (D_RET, D_RET, D_RET, D_RET, D_CONV, D_CONV, D_CONV, D_MODEL, D_MODEL)
D_IN = sum(IN_SIZES)

kernel_name = "hybrid_retention_shortconv_moe_block"


def rms_norm(x, w):
    xf = x.astype(jnp.float32)
    y = xf * lax.rsqrt(jnp.mean(xf * xf, axis=-1, keepdims=True) + EPS)
    return (y * w.astype(jnp.float32)).astype(x.dtype)


def rotary(t):
    s, d = t.shape[1], t.shape[-1]
    half = d // 2
    inv = ROPE_BASE ** (-jnp.arange(half, dtype=jnp.float32) / half)
    ang = jnp.arange(s, dtype=jnp.float32)[:, None] * inv[None, :]
    cos, sin = jnp.cos(ang)[:, None, :], jnp.sin(ang)[:, None, :]
    t1, t2 = t[..., :half], t[..., half:]
    return jnp.concatenate([t1 * cos - t2 * sin, t1 * sin + t2 * cos], axis=-1)


def bidirectional_retention(q, k, v, decay_fwd, decay_bwd):
    b, s, h, d = q.shape
    n = s // CHUNK
    to_chunks = lambda t: t.reshape(b, n, CHUNK, h, d).transpose(0, 3, 1, 2, 4)
    q, k, v = to_chunks(q), to_chunks(k), to_chunks(v)
    lg_f = jax.nn.log_sigmoid(decay_fwd.astype(jnp.float32))
    lg_b = jax.nn.log_sigmoid(decay_bwd.astype(jnp.float32))
    pos = jnp.arange(CHUNK, dtype=jnp.float32)
    diff = pos[:, None] - pos[None, :]
    mask = jnp.where(diff >= 0,
                     jnp.exp(lg_f[:, None, None] * jnp.maximum(diff, 0.0)),
                     jnp.exp(lg_b[:, None, None] * jnp.maximum(-diff, 0.0)))
    scores = jnp.einsum('bhnid,bhnjd->bhnij', q, k) * mask[None, :, None]
    out = jnp.einsum('bhnij,bhnje->bhnie', scores, v).astype(jnp.float32)

    def chunk_scan(lg, key_w, reverse):
        kv = jnp.einsum('bhnjd,hj,bhnje->nbhde', k, key_w, v).astype(jnp.float32)
        decay = jnp.exp(lg * CHUNK)[None, :, None, None]

        def step(state, kv_n):
            return decay * state + kv_n, state
        _, states = lax.scan(step, jnp.zeros((b, h, d, d), jnp.float32), kv, reverse=reverse)
        return states

    st_f = chunk_scan(lg_f, jnp.exp(lg_f[:, None] * (CHUNK - 1.0 - pos)[None, :]), False)
    q_f = q * jnp.exp(lg_f[:, None] * (pos + 1.0)[None, :])[None, :, None, :, None]
    out = out + jnp.einsum('bhnid,nbhde->bhnie', q_f, st_f)
    st_b = chunk_scan(lg_b, jnp.exp(lg_b[:, None] * pos[None, :]), True)
    q_b = q * jnp.exp(lg_b[:, None] * (CHUNK - pos)[None, :])[None, :, None, :, None]
    out = out + jnp.einsum('bhnid,nbhde->bhnie', q_b, st_b)
    return out.transpose(0, 2, 3, 1, 4).reshape(b, s, h, d)


def token_mixer(h, w_in, decay_fwd, decay_bwd, ret_norm_w, w_ret_o, conv_w, w_conv_o, w_out):
    b, s, _ = h.shape
    proj = h @ w_in
    split_pts = np.cumsum(IN_SIZES)[:-1].tolist()
    q, k, v, g, cb, cc, cu, gate_r, gate_c = jnp.split(proj, split_pts, axis=-1)
    heads = lambda t: t.reshape(b, s, RET_HEADS, RET_HEAD_DIM)
    qh = rotary(heads(q))
    kh = rotary(heads(k)) * (RET_HEAD_DIM ** -0.5)
    o = bidirectional_retention(qh, kh, heads(v), decay_fwd, decay_bwd)
    o = rms_norm(o, ret_norm_w.reshape(RET_HEADS, RET_HEAD_DIM)).reshape(b, s, D_RET)
    y_ret = (jax.nn.silu(g) * o.astype(g.dtype)) @ w_ret_o
    u = cc * cu
    y = lax.conv_general_dilated(u, conv_w[:, None, :].astype(u.dtype), window_strides=(1,),
                                 padding=[(CONV_K // 2, CONV_K // 2)],
                                 dimension_numbers=('NWC', 'WIO', 'NWC'),
                                 feature_group_count=D_CONV)
    y_conv = (cb * y) @ w_conv_o
    merged = jax.nn.sigmoid(gate_r) * y_ret + jax.nn.sigmoid(gate_c) * y_conv
    return merged @ w_out


def swiglu(x, wg, wu, wd):
    return (jax.nn.silu(x @ wg) * (x @ wu)) @ wd


def moe_ffn(h, router_w, router_bias, w_gate, w_up, w_down, ws_gate, ws_up, ws_down):
    b, s, dm = h.shape
    t = b * s
    xf = h.reshape(t, dm)
    scores = jax.nn.sigmoid(jnp.dot(xf.astype(jnp.float32), router_w.astype(jnp.float32)))
    biased = scores + router_bias.astype(jnp.float32)
    grp_score = lax.top_k(biased.reshape(t, N_GROUPS, EXPERTS_PER_GROUP), 2)[0].sum(-1)
    _, grp_idx = lax.top_k(grp_score, TOPK_GROUPS)
    grp_mask = jnp.any(grp_idx[:, :, None] == jnp.arange(N_GROUPS)[None, None, :], axis=1)
    masked = jnp.where(jnp.repeat(grp_mask, EXPERTS_PER_GROUP, axis=1), biased, -jnp.inf)
    _, top_idx = lax.top_k(masked, TOP_K)
    top_s = jnp.take_along_axis(scores, top_idx, axis=1)
    top_w = top_s / jnp.sum(top_s, axis=-1, keepdims=True) * ROUTED_SCALE

    tk = t * TOP_K
    flat_e = top_idx.reshape(tk)
    flat_tok = jnp.arange(tk, dtype=jnp.int32) // TOP_K
    flat_w = top_w.reshape(tk)
    order = jnp.argsort(flat_e)
    se, stok, sw = flat_e[order], flat_tok[order], flat_w[order]
    counts = jnp.bincount(flat_e, length=N_EXPERTS)
    padded = (counts + MOE_BLOCK - 1) // MOE_BLOCK * MOE_BLOCK
    pad_end = jnp.cumsum(padded)
    pad_start = pad_end - padded
    start = jnp.cumsum(counts) - counts
    dest = pad_start[se] + (jnp.arange(tk) - start[se])
    cap = tk + N_EXPERTS * MOE_BLOCK
    n_blocks = cap // MOE_BLOCK
    buf_tok = jnp.zeros((cap,), jnp.int32).at[dest].set(stok)
    buf_w = jnp.zeros((cap,), jnp.float32).at[dest].set(sw)
    blk_e = jnp.minimum(jnp.searchsorted(pad_end, jnp.arange(n_blocks) * MOE_BLOCK, side='right'),
                        N_EXPERTS - 1)

    def expert_block(args):
        tok, wgt, e = args
        xb = xf[tok]
        return swiglu(xb, w_gate[e], w_up[e], w_down[e]) * wgt[:, None].astype(xb.dtype)

    out = lax.map(expert_block, (buf_tok.reshape(n_blocks, MOE_BLOCK),
                                 buf_w.reshape(n_blocks, MOE_BLOCK), blk_e))
    routed = jax.ops.segment_sum(out.reshape(cap, dm), buf_tok, num_segments=t)
    shared = swiglu(xf, ws_gate, ws_up, ws_down)
    return (routed + shared).reshape(b, s, dm)


def setup_inputs(seed: int = 0) -> dict:
    key = jax.random.key(seed)
    ks = jax.random.split(key, 24)
    f32 = jnp.float32
    nrm = lambda k, shape, fan_in, gain=1.0: jax.random.normal(k, shape, f32) * (gain * fan_in ** -0.5)
    L = DEPTH
    gam = 1.0 - 2.0 ** (-5.0 - jnp.arange(RET_HEADS, dtype=f32))
    logit_gam = jnp.log(gam) - jnp.log1p(-gam)
    return {
        "x": jax.random.normal(ks[0], (BATCH, SEQ, D_MODEL), f32),
        "c": jax.random.normal(ks[1], (BATCH, D_MODEL), f32),
        "w_ada": nrm(ks[2], (L, D_MODEL, 6 * D_MODEL), D_MODEL, 0.5),
        "b_ada": 0.02 * jax.random.normal(ks[3], (L, 6 * D_MODEL), f32),
        "norm1_w": 1.0 + 0.02 * jax.random.normal(ks[4], (L, D_MODEL), f32),
        "w_in": nrm(ks[5], (L, D_MODEL, D_IN), D_MODEL),
        "ret_decay_fwd": logit_gam[None] + 0.1 * jax.random.normal(ks[6], (L, RET_HEADS), f32),
        "ret_decay_bwd": logit_gam[None] + 0.1 * jax.random.normal(ks[7], (L, RET_HEADS), f32),
        "ret_norm_w": 1.0 + 0.02 * jax.random.normal(ks[8], (L, D_RET), f32),
        "w_ret_o": nrm(ks[9], (L, D_RET, D_MODEL), D_RET),
        "conv_w": nrm(ks[10], (L, CONV_K, D_CONV), CONV_K),
        "w_conv_o": nrm(ks[11], (L, D_CONV, D_MODEL), D_CONV),
        "w_out": nrm(ks[12], (L, D_MODEL, D_MODEL), D_MODEL),
        "norm2_w": 1.0 + 0.02 * jax.random.normal(ks[13], (L, D_MODEL), f32),
        "router_w": nrm(ks[14], (L, D_MODEL, N_EXPERTS), D_MODEL),
        "router_bias": 0.01 * jax.random.normal(ks[15], (L, N_EXPERTS), f32),
        "w_gate": nrm(ks[16], (L, N_EXPERTS, D_MODEL, D_EXPERT), D_MODEL),
        "w_up": nrm(ks[17], (L, N_EXPERTS, D_MODEL, D_EXPERT), D_MODEL),
        "w_down": nrm(ks[18], (L, N_EXPERTS, D_EXPERT, D_MODEL), D_EXPERT),
        "ws_gate": nrm(ks[19], (L, D_MODEL, D_SHARED), D_MODEL),
        "ws_up": nrm(ks[20], (L, D_MODEL, D_SHARED), D_MODEL),
        "ws_down": nrm(ks[21], (L, D_SHARED, D_MODEL), D_SHARED),
        "final_norm_w": 1.0 + 0.02 * jax.random.normal(ks[22], (D_MODEL,), f32),
    }


def reference(x, c, w_ada, b_ada, norm1_w, w_in, ret_decay_fwd, ret_decay_bwd, ret_norm_w,
              w_ret_o, conv_w, w_conv_o, w_out, norm2_w, router_w, router_bias, w_gate, w_up,
              w_down, ws_gate, ws_up, ws_down, final_norm_w):
    for l in range(DEPTH):
        mod = jax.nn.silu(c) @ w_ada[l] + b_ada[l]
        sh1, sc1, g1, sh2, sc2, g2 = jnp.split(mod[:, None, :], 6, axis=-1)
        h = rms_norm(x, norm1_w[l]) * (1.0 + sc1) + sh1
        x = x + g1 * token_mixer(h, w_in[l], ret_decay_fwd[l], ret_decay_bwd[l], ret_norm_w[l],
                                 w_ret_o[l], conv_w[l], w_conv_o[l], w_out[l])
        h = rms_norm(x, norm2_w[l]) * (1.0 + sc2) + sh2
        x = x + g2 * moe_ffn(h, router_w[l], router_bias[l], w_gate[l], w_up[l], w_down[l],
                             ws_gate[l], ws_up[l], ws_down[l])
    return rms_norm(x, final_norm_w)
```

```python
import functools

import jax
import jax.numpy as jnp
from jax import lax
from jax.experimental import pallas as pl
from jax.experimental.pallas import tpu as pltpu

F32 = jnp.float32
BF16 = jnp.bfloat16
I32 = jnp.int32

EPS = 1e-6
RET_HEADS = 8
HEAD_DIM = 128
ROPE_BASE = 10000.0
N_EXPERTS = 64
TOP_K = 8
N_GROUPS = 8
TOPK_GROUPS = 4
GROUP_SIZE = N_EXPERTS // N_GROUPS
ROUTED_SCALE = 2.5

V7X_VMEM_BYTES = 64 * 1024 * 1024
VMEM_LIMIT = V7X_VMEM_BYTES - 8 * 1024 * 1024
BF16_SUBLANES = 16

RET_CHUNK = 512
EXPERT_BLOCK = 256
NEG_INF = float("-inf")


def _params(n_axes):
    return pltpu.CompilerParams(dimension_semantics=("arbitrary",) * n_axes,
                                vmem_limit_bytes=VMEM_LIMIT)


def _sigmoid(x):
    return 1.0 / (1.0 + jnp.exp(-x))


def _silu(x):
    return x * _sigmoid(x)


def _pack_halves(x):
    m = x.shape[1] // 2
    return pltpu.pack_elementwise([x[:, :m], x[:, m:]], packed_dtype=BF16)


def _unpack_halves(p):
    a = pltpu.unpack_elementwise(p, index=0, packed_dtype=BF16, unpacked_dtype=F32)
    b = pltpu.unpack_elementwise(p, index=1, packed_dtype=BF16, unpacked_dtype=F32)
    return jnp.concatenate([a, b], axis=1)


def _ada_kernel(c_ref, w_ref, b_ref, o_ref):
    s = _silu(c_ref[...]).astype(BF16)
    o_ref[...] = jnp.dot(s, w_ref[...].astype(BF16), preferred_element_type=F32) + b_ref[...]


def _ada(c_pad, w_ada, b_ada):
    m, d = c_pad.shape
    n = w_ada.shape[1]
    tn = 1024
    return pl.pallas_call(
        _ada_kernel, grid=(n // tn,),
        in_specs=[pl.BlockSpec((m, d), lambda j: (0, 0)),
                  pl.BlockSpec((d, tn), lambda j: (0, j)),
                  pl.BlockSpec((1, tn), lambda j: (0, j))],
        out_specs=pl.BlockSpec((m, tn), lambda j: (0, j)),
        out_shape=jax.ShapeDtypeStruct((m, n), F32),
        compiler_params=_params(1), name="ada")(c_pad, w_ada, b_ada)


def _inproj_kernel(x_ref, mod_ref, nw_ref, w_ref, o_ref, h_ref):
    @pl.when(pl.program_id(1) == 0)
    def _():
        x = x_ref[...]
        y = x * lax.rsqrt(jnp.mean(x * x, axis=-1, keepdims=True) + EPS) * nw_ref[...]
        m = mod_ref[0]
        h_ref[...] = (y * (1.0 + m[1:2, :]) + m[0:1, :]).astype(BF16)

    o_ref[...] = jnp.dot(h_ref[...], w_ref[...], preferred_element_type=F32).astype(o_ref.dtype)


def _inproj(x2, mod3, norm_w, w_in_b, seq):
    t, d = x2.shape
    n = w_in_b.shape[1]
    tm, tn = min(1024, seq), 1024
    tiles_per_seq = seq // tm
    return pl.pallas_call(
        _inproj_kernel, grid=(t // tm, n // tn),
        in_specs=[pl.BlockSpec((tm, d), lambda i, j: (i, 0)),
                  pl.BlockSpec((1, 6, d), lambda i, j: (i // tiles_per_seq, 0, 0)),
                  pl.BlockSpec((1, d), lambda i, j: (0, 0)),
                  pl.BlockSpec((d, tn), lambda i, j: (0, j))],
        out_specs=pl.BlockSpec((tm, tn), lambda i, j: (i, j)),
        out_shape=jax.ShapeDtypeStruct((t, n), BF16),
        scratch_shapes=[pltpu.VMEM((tm, d), BF16)],
        compiler_params=_params(2), name="inproj")(x2, mod3, norm_w, w_in_b)


def _log_sigmoid(x):
    return jnp.minimum(x, 0.0) - jnp.log1p(jnp.exp(-jnp.abs(x)))


def _ret_kernel(q_ref, k_ref, v_ref, g_ref, cos_ref, sin_ref, df_ref, db_ref, nw_ref, o_ref,
                mask_ref, qf_ref, qb_ref, kf_ref, kb_ref, dec_ref, qr_ref, kr_ref, acc_ref, *, chunk):
    seq, d = q_ref.shape
    n = seq // chunk
    c = chunk

    @pl.when(pl.program_id(1) == 0)
    def _():
        lgf = _log_sigmoid(df_ref[0])
        lgb = _log_sigmoid(db_ref[0])
        ii = lax.broadcasted_iota(I32, (c, c), 0)
        jj = lax.broadcasted_iota(I32, (c, c), 1)
        diff = (ii - jj).astype(F32)
        lgf_c = jnp.concatenate([lgf] * (c // d), axis=1)
        lgb_c = jnp.concatenate([lgb] * (c // d), axis=1)
        mask_ref[...] = jnp.where(diff >= 0.0,
                                  jnp.exp(lgf_c * jnp.maximum(diff, 0.0)),
                                  jnp.exp(lgb_c * jnp.maximum(-diff, 0.0)))
        pos = lax.broadcasted_iota(I32, (c, d), 0).astype(F32)
        qf_ref[...] = jnp.exp(lgf * (pos + 1.0))
        qb_ref[...] = jnp.exp(lgb * (c - pos))
        kf_ref[...] = jnp.exp(lgf * (c - 1.0 - pos))
        kb_ref[...] = jnp.exp(lgb * pos)
        dec_ref[0:1, :] = jnp.exp(lgf * c)
        dec_ref[1:2, :] = jnp.exp(lgb * c)

    scale = d ** -0.5
    nt = (((1,), (1,)), ((), ()))
    sls = [pl.ds(i * c, c) for i in range(n)]

    for sl in sls:
        cs = cos_ref[sl, :]
        sn = sin_ref[sl, :]
        q = q_ref[sl, :].astype(F32)
        k = k_ref[sl, :].astype(F32)
        qr_ref[sl, :] = q * cs + pltpu.roll(q, d // 2, 1) * sn
        kr_ref[sl, :] = (k * cs + pltpu.roll(k, d // 2, 1) * sn) * scale

    for sl in sls:
        s = lax.dot_general(qr_ref[sl, :].astype(BF16), kr_ref[sl, :].astype(BF16), nt,
                            preferred_element_type=F32)
        p = (s * mask_ref[...]).astype(BF16)
        acc_ref[sl, :] = jnp.dot(p, v_ref[sl, :], preferred_element_type=F32)

    def kv_state(sl, kw_ref):
        kw = (kr_ref[sl, :] * kw_ref[...]).T.astype(BF16)
        return jnp.dot(kw, v_ref[sl, :], preferred_element_type=F32)

    def scan(order, qw_ref, kw_ref, dec):
        st = jnp.zeros((d, d), F32)
        for idx, ci in enumerate(order):
            sl = sls[ci]
            if idx > 0:
                qw = (qr_ref[sl, :] * qw_ref[...]).astype(BF16)
                acc_ref[sl, :] += jnp.dot(qw, st.astype(BF16), preferred_element_type=F32)
            if idx < n - 1:
                st = dec * st + kv_state(sl, kw_ref)

    scan(list(range(n)), qf_ref, kf_ref, dec_ref[0:1, :])
    scan(list(range(n - 1, -1, -1)), qb_ref, kb_ref, dec_ref[1:2, :])

    nw = nw_ref[0]
    for sl in sls:
        o = acc_ref[sl, :]
        on = o * lax.rsqrt(jnp.mean(o * o, axis=-1, keepdims=True) + EPS) * nw
        g = g_ref[sl, :].astype(F32)
        o_ref[sl, :] = (_silu(g) * on).astype(o_ref.dtype)


def _retention(proj, cos, sin, dec_f, dec_b, ret_norm_w, batch, seq):
    h, d = RET_HEADS, HEAD_DIM
    c = min(RET_CHUNK, seq)
    col = lambda off: pl.BlockSpec((seq, d), lambda hh, b: (b, off + hh))
    per_head = pl.BlockSpec((1, 1, d), lambda hh, b: (hh, 0, 0))
    table = pl.BlockSpec((seq, d), lambda hh, b: (0, 0))
    vm = lambda shape, dt=F32: pltpu.VMEM(shape, dt)
    return pl.pallas_call(
        functools.partial(_ret_kernel, chunk=c), grid=(h, batch),
        in_specs=[col(0), col(h), col(2 * h), col(3 * h), table, table, per_head, per_head, per_head],
        out_specs=pl.BlockSpec((seq, d), lambda hh, b: (b, hh)),
        out_shape=jax.ShapeDtypeStruct((batch * seq, h * d), BF16),
        scratch_shapes=[vm((c, c)), vm((c, d)), vm((c, d)), vm((c, d)), vm((c, d)), vm((8, d)),
                        vm((seq, d)), vm((seq, d)), vm((seq, d))],
        compiler_params=_params(2), name="retention",
    )(proj, proj, proj, proj, cos, sin, dec_f, dec_b, ret_norm_w)


def _mix_kernel(og_ref, cb_ref, cc_ref, cu_ref, ccp_ref, cup_ref, ccn_ref, cun_ref, gr_ref, gc_ref,
                cw_ref, wr_ref, wc_ref, o_ref, z_ref, *, tiles_per_seq):
    i = pl.program_id(0)

    @pl.when(pl.program_id(1) == 0)
    def _():
        tm = cc_ref.shape[0]
        u = cc_ref[...].astype(F32) * cu_ref[...].astype(F32)
        pos = i % tiles_per_seq
        last = BF16_SUBLANES - 1
        u_before = ccp_ref[last:last + 1, :].astype(F32) * cup_ref[last:last + 1, :].astype(F32)
        u_before = jnp.where(pos == 0, 0.0, u_before)
        u_after = ccn_ref[0:1, :].astype(F32) * cun_ref[0:1, :].astype(F32)
        u_after = jnp.where(pos == tiles_per_seq - 1, 0.0, u_after)
        row = lax.broadcasted_iota(I32, u.shape, 0)
        u_prev = jnp.where(row == 0, u_before, pltpu.roll(u, 1, 0))
        u_next = jnp.where(row == tm - 1, u_after, pltpu.roll(u, tm - 1, 0))
        cw = cw_ref[...]
        y = cw[0:1, :] * u_prev + cw[1:2, :] * u + cw[2:3, :] * u_next
        z_ref[...] = (cb_ref[...].astype(F32) * y).astype(BF16)

    yr = jnp.dot(og_ref[...], wr_ref[...], preferred_element_type=F32)
    yc = jnp.dot(z_ref[...], wc_ref[...], preferred_element_type=F32)
    merged = _sigmoid(gr_ref[...].astype(F32)) * yr + _sigmoid(gc_ref[...].astype(F32)) * yc
    o_ref[...] = merged.astype(o_ref.dtype)


def _mix(og, proj, conv_w, w_ret_o_b, w_conv_o_b, seq):
    t, dr = og.shape
    d = w_ret_o_b.shape[1]
    tm = min(512, seq)
    tn = dr
    tiles_per_seq = seq // tm
    hb = tm // BF16_SUBLANES
    n_hblk = t // BF16_SUBLANES
    wide = lambda off: pl.BlockSpec((tm, dr), lambda i, j: (i, off))
    before = lambda off: pl.BlockSpec((BF16_SUBLANES, dr), lambda i, j: (jnp.maximum(i * hb - 1, 0), off))
    after = lambda off: pl.BlockSpec((BF16_SUBLANES, dr), lambda i, j: (jnp.minimum((i + 1) * hb, n_hblk - 1), off))
    gate = lambda off: pl.BlockSpec((tm, tn), lambda i, j: (i, off + j))
    return pl.pallas_call(
        functools.partial(_mix_kernel, tiles_per_seq=tiles_per_seq), grid=(t // tm, d // tn),
        in_specs=[pl.BlockSpec((tm, dr), lambda i, j: (i, 0)),
                  wide(4), wide(5), wide(6), before(5), before(6), after(5), after(6),
                  gate(7), gate(7 + d // tn),
                  pl.BlockSpec((3, dr), lambda i, j: (0, 0)),
                  pl.BlockSpec((dr, tn), lambda i, j: (0, j)),
                  pl.BlockSpec((dr, tn), lambda i, j: (0, j))],
        out_specs=pl.BlockSpec((tm, tn), lambda i, j: (i, j)),
        out_shape=jax.ShapeDtypeStruct((t, d), BF16),
        scratch_shapes=[pltpu.VMEM((tm, dr), BF16)],
        compiler_params=_params(2), name="mix",
    )(og, proj, proj, proj, proj, proj, proj, proj, proj, proj, conv_w, w_ret_o_b, w_conv_o_b)


def _outproj_kernel(m_ref, x_ref, mod_ref, nw_ref, w_ref, rwt_ref, x1_ref, hp_ref, lg_ref):
    y = jnp.dot(m_ref[...], w_ref[...], preferred_element_type=F32)
    m = mod_ref[0]
    x1 = x_ref[...] + m[2:3, :] * y
    x1_ref[...] = x1
    hn = x1 * lax.rsqrt(jnp.mean(x1 * x1, axis=-1, keepdims=True) + EPS) * nw_ref[...]
    h = hn * (1.0 + m[4:5, :]) + m[3:4, :]
    hp_ref[...] = _pack_halves(h)
    h_hi = h.astype(BF16)
    h_lo = (h - h_hi.astype(F32)).astype(BF16)
    rw = rwt_ref[...]
    r_hi = rw.astype(BF16)
    r_lo = (rw - r_hi.astype(F32)).astype(BF16)
    nt = (((1,), (1,)), ((), ()))
    dot = functools.partial(lax.dot_general, dimension_numbers=nt, preferred_element_type=F32)
    lg_ref[...] = dot(r_hi, h_hi) + dot(r_hi, h_lo) + dot(r_lo, h_hi)


def _outproj(merged, x2, mod3, norm_w, w_out_b, router_wt, seq):
    t, d = x2.shape
    e = router_wt.shape[0]
    tm = min(512, seq)
    tiles_per_seq = seq // tm
    return pl.pallas_call(
        _outproj_kernel, grid=(t // tm,),
        in_specs=[pl.BlockSpec((tm, d), lambda i: (i, 0)),
                  pl.BlockSpec((tm, d), lambda i: (i, 0)),
                  pl.BlockSpec((1, 6, d), lambda i: (i // tiles_per_seq, 0, 0)),
                  pl.BlockSpec((1, d), lambda i: (0, 0)),
                  pl.BlockSpec((d, d), lambda i: (0, 0)),
                  pl.BlockSpec((e, d), lambda i: (0, 0))],
        out_specs=[pl.BlockSpec((tm, d), lambda i: (i, 0)),
                   pl.BlockSpec((tm, d // 2), lambda i: (i, 0)),
                   pl.BlockSpec((e, tm), lambda i: (0, i))],
        out_shape=[jax.ShapeDtypeStruct((t, d), F32),
                   jax.ShapeDtypeStruct((t, d // 2), jnp.uint32),
                   jax.ShapeDtypeStruct((e, t), F32)],
        compiler_params=_params(1), name="outproj",
    )(merged, x2, mod3, norm_w, w_out_b, router_wt)


def _first_max(x, iota, sentinel):
    m = jnp.max(x, axis=0, keepdims=True)
    idx = jnp.min(jnp.where(x == m, iota, sentinel), axis=0, keepdims=True)
    return m, idx, iota == idx


def _route_kernel(lg_ref, bias_ref, ids_ref, w_ref, rank_ref, cnt_ref, carry_ref):
    i = pl.program_id(0)
    e, tk = lg_ref.shape

    @pl.when(i == 0)
    def _():
        carry_ref[...] = jnp.zeros_like(carry_ref)

    s = _sigmoid(lg_ref[...])
    biased = s + bias_ref[...]
    sub = lax.broadcasted_iota(I32, (GROUP_SIZE, tk), 0)
    group_rows = []
    for g in range(N_GROUPS):
        xg = biased[g * GROUP_SIZE:(g + 1) * GROUP_SIZE, :]
        m1, _, pick = _first_max(xg, sub, GROUP_SIZE)
        m2 = jnp.max(jnp.where(pick, NEG_INF, xg), axis=0, keepdims=True)
        group_rows.append(m1 + m2)
    gs = jnp.concatenate(group_rows, axis=0)
    gsub = lax.broadcasted_iota(I32, (N_GROUPS, tk), 0)
    sel = jnp.zeros((N_GROUPS, tk), F32)
    for _ in range(TOPK_GROUPS):
        _, _, pick = _first_max(gs, gsub, N_GROUPS)
        sel = jnp.where(pick, 1.0, sel)
        gs = jnp.where(pick, NEG_INF, gs)
    masked_rows = []
    for g in range(N_GROUPS):
        xg = biased[g * GROUP_SIZE:(g + 1) * GROUP_SIZE, :]
        masked_rows.append(jnp.where(sel[g:g + 1, :] > 0.5, xg, NEG_INF))
    masked = jnp.concatenate(masked_rows, axis=0)

    eio = lax.broadcasted_iota(I32, (e, tk), 0)
    chosen = jnp.zeros((e, tk), F32)
    ids, top_s = [], []
    for _ in range(TOP_K):
        _, idx, pick = _first_max(masked, eio, e)
        ids.append(idx)
        top_s.append(jnp.sum(jnp.where(pick, s, 0.0), axis=0, keepdims=True))
        chosen = jnp.where(pick, 1.0, chosen)
        masked = jnp.where(pick, NEG_INF, masked)
    total = top_s[0]
    for ts in top_s[1:]:
        total = total + ts

    before = (lax.broadcasted_iota(I32, (tk, tk), 0) < lax.broadcasted_iota(I32, (tk, tk), 1))
    upper = jnp.where(before, 1.0, 0.0).astype(BF16)
    rank = jnp.dot(chosen.astype(BF16), upper, preferred_element_type=F32) + carry_ref[:, 0:1]
    for k in range(TOP_K):
        ids_ref[k:k + 1, :] = ids[k]
        w_ref[k:k + 1, :] = top_s[k] / total * ROUTED_SCALE
        rk = jnp.sum(jnp.where(eio == ids[k], rank, 0.0), axis=0, keepdims=True)
        rank_ref[k:k + 1, :] = rk.astype(I32)
    carry_ref[...] = carry_ref[...] + jnp.sum(chosen, axis=1, keepdims=True)
    cnt_ref[...] = carry_ref[...].astype(I32)


def _route(logits_t, bias_col):
    e, t = logits_t.shape
    tk = min(512, t)
    row8 = lambda dt: jax.ShapeDtypeStruct((TOP_K, t), dt)
    blk8 = pl.BlockSpec((TOP_K, tk), lambda i: (0, i))
    return pl.pallas_call(
        _route_kernel, grid=(t // tk,),
        in_specs=[pl.BlockSpec((e, tk), lambda i: (0, i)),
                  pl.BlockSpec((e, 1), lambda i: (0, 0))],
        out_specs=[blk8, blk8, blk8, pl.BlockSpec((e, 128), lambda i: (0, 0))],
        out_shape=[row8(I32), row8(F32), row8(I32), jax.ShapeDtypeStruct((e, 128), I32)],
        scratch_shapes=[pltpu.VMEM((e, 128), F32)],
        compiler_params=_params(1), name="route",
    )(logits_t, bias_col)


def _dispatch_kernel(pad_ref, cnt_ref, hp_ref, ids_ref, rk_ref, xs_ref, zero_ref, sem, zsem, *, bm):
    i = pl.program_id(0)
    td = hp_ref.shape[0]
    n_blocks = xs_ref.shape[0] // bm

    @pl.when(i == 0)
    def _():
        zero_ref[...] = jnp.zeros_like(zero_ref)
        last_e = N_EXPERTS - 1
        used_blocks = pad_ref[last_e] // bm + (cnt_ref[last_e] + bm - 1) // bm

        def zero_block(j):
            rows = pl.ds(pl.multiple_of(j * bm, bm), bm)
            return pltpu.make_async_copy(zero_ref, xs_ref.at[rows], zsem)

        def block_start(j, c):
            zero_block(j).start()
            return c

        def block_wait(j, c):
            zero_block(j).wait()
            return c

        lax.fori_loop(used_blocks, n_blocks, block_start, 0)
        lax.fori_loop(used_blocks, n_blocks, block_wait, 0)

        def zero_row(row):
            return pltpu.make_async_copy(zero_ref.at[pl.ds(0, 1)], xs_ref.at[pl.ds(row, 1)], zsem)

        def per_expert(e, c):
            first = pad_ref[e] + cnt_ref[e]
            last = pad_ref[e] + (cnt_ref[e] + bm - 1) // bm * bm

            def start_body(row, cc):
                zero_row(row).start()
                return cc

            def wait_body(row, cc):
                zero_row(row).wait()
                return cc

            lax.fori_loop(first, last, start_body, 0)
            lax.fori_loop(first, last, wait_body, 0)
            return c

        lax.fori_loop(0, N_EXPERTS, per_expert, 0)

    def body(t, c):
        for k in range(TOP_K):
            slot = pad_ref[ids_ref[k, t]] + rk_ref[k, t]
            pltpu.make_async_copy(hp_ref.at[pl.ds(t, 1)], xs_ref.at[pl.ds(slot, 1)], sem).start()
        return c

    lax.fori_loop(0, td, body, 0)
    for k in range(TOP_K):
        pltpu.make_async_copy(hp_ref, xs_ref.at[pl.ds(0, td)], sem).wait()


def _dispatch(pad_start, counts, hp, ids_t, rank_t, n_rows, bm):
    t, half = hp.shape
    td = min(256, t)
    smem8 = pl.BlockSpec((TOP_K, td), lambda i, *_: (0, i), memory_space=pltpu.SMEM)
    grid_spec = pltpu.PrefetchScalarGridSpec(
        num_scalar_prefetch=2, grid=(t // td,),
        in_specs=[pl.BlockSpec((td, half), lambda i, *_: (i, 0)), smem8, smem8],
        out_specs=pl.BlockSpec(memory_space=pl.ANY),
        scratch_shapes=[pltpu.VMEM((bm, half), jnp.uint32),
                        pltpu.SemaphoreType.DMA(()), pltpu.SemaphoreType.DMA(())])
    return pl.pallas_call(
        functools.partial(_dispatch_kernel, bm=bm), grid_spec=grid_spec,
        out_shape=jax.ShapeDtypeStruct((n_rows, half), jnp.uint32),
        compiler_params=_params(1), name="dispatch",
    )(pad_start, counts, hp, ids_t, rank_t)


def _expert_kernel(be_ref, nu_ref, x_ref, wg_ref, wu_ref, wd_ref, o_ref):
    i = pl.program_id(0)

    @pl.when(i < nu_ref[0])
    def _():
        x = _unpack_halves(x_ref[...]).astype(BF16)
        g = jnp.dot(x, wg_ref[0], preferred_element_type=F32)
        u = jnp.dot(x, wu_ref[0], preferred_element_type=F32)
        mid = (_silu(g) * u).astype(BF16)
        o_ref[...] = _pack_halves(jnp.dot(mid, wd_ref[0], preferred_element_type=F32))

    @pl.when(i >= nu_ref[0])
    def _():
        o_ref[...] = jnp.zeros_like(o_ref)


def _experts(blk_e, n_used, xs, wg_b, wu_b, wd_b, n_blocks, bm):
    half = xs.shape[1]
    _, d, de = wg_b.shape
    x_map = lambda i, be, nu: (jnp.minimum(i, nu[0] - 1), 0)
    w_map = lambda i, be, nu: (be[i], 0, 0)
    grid_spec = pltpu.PrefetchScalarGridSpec(
        num_scalar_prefetch=2, grid=(n_blocks,),
        in_specs=[pl.BlockSpec((bm, half), x_map),
                  pl.BlockSpec((1, d, de), w_map),
                  pl.BlockSpec((1, d, de), w_map),
                  pl.BlockSpec((1, de, d), w_map)],
        out_specs=pl.BlockSpec((bm, half), lambda i, be, nu: (i, 0)))
    return pl.pallas_call(
        _expert_kernel, grid_spec=grid_spec,
        out_shape=jax.ShapeDtypeStruct((n_blocks * bm, half), jnp.uint32),
        compiler_params=_params(1), name="experts",
    )(blk_e, n_used, xs, wg_b, wu_b, wd_b)


def _combine_kernel(pad_ref, x1_ref, hp_ref, w_ref, mod_ref, fw_ref, wsg_ref, wsu_ref, wsd_ref,
                    idc_ref, rkc_ref, idn_ref, rkn_ref, ys_ref, o_ref, ybuf, sem, *, n_tiles):
    i = pl.program_id(0)
    tc = x1_ref.shape[0]

    def issue(ids_ref, rk_ref, buf):
        def body(t, c):
            for k in range(TOP_K):
                slot = pad_ref[ids_ref[k, t]] + rk_ref[k, t]
                pltpu.make_async_copy(ys_ref.at[pl.ds(slot, 1)], ybuf.at[buf, k, pl.ds(t, 1)],
                                      sem.at[buf]).start()
            return c
        lax.fori_loop(0, tc, body, 0)

    @pl.when(i == 0)
    def _():
        issue(idc_ref, rkc_ref, 0)

    @pl.when(i + 1 < n_tiles)
    def _():
        issue(idn_ref, rkn_ref, (i + 1) % 2)

    buf = i % 2
    for k in range(TOP_K):
        pltpu.make_async_copy(ys_ref.at[pl.ds(0, tc)], ybuf.at[buf, k], sem.at[buf]).wait()

    h = _unpack_halves(hp_ref[...]).astype(BF16)
    sg = jnp.dot(h, wsg_ref[...], preferred_element_type=F32)
    su = jnp.dot(h, wsu_ref[...], preferred_element_type=F32)
    shared = jnp.dot((_silu(sg) * su).astype(BF16), wsd_ref[...], preferred_element_type=F32)
    w = w_ref[...]
    routed = w[:, 0:1] * _unpack_halves(ybuf[buf, 0])
    for k in range(1, TOP_K):
        routed = routed + w[:, k:k + 1] * _unpack_halves(ybuf[buf, k])
    x = x1_ref[...] + mod_ref[0][5:6, :] * (routed + shared)
    o_ref[...] = x * lax.rsqrt(jnp.mean(x * x, axis=-1, keepdims=True) + EPS) * fw_ref[...]


def _combine(pad_start, x1, hp, w_tok, mod3, final_w, wsg_b, wsu_b, wsd_b, ids_t, rank_t, ys, seq):
    t, d = x1.shape
    half = d // 2
    ds = wsg_b.shape[1]
    tc = min(256, seq)
    n_tiles = t // tc
    tiles_per_seq = seq // tc
    cur8 = pl.BlockSpec((TOP_K, tc), lambda i, *_: (0, i), memory_space=pltpu.SMEM)
    nxt8 = pl.BlockSpec((TOP_K, tc), lambda i, *_: (0, jnp.minimum(i + 1, n_tiles - 1)),
                        memory_space=pltpu.SMEM)
    full = lambda shape: pl.BlockSpec(shape, lambda i, *_: (0,) * len(shape))
    grid_spec = pltpu.PrefetchScalarGridSpec(
        num_scalar_prefetch=1, grid=(n_tiles,),
        in_specs=[pl.BlockSpec((tc, d), lambda i, *_: (i, 0)),
                  pl.BlockSpec((tc, half), lambda i, *_: (i, 0)),
                  pl.BlockSpec((tc, TOP_K), lambda i, *_: (i, 0)),
                  pl.BlockSpec((1, 6, d), lambda i, *_: (i // tiles_per_seq, 0, 0)),
                  full((1, d)), full((d, ds)), full((d, ds)), full((ds, d)),
                  cur8, cur8, nxt8, nxt8,
                  pl.BlockSpec(memory_space=pl.ANY)],
        out_specs=pl.BlockSpec((tc, d), lambda i, *_: (i, 0)),
        scratch_shapes=[pltpu.VMEM((2, TOP_K, tc, half), jnp.uint32),
                        pltpu.SemaphoreType.DMA((2,))])
    return pl.pallas_call(
        functools.partial(_combine_kernel, n_tiles=n_tiles), grid_spec=grid_spec,
        out_shape=jax.ShapeDtypeStruct((t, d), F32),
        compiler_params=_params(1), name="combine",
    )(pad_start, x1, hp, w_tok, mod3, final_w, wsg_b, wsu_b, wsd_b, ids_t, rank_t, ids_t, rank_t, ys)


def _rope_tables(seq, d):
    half = d // 2
    inv = ROPE_BASE ** (-jnp.arange(half, dtype=F32) / half)
    ang = jnp.arange(seq, dtype=F32)[:, None] * inv[None, :]
    cos, sin = jnp.cos(ang), jnp.sin(ang)
    return jnp.concatenate([cos, cos], axis=1), jnp.concatenate([-sin, sin], axis=1)


def kernel(x, c, w_ada, b_ada, norm1_w, w_in, ret_decay_fwd, ret_decay_bwd, ret_norm_w, w_ret_o, conv_w,
           w_conv_o, w_out, norm2_w, router_w, router_bias, w_gate, w_up, w_down, ws_gate, ws_up, ws_down,
           final_norm_w):
    batch, seq, d = x.shape
    depth = w_ada.shape[0]
    t = batch * seq
    bm = EXPERT_BLOCK
    n_blocks = (t * TOP_K + N_EXPERTS * (bm - 1)) // bm
    n_rows = n_blocks * bm
    cos, sin = _rope_tables(seq, HEAD_DIM)
    c_pad = jnp.pad(c, ((0, BF16_SUBLANES - batch % BF16_SUBLANES), (0, 0)))
    x2 = x.reshape(t, d)

    assert depth == 1, "the final norm is fused into the last stage of a single layer"
    for l in range(depth):
        mod = _ada(c_pad, w_ada[l], b_ada[l][None, :])[:batch]
        mod3 = mod.reshape(batch, 6, d)
        proj = _inproj(x2, mod3, norm1_w[l][None, :], w_in[l].astype(BF16), seq)
        lane_bcast = lambda v: jnp.broadcast_to(v[:, None, None], (RET_HEADS, 1, HEAD_DIM))
        og = _retention(proj, cos, sin, lane_bcast(ret_decay_fwd[l]), lane_bcast(ret_decay_bwd[l]),
                        ret_norm_w[l].reshape(RET_HEADS, 1, HEAD_DIM), batch, seq)
        merged = _mix(og, proj, conv_w[l], w_ret_o[l].astype(BF16), w_conv_o[l].astype(BF16), seq)
        x1, hp, logits_t = _outproj(merged, x2, mod3, norm2_w[l][None, :], w_out[l].astype(BF16),
                                    router_w[l].T, seq)
        ids_t, w_t, rank_t, cnt = _route(logits_t, router_bias[l][:, None])

        counts = cnt[:, 0]
        nblk = (counts + bm - 1) // bm
        blk_end = jnp.cumsum(nblk)
        pad_start = ((blk_end - nblk) * bm).astype(I32)
        n_used = blk_end[-1:].astype(I32)
        blk_e = jnp.minimum(jnp.searchsorted(blk_end, jnp.arange(n_blocks, dtype=I32), side="right"),
                            N_EXPERTS - 1).astype(I32)

        xs = _dispatch(pad_start, counts, hp, ids_t, rank_t, n_rows, bm)
        ys = _experts(blk_e, n_used, xs, w_gate[l].astype(BF16), w_up[l].astype(BF16),
                      w_down[l].astype(BF16), n_blocks, bm)
        x2 = _combine(pad_start, x1, hp, w_t.T, mod3, final_norm_w[None, :], ws_gate[l].astype(BF16),
                      ws_up[l].astype(BF16), ws_down[l].astype(BF16), ids_t, rank_t, ys, seq)
    return x2.reshape(batch, seq, d)
```

```python
import functools

import jax
import jax.numpy as jnp
from jax import lax
from jax.experimental import pallas as pl
from jax.experimental.pallas import tpu as pltpu

F32 = jnp.float32
BF16 = jnp.bfloat16
I32 = jnp.int32

EPS = 1e-6
RET_HEADS = 8
HEAD_DIM = 128
ROPE_BASE = 10000.0
N_EXPERTS = 64
TOP_K = 8
N_GROUPS = 8
TOPK_GROUPS = 4
GROUP_SIZE = N_EXPERTS // N_GROUPS
ROUTED_SCALE = 2.5

V7X_VMEM_BYTES = 64 * 1024 * 1024
VMEM_LIMIT = V7X_VMEM_BYTES - 8 * 1024 * 1024
BF16_SUBLANES = 16

RET_CHUNK = 512
EXPERT_BLOCK = 256
NEG_INF = float("-inf")


def _params(n_axes):
    return pltpu.CompilerParams(dimension_semantics=("arbitrary",) * n_axes,
                                vmem_limit_bytes=VMEM_LIMIT)


def _sigmoid(x):
    return 1.0 / (1.0 + jnp.exp(-x))


def _silu(x):
    return x * _sigmoid(x)


ROWS_PER_TOKEN = 2


def _to_token_rows(x):
    m = x.shape[1] // 2
    packed = pltpu.pack_elementwise([x[:, :m], x[:, m:]], packed_dtype=BF16)
    return pltpu.bitcast(packed, BF16)


def _from_token_rows(rows):
    p = pltpu.bitcast(rows, jnp.uint32)
    a = pltpu.unpack_elementwise(p, index=0, packed_dtype=BF16, unpacked_dtype=F32)
    b = pltpu.unpack_elementwise(p, index=1, packed_dtype=BF16, unpacked_dtype=F32)
    return jnp.concatenate([a, b], axis=1)


def _token_slice(row):
    return pl.ds(pl.multiple_of(row * ROWS_PER_TOKEN, ROWS_PER_TOKEN), ROWS_PER_TOKEN)


def _ada_kernel(c_ref, w_ref, b_ref, o_ref):
    s = _silu(c_ref[...]).astype(BF16)
    o_ref[...] = jnp.dot(s, w_ref[...].astype(BF16), preferred_element_type=F32) + b_ref[...]


def _ada(c_pad, w_ada, b_ada):
    m, d = c_pad.shape
    n = w_ada.shape[1]
    tn = 1024
    return pl.pallas_call(
        _ada_kernel, grid=(n // tn,),
        in_specs=[pl.BlockSpec((m, d), lambda j: (0, 0)),
                  pl.BlockSpec((d, tn), lambda j: (0, j)),
                  pl.BlockSpec((1, tn), lambda j: (0, j))],
        out_specs=pl.BlockSpec((m, tn), lambda j: (0, j)),
        out_shape=jax.ShapeDtypeStruct((m, n), F32),
        compiler_params=_params(1), name="ada")(c_pad, w_ada, b_ada)


def _inproj_kernel(x_ref, mod_ref, nw_ref, w_ref, o_ref, h_ref):
    @pl.when(pl.program_id(1) == 0)
    def _():
        x = x_ref[...]
        y = x * lax.rsqrt(jnp.mean(x * x, axis=-1, keepdims=True) + EPS) * nw_ref[...]
        m = mod_ref[0]
        h_ref[...] = (y * (1.0 + m[1:2, :]) + m[0:1, :]).astype(BF16)

    o_ref[...] = jnp.dot(h_ref[...], w_ref[...], preferred_element_type=F32).astype(o_ref.dtype)


def _inproj(x2, mod3, norm_w, w_in_b, seq):
    t, d = x2.shape
    n = w_in_b.shape[1]
    tm, tn = min(1024, seq), 1024
    tiles_per_seq = seq // tm
    return pl.pallas_call(
        _inproj_kernel, grid=(t // tm, n // tn),
        in_specs=[pl.BlockSpec((tm, d), lambda i, j: (i, 0)),
                  pl.BlockSpec((1, 6, d), lambda i, j: (i // tiles_per_seq, 0, 0)),
                  pl.BlockSpec((1, d), lambda i, j: (0, 0)),
                  pl.BlockSpec((d, tn), lambda i, j: (0, j))],
        out_specs=pl.BlockSpec((tm, tn), lambda i, j: (i, j)),
        out_shape=jax.ShapeDtypeStruct((t, n), BF16),
        scratch_shapes=[pltpu.VMEM((tm, d), BF16)],
        compiler_params=_params(2), name="inproj")(x2, mod3, norm_w, w_in_b)


def _log_sigmoid(x):
    return jnp.minimum(x, 0.0) - jnp.log1p(jnp.exp(-jnp.abs(x)))


def _ret_kernel(q_ref, k_ref, v_ref, g_ref, cos_ref, sin_ref, df_ref, db_ref, nw_ref, o_ref,
                mask_ref, qf_ref, qb_ref, kf_ref, kb_ref, dec_ref, qr_ref, kr_ref, acc_ref, *, chunk):
    seq, d = q_ref.shape
    n = seq // chunk
    c = chunk

    @pl.when(pl.program_id(1) == 0)
    def _():
        lgf = _log_sigmoid(df_ref[0])
        lgb = _log_sigmoid(db_ref[0])
        ii = lax.broadcasted_iota(I32, (c, c), 0)
        jj = lax.broadcasted_iota(I32, (c, c), 1)
        diff = (ii - jj).astype(F32)
        lgf_c = jnp.concatenate([lgf] * (c // d), axis=1)
        lgb_c = jnp.concatenate([lgb] * (c // d), axis=1)
        mask_ref[...] = jnp.where(diff >= 0.0,
                                  jnp.exp(lgf_c * jnp.maximum(diff, 0.0)),
                                  jnp.exp(lgb_c * jnp.maximum(-diff, 0.0)))
        pos = lax.broadcasted_iota(I32, (c, d), 0).astype(F32)
        qf_ref[...] = jnp.exp(lgf * (pos + 1.0))
        qb_ref[...] = jnp.exp(lgb * (c - pos))
        kf_ref[...] = jnp.exp(lgf * (c - 1.0 - pos))
        kb_ref[...] = jnp.exp(lgb * pos)
        dec_ref[0:1, :] = jnp.exp(lgf * c)
        dec_ref[1:2, :] = jnp.exp(lgb * c)

    scale = d ** -0.5
    nt = (((1,), (1,)), ((), ()))
    sls = [pl.ds(i * c, c) for i in range(n)]

    for sl in sls:
        cs = cos_ref[sl, :]
        sn = sin_ref[sl, :]
        q = q_ref[sl, :].astype(F32)
        k = k_ref[sl, :].astype(F32)
        qr_ref[sl, :] = q * cs + pltpu.roll(q, d // 2, 1) * sn
        kr_ref[sl, :] = (k * cs + pltpu.roll(k, d // 2, 1) * sn) * scale

    for sl in sls:
        s = lax.dot_general(qr_ref[sl, :].astype(BF16), kr_ref[sl, :].astype(BF16), nt,
                            preferred_element_type=F32)
        p = (s * mask_ref[...]).astype(BF16)
        acc_ref[sl, :] = jnp.dot(p, v_ref[sl, :], preferred_element_type=F32)

    def kv_state(sl, kw_ref):
        kw = (kr_ref[sl, :] * kw_ref[...]).T.astype(BF16)
        return jnp.dot(kw, v_ref[sl, :], preferred_element_type=F32)

    def scan(order, qw_ref, kw_ref, dec):
        st = jnp.zeros((d, d), F32)
        for idx, ci in enumerate(order):
            sl = sls[ci]
            if idx > 0:
                qw = (qr_ref[sl, :] * qw_ref[...]).astype(BF16)
                acc_ref[sl, :] += jnp.dot(qw, st.astype(BF16), preferred_element_type=F32)
            if idx < n - 1:
                st = dec * st + kv_state(sl, kw_ref)

    scan(list(range(n)), qf_ref, kf_ref, dec_ref[0:1, :])
    scan(list(range(n - 1, -1, -1)), qb_ref, kb_ref, dec_ref[1:2, :])

    nw = nw_ref[0]
    for sl in sls:
        o = acc_ref[sl, :]
        on = o * lax.rsqrt(jnp.mean(o * o, axis=-1, keepdims=True) + EPS) * nw
        g = g_ref[sl, :].astype(F32)
        o_ref[sl, :] = (_silu(g) * on).astype(o_ref.dtype)


def _retention(proj, cos, sin, dec_f, dec_b, ret_norm_w, batch, seq):
    h, d = RET_HEADS, HEAD_DIM
    c = min(RET_CHUNK, seq)
    col = lambda off: pl.BlockSpec((seq, d), lambda hh, b: (b, off + hh))
    per_head = pl.BlockSpec((1, 1, d), lambda hh, b: (hh, 0, 0))
    table = pl.BlockSpec((seq, d), lambda hh, b: (0, 0))
    vm = lambda shape, dt=F32: pltpu.VMEM(shape, dt)
    return pl.pallas_call(
        functools.partial(_ret_kernel, chunk=c), grid=(h, batch),
        in_specs=[col(0), col(h), col(2 * h), col(3 * h), table, table, per_head, per_head, per_head],
        out_specs=pl.BlockSpec((seq, d), lambda hh, b: (b, hh)),
        out_shape=jax.ShapeDtypeStruct((batch * seq, h * d), BF16),
        scratch_shapes=[vm((c, c)), vm((c, d)), vm((c, d)), vm((c, d)), vm((c, d)), vm((8, d)),
                        vm((seq, d)), vm((seq, d)), vm((seq, d))],
        compiler_params=_params(2), name="retention",
    )(proj, proj, proj, proj, cos, sin, dec_f, dec_b, ret_norm_w)


def _mix_kernel(og_ref, cb_ref, cc_ref, cu_ref, ccp_ref, cup_ref, ccn_ref, cun_ref, gr_ref, gc_ref,
                cw_ref, wr_ref, wc_ref, o_ref, z_ref, *, tiles_per_seq):
    i = pl.program_id(0)

    @pl.when(pl.program_id(1) == 0)
    def _():
        tm = cc_ref.shape[0]
        u = cc_ref[...].astype(F32) * cu_ref[...].astype(F32)
        pos = i % tiles_per_seq
        last = BF16_SUBLANES - 1
        u_before = ccp_ref[last:last + 1, :].astype(F32) * cup_ref[last:last + 1, :].astype(F32)
        u_before = jnp.where(pos == 0, 0.0, u_before)
        u_after = ccn_ref[0:1, :].astype(F32) * cun_ref[0:1, :].astype(F32)
        u_after = jnp.where(pos == tiles_per_seq - 1, 0.0, u_after)
        row = lax.broadcasted_iota(I32, u.shape, 0)
        u_prev = jnp.where(row == 0, u_before, pltpu.roll(u, 1, 0))
        u_next = jnp.where(row == tm - 1, u_after, pltpu.roll(u, tm - 1, 0))
        cw = cw_ref[...]
        y = cw[0:1, :] * u_prev + cw[1:2, :] * u + cw[2:3, :] * u_next
        z_ref[...] = (cb_ref[...].astype(F32) * y).astype(BF16)

    yr = jnp.dot(og_ref[...], wr_ref[...], preferred_element_type=F32)
    yc = jnp.dot(z_ref[...], wc_ref[...], preferred_element_type=F32)
    merged = _sigmoid(gr_ref[...].astype(F32)) * yr + _sigmoid(gc_ref[...].astype(F32)) * yc
    o_ref[...] = merged.astype(o_ref.dtype)


def _mix(og, proj, conv_w, w_ret_o_b, w_conv_o_b, seq):
    t, dr = og.shape
    d = w_ret_o_b.shape[1]
    tm = min(512, seq)
    tn = dr
    tiles_per_seq = seq // tm
    hb = tm // BF16_SUBLANES
    n_hblk = t // BF16_SUBLANES
    wide = lambda off: pl.BlockSpec((tm, dr), lambda i, j: (i, off))
    before = lambda off: pl.BlockSpec((BF16_SUBLANES, dr), lambda i, j: (jnp.maximum(i * hb - 1, 0), off))
    after = lambda off: pl.BlockSpec((BF16_SUBLANES, dr), lambda i, j: (jnp.minimum((i + 1) * hb, n_hblk - 1), off))
    gate = lambda off: pl.BlockSpec((tm, tn), lambda i, j: (i, off + j))
    return pl.pallas_call(
        functools.partial(_mix_kernel, tiles_per_seq=tiles_per_seq), grid=(t // tm, d // tn),
        in_specs=[pl.BlockSpec((tm, dr), lambda i, j: (i, 0)),
                  wide(4), wide(5), wide(6), before(5), before(6), after(5), after(6),
                  gate(7), gate(7 + d // tn),
                  pl.BlockSpec((3, dr), lambda i, j: (0, 0)),
                  pl.BlockSpec((dr, tn), lambda i, j: (0, j)),
                  pl.BlockSpec((dr, tn), lambda i, j: (0, j))],
        out_specs=pl.BlockSpec((tm, tn), lambda i, j: (i, j)),
        out_shape=jax.ShapeDtypeStruct((t, d), BF16),
        scratch_shapes=[pltpu.VMEM((tm, dr), BF16)],
        compiler_params=_params(2), name="mix",
    )(og, proj, proj, proj, proj, proj, proj, proj, proj, proj, conv_w, w_ret_o_b, w_conv_o_b)


def _outproj_kernel(m_ref, x_ref, mod_ref, nw_ref, w_ref, rwt_ref, x1_ref, hp_ref, lg_ref):
    y = jnp.dot(m_ref[...], w_ref[...], preferred_element_type=F32)
    m = mod_ref[0]
    x1 = x_ref[...] + m[2:3, :] * y
    x1_ref[...] = x1
    hn = x1 * lax.rsqrt(jnp.mean(x1 * x1, axis=-1, keepdims=True) + EPS) * nw_ref[...]
    h = hn * (1.0 + m[4:5, :]) + m[3:4, :]
    hp_ref[...] = _to_token_rows(h)
    h_hi = h.astype(BF16)
    h_lo = (h - h_hi.astype(F32)).astype(BF16)
    rw = rwt_ref[...]
    r_hi = rw.astype(BF16)
    r_lo = (rw - r_hi.astype(F32)).astype(BF16)
    nt = (((1,), (1,)), ((), ()))
    dot = functools.partial(lax.dot_general, dimension_numbers=nt, preferred_element_type=F32)
    lg_ref[...] = dot(r_hi, h_hi) + dot(r_hi, h_lo) + dot(r_lo, h_hi)


def _outproj(merged, x2, mod3, norm_w, w_out_b, router_wt, seq):
    t, d = x2.shape
    e = router_wt.shape[0]
    tm = min(512, seq)
    tiles_per_seq = seq // tm
    return pl.pallas_call(
        _outproj_kernel, grid=(t // tm,),
        in_specs=[pl.BlockSpec((tm, d), lambda i: (i, 0)),
                  pl.BlockSpec((tm, d), lambda i: (i, 0)),
                  pl.BlockSpec((1, 6, d), lambda i: (i // tiles_per_seq, 0, 0)),
                  pl.BlockSpec((1, d), lambda i: (0, 0)),
                  pl.BlockSpec((d, d), lambda i: (0, 0)),
                  pl.BlockSpec((e, d), lambda i: (0, 0))],
        out_specs=[pl.BlockSpec((tm, d), lambda i: (i, 0)),
                   pl.BlockSpec((ROWS_PER_TOKEN * tm, d // 2), lambda i: (i, 0)),
                   pl.BlockSpec((e, tm), lambda i: (0, i))],
        out_shape=[jax.ShapeDtypeStruct((t, d), F32),
                   jax.ShapeDtypeStruct((ROWS_PER_TOKEN * t, d // 2), BF16),
                   jax.ShapeDtypeStruct((e, t), F32)],
        compiler_params=_params(1), name="outproj",
    )(merged, x2, mod3, norm_w, w_out_b, router_wt)


def _first_max(x, iota, sentinel):
    m = jnp.max(x, axis=0, keepdims=True)
    idx = jnp.min(jnp.where(x == m, iota, sentinel), axis=0, keepdims=True)
    return m, idx, iota == idx


def _route_kernel(lg_ref, bias_ref, ids_ref, w_ref, rank_ref, cnt_ref, carry_ref):
    i = pl.program_id(0)
    e, tk = lg_ref.shape

    @pl.when(i == 0)
    def _():
        carry_ref[...] = jnp.zeros_like(carry_ref)

    s = _sigmoid(lg_ref[...])
    biased = s + bias_ref[...]
    sub = lax.broadcasted_iota(I32, (GROUP_SIZE, tk), 0)
    group_rows = []
    for g in range(N_GROUPS):
        xg = biased[g * GROUP_SIZE:(g + 1) * GROUP_SIZE, :]
        m1, _, pick = _first_max(xg, sub, GROUP_SIZE)
        m2 = jnp.max(jnp.where(pick, NEG_INF, xg), axis=0, keepdims=True)
        group_rows.append(m1 + m2)
    gs = jnp.concatenate(group_rows, axis=0)
    gsub = lax.broadcasted_iota(I32, (N_GROUPS, tk), 0)
    sel = jnp.zeros((N_GROUPS, tk), F32)
    for _ in range(TOPK_GROUPS):
        _, _, pick = _first_max(gs, gsub, N_GROUPS)
        sel = jnp.where(pick, 1.0, sel)
        gs = jnp.where(pick, NEG_INF, gs)
    masked_rows = []
    for g in range(N_GROUPS):
        xg = biased[g * GROUP_SIZE:(g + 1) * GROUP_SIZE, :]
        masked_rows.append(jnp.where(sel[g:g + 1, :] > 0.5, xg, NEG_INF))
    masked = jnp.concatenate(masked_rows, axis=0)

    eio = lax.broadcasted_iota(I32, (e, tk), 0)
    chosen = jnp.zeros((e, tk), F32)
    ids, top_s = [], []
    for _ in range(TOP_K):
        _, idx, pick = _first_max(masked, eio, e)
        ids.append(idx)
        top_s.append(jnp.sum(jnp.where(pick, s, 0.0), axis=0, keepdims=True))
        chosen = jnp.where(pick, 1.0, chosen)
        masked = jnp.where(pick, NEG_INF, masked)
    total = top_s[0]
    for ts in top_s[1:]:
        total = total + ts

    before = (lax.broadcasted_iota(I32, (tk, tk), 0) < lax.broadcasted_iota(I32, (tk, tk), 1))
    upper = jnp.where(before, 1.0, 0.0).astype(BF16)
    rank = jnp.dot(chosen.astype(BF16), upper, preferred_element_type=F32) + carry_ref[:, 0:1]
    for k in range(TOP_K):
        ids_ref[k:k + 1, :] = ids[k]
        w_ref[k:k + 1, :] = top_s[k] / total * ROUTED_SCALE
        rk = jnp.sum(jnp.where(eio == ids[k], rank, 0.0), axis=0, keepdims=True)
        rank_ref[k:k + 1, :] = rk.astype(I32)
    carry_ref[...] = carry_ref[...] + jnp.sum(chosen, axis=1, keepdims=True)
    cnt_ref[...] = carry_ref[...].astype(I32)


def _route(logits_t, bias_col):
    e, t = logits_t.shape
    tk = min(512, t)
    row8 = lambda dt: jax.ShapeDtypeStruct((TOP_K, t), dt)
    blk8 = pl.BlockSpec((TOP_K, tk), lambda i: (0, i))
    return pl.pallas_call(
        _route_kernel, grid=(t // tk,),
        in_specs=[pl.BlockSpec((e, tk), lambda i: (0, i)),
                  pl.BlockSpec((e, 1), lambda i: (0, 0))],
        out_specs=[blk8, blk8, blk8, pl.BlockSpec((e, 128), lambda i: (0, 0))],
        out_shape=[row8(I32), row8(F32), row8(I32), jax.ShapeDtypeStruct((e, 128), I32)],
        scratch_shapes=[pltpu.VMEM((e, 128), F32)],
        compiler_params=_params(1), name="route",
    )(logits_t, bias_col)


def _dispatch_kernel(pad_ref, cnt_ref, hp_ref, ids_ref, rk_ref, xs_ref, zero_ref, sem, zsem, *, bm):
    i = pl.program_id(0)
    td = hp_ref.shape[0] // ROWS_PER_TOKEN
    n_blocks = xs_ref.shape[0] // (bm * ROWS_PER_TOKEN)

    @pl.when(i == 0)
    def _():
        zero_ref[...] = jnp.zeros_like(zero_ref)
        last_e = N_EXPERTS - 1
        used_blocks = pad_ref[last_e] // bm + (cnt_ref[last_e] + bm - 1) // bm

        def zero_block(j):
            n = bm * ROWS_PER_TOKEN
            return pltpu.make_async_copy(zero_ref, xs_ref.at[pl.ds(pl.multiple_of(j * n, n), n)], zsem)

        def block_start(j, c):
            zero_block(j).start()
            return c

        def block_wait(j, c):
            zero_block(j).wait()
            return c

        lax.fori_loop(used_blocks, n_blocks, block_start, 0)
        lax.fori_loop(used_blocks, n_blocks, block_wait, 0)

        def zero_row(row):
            return pltpu.make_async_copy(zero_ref.at[_token_slice(0)], xs_ref.at[_token_slice(row)], zsem)

        def per_expert(e, c):
            first = pad_ref[e] + cnt_ref[e]
            last = pad_ref[e] + (cnt_ref[e] + bm - 1) // bm * bm

            def start_body(row, cc):
                zero_row(row).start()
                return cc

            def wait_body(row, cc):
                zero_row(row).wait()
                return cc

            lax.fori_loop(first, last, start_body, 0)
            lax.fori_loop(first, last, wait_body, 0)
            return c

        lax.fori_loop(0, N_EXPERTS, per_expert, 0)

    def body(t, c):
        for k in range(TOP_K):
            slot = pad_ref[ids_ref[k, t]] + rk_ref[k, t]
            pltpu.make_async_copy(hp_ref.at[_token_slice(t)], xs_ref.at[_token_slice(slot)], sem).start()
        return c

    lax.fori_loop(0, td, body, 0)
    for k in range(TOP_K):
        pltpu.make_async_copy(hp_ref, xs_ref.at[pl.ds(0, td * ROWS_PER_TOKEN)], sem).wait()


def _dispatch(pad_start, counts, hp, ids_t, rank_t, n_rows, bm):
    half = hp.shape[1]
    t = hp.shape[0] // ROWS_PER_TOKEN
    td = min(256, t)
    smem8 = pl.BlockSpec((TOP_K, td), lambda i, *_: (0, i), memory_space=pltpu.SMEM)
    grid_spec = pltpu.PrefetchScalarGridSpec(
        num_scalar_prefetch=2, grid=(t // td,),
        in_specs=[pl.BlockSpec((td * ROWS_PER_TOKEN, half), lambda i, *_: (i, 0)), smem8, smem8],
        out_specs=pl.BlockSpec(memory_space=pl.ANY),
        scratch_shapes=[pltpu.VMEM((bm * ROWS_PER_TOKEN, half), BF16),
                        pltpu.SemaphoreType.DMA(()), pltpu.SemaphoreType.DMA(())])
    return pl.pallas_call(
        functools.partial(_dispatch_kernel, bm=bm), grid_spec=grid_spec,
        out_shape=jax.ShapeDtypeStruct((n_rows * ROWS_PER_TOKEN, half), BF16),
        compiler_params=_params(1), name="dispatch",
    )(pad_start, counts, hp, ids_t, rank_t)


def _expert_kernel(be_ref, nu_ref, x_ref, wg_ref, wu_ref, wd_ref, o_ref):
    i = pl.program_id(0)

    @pl.when(i < nu_ref[0])
    def _():
        x = _from_token_rows(x_ref[...]).astype(BF16)
        g = jnp.dot(x, wg_ref[0], preferred_element_type=F32)
        u = jnp.dot(x, wu_ref[0], preferred_element_type=F32)
        mid = (_silu(g) * u).astype(BF16)
        o_ref[...] = _to_token_rows(jnp.dot(mid, wd_ref[0], preferred_element_type=F32))

    @pl.when(i >= nu_ref[0])
    def _():
        o_ref[...] = jnp.zeros_like(o_ref)


def _experts(blk_e, n_used, xs, wg_b, wu_b, wd_b, n_blocks, bm):
    half = xs.shape[1]
    _, d, de = wg_b.shape
    x_map = lambda i, be, nu: (jnp.minimum(i, nu[0] - 1), 0)
    w_map = lambda i, be, nu: (be[i], 0, 0)
    grid_spec = pltpu.PrefetchScalarGridSpec(
        num_scalar_prefetch=2, grid=(n_blocks,),
        in_specs=[pl.BlockSpec((bm * ROWS_PER_TOKEN, half), x_map),
                  pl.BlockSpec((1, d, de), w_map),
                  pl.BlockSpec((1, d, de), w_map),
                  pl.BlockSpec((1, de, d), w_map)],
        out_specs=pl.BlockSpec((bm * ROWS_PER_TOKEN, half), lambda i, be, nu: (i, 0)))
    return pl.pallas_call(
        _expert_kernel, grid_spec=grid_spec,
        out_shape=jax.ShapeDtypeStruct((n_blocks * bm * ROWS_PER_TOKEN, half), BF16),
        compiler_params=_params(1), name="experts",
    )(blk_e, n_used, xs, wg_b, wu_b, wd_b)


def _combine_kernel(pad_ref, x1_ref, hp_ref, w_ref, mod_ref, fw_ref, wsg_ref, wsu_ref, wsd_ref,
                    idc_ref, rkc_ref, idn_ref, rkn_ref, ys_ref, o_ref, ybuf, sem, *, n_tiles):
    i = pl.program_id(0)
    tc = x1_ref.shape[0]

    def issue(ids_ref, rk_ref, buf):
        def body(t, c):
            for k in range(TOP_K):
                slot = pad_ref[ids_ref[k, t]] + rk_ref[k, t]
                pltpu.make_async_copy(ys_ref.at[_token_slice(slot)], ybuf.at[buf, k, _token_slice(t)],
                                      sem.at[buf]).start()
            return c
        lax.fori_loop(0, tc, body, 0)

    @pl.when(i == 0)
    def _():
        issue(idc_ref, rkc_ref, 0)

    @pl.when(i + 1 < n_tiles)
    def _():
        issue(idn_ref, rkn_ref, (i + 1) % 2)

    buf = i % 2
    for k in range(TOP_K):
        pltpu.make_async_copy(ys_ref.at[pl.ds(0, tc * ROWS_PER_TOKEN)], ybuf.at[buf, k], sem.at[buf]).wait()

    h = _from_token_rows(hp_ref[...]).astype(BF16)
    sg = jnp.dot(h, wsg_ref[...], preferred_element_type=F32)
    su = jnp.dot(h, wsu_ref[...], preferred_element_type=F32)
    shared = jnp.dot((_silu(sg) * su).astype(BF16), wsd_ref[...], preferred_element_type=F32)
    w = w_ref[...]
    routed = w[:, 0:1] * _from_token_rows(ybuf[buf, 0])
    for k in range(1, TOP_K):
        routed = routed + w[:, k:k + 1] * _from_token_rows(ybuf[buf, k])
    x = x1_ref[...] + mod_ref[0][5:6, :] * (routed + shared)
    o_ref[...] = x * lax.rsqrt(jnp.mean(x * x, axis=-1, keepdims=True) + EPS) * fw_ref[...]


def _combine(pad_start, x1, hp, w_tok, mod3, final_w, wsg_b, wsu_b, wsd_b, ids_t, rank_t, ys, seq):
    t, d = x1.shape
    half = d // 2
    ds = wsg_b.shape[1]
    tc = min(256, seq)
    n_tiles = t // tc
    tiles_per_seq = seq // tc
    cur8 = pl.BlockSpec((TOP_K, tc), lambda i, *_: (0, i), memory_space=pltpu.SMEM)
    nxt8 = pl.BlockSpec((TOP_K, tc), lambda i, *_: (0, jnp.minimum(i + 1, n_tiles - 1)),
                        memory_space=pltpu.SMEM)
    full = lambda shape: pl.BlockSpec(shape, lambda i, *_: (0,) * len(shape))
    grid_spec = pltpu.PrefetchScalarGridSpec(
        num_scalar_prefetch=1, grid=(n_tiles,),
        in_specs=[pl.BlockSpec((tc, d), lambda i, *_: (i, 0)),
                  pl.BlockSpec((tc * ROWS_PER_TOKEN, half), lambda i, *_: (i, 0)),
                  pl.BlockSpec((tc, TOP_K), lambda i, *_: (i, 0)),
                  pl.BlockSpec((1, 6, d), lambda i, *_: (i // tiles_per_seq, 0, 0)),
                  full((1, d)), full((d, ds)), full((d, ds)), full((ds, d)),
                  cur8, cur8, nxt8, nxt8,
                  pl.BlockSpec(memory_space=pl.ANY)],
        out_specs=pl.BlockSpec((tc, d), lambda i, *_: (i, 0)),
        scratch_shapes=[pltpu.VMEM((2, TOP_K, tc * ROWS_PER_TOKEN, half), BF16),
                        pltpu.SemaphoreType.DMA((2,))])
    return pl.pallas_call(
        functools.partial(_combine_kernel, n_tiles=n_tiles), grid_spec=grid_spec,
        out_shape=jax.ShapeDtypeStruct((t, d), F32),
        compiler_params=_params(1), name="combine",
    )(pad_start, x1, hp, w_tok, mod3, final_w, wsg_b, wsu_b, wsd_b, ids_t, rank_t, ids_t, rank_t, ys)


def _rope_tables(seq, d):
    half = d // 2
    inv = ROPE_BASE ** (-jnp.arange(half, dtype=F32) / half)
    ang = jnp.arange(seq, dtype=F32)[:, None] * inv[None, :]
    cos, sin = jnp.cos(ang), jnp.sin(ang)
    return jnp.concatenate([cos, cos], axis=1), jnp.concatenate([-sin, sin], axis=1)


def kernel(x, c, w_ada, b_ada, norm1_w, w_in, ret_decay_fwd, ret_decay_bwd, ret_norm_w, w_ret_o, conv_w,
           w_conv_o, w_out, norm2_w, router_w, router_bias, w_gate, w_up, w_down, ws_gate, ws_up, ws_down,
           final_norm_w):
    batch, seq, d = x.shape
    depth = w_ada.shape[0]
    t = batch * seq
    bm = EXPERT_BLOCK
    n_blocks = (t * TOP_K + N_EXPERTS * (bm - 1)) // bm
    n_rows = n_blocks * bm
    cos, sin = _rope_tables(seq, HEAD_DIM)
    c_pad = jnp.pad(c, ((0, BF16_SUBLANES - batch % BF16_SUBLANES), (0, 0)))
    x2 = x.reshape(t, d)

    assert depth == 1, "the final norm is fused into the last stage of a single layer"
    for l in range(depth):
        mod = _ada(c_pad, w_ada[l], b_ada[l][None, :])[:batch]
        mod3 = mod.reshape(batch, 6, d)
        proj = _inproj(x2, mod3, norm1_w[l][None, :], w_in[l].astype(BF16), seq)
        lane_bcast = lambda v: jnp.broadcast_to(v[:, None, None], (RET_HEADS, 1, HEAD_DIM))
        og = _retention(proj, cos, sin, lane_bcast(ret_decay_fwd[l]), lane_bcast(ret_decay_bwd[l]),
                        ret_norm_w[l].reshape(RET_HEADS, 1, HEAD_DIM), batch, seq)
        merged = _mix(og, proj, conv_w[l], w_ret_o[l].astype(BF16), w_conv_o[l].astype(BF16), seq)
        x1, hp, logits_t = _outproj(merged, x2, mod3, norm2_w[l][None, :], w_out[l].astype(BF16),
                                    router_w[l].T, seq)
        ids_t, w_t, rank_t, cnt = _route(logits_t, router_bias[l][:, None])

        counts = cnt[:, 0]
        nblk = (counts + bm - 1) // bm
        blk_end = jnp.cumsum(nblk)
        pad_start = ((blk_end - nblk) * bm).astype(I32)
        n_used = blk_end[-1:].astype(I32)
        blk_ids = jnp.arange(n_blocks, dtype=I32)
        blk_e = jnp.minimum(jnp.sum((blk_ids[:, None] >= blk_end[None, :]).astype(I32), axis=1),
                            N_EXPERTS - 1)

        xs = _dispatch(pad_start, counts, hp, ids_t, rank_t, n_rows, bm)
        ys = _experts(blk_e, n_used, xs, w_gate[l].astype(BF16), w_up[l].astype(BF16),
                      w_down[l].astype(BF16), n_blocks, bm)
        x2 = _combine(pad_start, x1, hp, w_t.T, mod3, final_norm_w[None, :], ws_gate[l].astype(BF16),
                      ws_up[l].astype(BF16), ws_down[l].astype(BF16), ids_t, rank_t, ys, seq)
    return x2.reshape(batch, seq, d)
```

```python
import functools

import jax
import jax.numpy as jnp
from jax import lax
from jax.experimental import pallas as pl
from jax.experimental.pallas import tpu as pltpu

F32 = jnp.float32
BF16 = jnp.bfloat16
I32 = jnp.int32

EPS = 1e-6
RET_HEADS = 8
HEAD_DIM = 128
ROPE_BASE = 10000.0
N_EXPERTS = 64
TOP_K = 8
N_GROUPS = 8
TOPK_GROUPS = 4
GROUP_SIZE = N_EXPERTS // N_GROUPS
ROUTED_SCALE = 2.5

V7X_VMEM_BYTES = 64 * 1024 * 1024
VMEM_LIMIT = V7X_VMEM_BYTES - 8 * 1024 * 1024
BF16_SUBLANES = 16

RET_CHUNK = 512
EXPERT_BLOCK = 256
NEG_INF = float("-inf")


def _params(n_axes):
    return pltpu.CompilerParams(dimension_semantics=("arbitrary",) * n_axes,
                                vmem_limit_bytes=VMEM_LIMIT)


def _sigmoid(x):
    return 1.0 / (1.0 + jnp.exp(-x))


def _silu(x):
    return x * _sigmoid(x)


LANES = 128
TOKEN_WORD_ROWS = 8
TOKEN_ROWS = 2 * TOKEN_WORD_ROWS


def _store_tokens(ref, x, words_ref):
    n, m = x.shape[0], x.shape[1] // 2
    assert m == TOKEN_WORD_ROWS * LANES
    packed = pltpu.pack_elementwise([x[:, :m], x[:, m:]], packed_dtype=BF16)
    for s in range(TOKEN_WORD_ROWS):
        words_ref[pl.ds(s, n, stride=TOKEN_WORD_ROWS), :] = packed[:, s * LANES:(s + 1) * LANES]
    ref[...] = pltpu.bitcast(words_ref[...], BF16)


def _load_tokens(tiles, words_ref):
    n = tiles.shape[0] // TOKEN_ROWS
    words_ref[...] = pltpu.bitcast(tiles, jnp.uint32)
    p = jnp.concatenate([words_ref[pl.ds(s, n, stride=TOKEN_WORD_ROWS), :]
                         for s in range(TOKEN_WORD_ROWS)], axis=1)
    a = pltpu.unpack_elementwise(p, index=0, packed_dtype=BF16, unpacked_dtype=F32)
    b = pltpu.unpack_elementwise(p, index=1, packed_dtype=BF16, unpacked_dtype=F32)
    return jnp.concatenate([a, b], axis=1)


def _token(row):
    return pl.ds(pl.multiple_of(row * TOKEN_ROWS, TOKEN_ROWS), TOKEN_ROWS)


def _words_scratch(n_tokens):
    return pltpu.VMEM((n_tokens * TOKEN_WORD_ROWS, LANES), jnp.uint32)


def _ada_kernel(c_ref, w_ref, b_ref, o_ref):
    s = _silu(c_ref[...]).astype(BF16)
    o_ref[...] = jnp.dot(s, w_ref[...].astype(BF16), preferred_element_type=F32) + b_ref[...]


def _ada(c_pad, w_ada, b_ada):
    m, d = c_pad.shape
    n = w_ada.shape[1]
    tn = 1024
    return pl.pallas_call(
        _ada_kernel, grid=(n // tn,),
        in_specs=[pl.BlockSpec((m, d), lambda j: (0, 0)),
                  pl.BlockSpec((d, tn), lambda j: (0, j)),
                  pl.BlockSpec((1, tn), lambda j: (0, j))],
        out_specs=pl.BlockSpec((m, tn), lambda j: (0, j)),
        out_shape=jax.ShapeDtypeStruct((m, n), F32),
        compiler_params=_params(1), name="ada")(c_pad, w_ada, b_ada)


def _inproj_kernel(x_ref, mod_ref, nw_ref, w_ref, o_ref, h_ref):
    @pl.when(pl.program_id(1) == 0)
    def _():
        x = x_ref[...]
        y = x * lax.rsqrt(jnp.mean(x * x, axis=-1, keepdims=True) + EPS) * nw_ref[...]
        m = mod_ref[0]
        h_ref[...] = (y * (1.0 + m[1:2, :]) + m[0:1, :]).astype(BF16)

    o_ref[...] = jnp.dot(h_ref[...], w_ref[...], preferred_element_type=F32).astype(o_ref.dtype)


def _inproj(x2, mod3, norm_w, w_in_b, seq):
    t, d = x2.shape
    n = w_in_b.shape[1]
    tm, tn = min(1024, seq), 1024
    tiles_per_seq = seq // tm
    return pl.pallas_call(
        _inproj_kernel, grid=(t // tm, n // tn),
        in_specs=[pl.BlockSpec((tm, d), lambda i, j: (i, 0)),
                  pl.BlockSpec((1, 6, d), lambda i, j: (i // tiles_per_seq, 0, 0)),
                  pl.BlockSpec((1, d), lambda i, j: (0, 0)),
                  pl.BlockSpec((d, tn), lambda i, j: (0, j))],
        out_specs=pl.BlockSpec((tm, tn), lambda i, j: (i, j)),
        out_shape=jax.ShapeDtypeStruct((t, n), BF16),
        scratch_shapes=[pltpu.VMEM((tm, d), BF16)],
        compiler_params=_params(2), name="inproj")(x2, mod3, norm_w, w_in_b)


def _log_sigmoid(x):
    return jnp.minimum(x, 0.0) - jnp.log1p(jnp.exp(-jnp.abs(x)))


def _ret_kernel(q_ref, k_ref, v_ref, g_ref, cos_ref, sin_ref, df_ref, db_ref, nw_ref, o_ref,
                mask_ref, qf_ref, qb_ref, kf_ref, kb_ref, dec_ref, qr_ref, kr_ref, acc_ref, *, chunk):
    seq, d = q_ref.shape
    n = seq // chunk
    c = chunk

    @pl.when(pl.program_id(1) == 0)
    def _():
        lgf = _log_sigmoid(df_ref[0])
        lgb = _log_sigmoid(db_ref[0])
        ii = lax.broadcasted_iota(I32, (c, c), 0)
        jj = lax.broadcasted_iota(I32, (c, c), 1)
        diff = (ii - jj).astype(F32)
        lgf_c = jnp.concatenate([lgf] * (c // d), axis=1)
        lgb_c = jnp.concatenate([lgb] * (c // d), axis=1)
        mask_ref[...] = jnp.where(diff >= 0.0,
                                  jnp.exp(lgf_c * jnp.maximum(diff, 0.0)),
                                  jnp.exp(lgb_c * jnp.maximum(-diff, 0.0)))
        pos = lax.broadcasted_iota(I32, (c, d), 0).astype(F32)
        qf_ref[...] = jnp.exp(lgf * (pos + 1.0))
        qb_ref[...] = jnp.exp(lgb * (c - pos))
        kf_ref[...] = jnp.exp(lgf * (c - 1.0 - pos))
        kb_ref[...] = jnp.exp(lgb * pos)
        dec_ref[0:1, :] = jnp.exp(lgf * c)
        dec_ref[1:2, :] = jnp.exp(lgb * c)

    scale = d ** -0.5
    nt = (((1,), (1,)), ((), ()))
    sls = [pl.ds(i * c, c) for i in range(n)]

    for sl in sls:
        cs = cos_ref[sl, :]
        sn = sin_ref[sl, :]
        q = q_ref[sl, :].astype(F32)
        k = k_ref[sl, :].astype(F32)
        qr_ref[sl, :] = q * cs + pltpu.roll(q, d // 2, 1) * sn
        kr_ref[sl, :] = (k * cs + pltpu.roll(k, d // 2, 1) * sn) * scale

    for sl in sls:
        s = lax.dot_general(qr_ref[sl, :].astype(BF16), kr_ref[sl, :].astype(BF16), nt,
                            preferred_element_type=F32)
        p = (s * mask_ref[...]).astype(BF16)
        acc_ref[sl, :] = jnp.dot(p, v_ref[sl, :], preferred_element_type=F32)

    def kv_state(sl, kw_ref):
        kw = (kr_ref[sl, :] * kw_ref[...]).T.astype(BF16)
        return jnp.dot(kw, v_ref[sl, :], preferred_element_type=F32)

    def scan(order, qw_ref, kw_ref, dec):
        st = jnp.zeros((d, d), F32)
        for idx, ci in enumerate(order):
            sl = sls[ci]
            if idx > 0:
                qw = (qr_ref[sl, :] * qw_ref[...]).astype(BF16)
                acc_ref[sl, :] += jnp.dot(qw, st.astype(BF16), preferred_element_type=F32)
            if idx < n - 1:
                st = dec * st + kv_state(sl, kw_ref)

    scan(list(range(n)), qf_ref, kf_ref, dec_ref[0:1, :])
    scan(list(range(n - 1, -1, -1)), qb_ref, kb_ref, dec_ref[1:2, :])

    nw = nw_ref[0]
    for sl in sls:
        o = acc_ref[sl, :]
        on = o * lax.rsqrt(jnp.mean(o * o, axis=-1, keepdims=True) + EPS) * nw
        g = g_ref[sl, :].astype(F32)
        o_ref[sl, :] = (_silu(g) * on).astype(o_ref.dtype)


def _retention(proj, cos, sin, dec_f, dec_b, ret_norm_w, batch, seq):
    h, d = RET_HEADS, HEAD_DIM
    c = min(RET_CHUNK, seq)
    col = lambda off: pl.BlockSpec((seq, d), lambda hh, b: (b, off + hh))
    per_head = pl.BlockSpec((1, 1, d), lambda hh, b: (hh, 0, 0))
    table = pl.BlockSpec((seq, d), lambda hh, b: (0, 0))
    vm = lambda shape, dt=F32: pltpu.VMEM(shape, dt)
    return pl.pallas_call(
        functools.partial(_ret_kernel, chunk=c), grid=(h, batch),
        in_specs=[col(0), col(h), col(2 * h), col(3 * h), table, table, per_head, per_head, per_head],
        out_specs=pl.BlockSpec((seq, d), lambda hh, b: (b, hh)),
        out_shape=jax.ShapeDtypeStruct((batch * seq, h * d), BF16),
        scratch_shapes=[vm((c, c)), vm((c, d)), vm((c, d)), vm((c, d)), vm((c, d)), vm((8, d)),
                        vm((seq, d)), vm((seq, d)), vm((seq, d))],
        compiler_params=_params(2), name="retention",
    )(proj, proj, proj, proj, cos, sin, dec_f, dec_b, ret_norm_w)


def _mix_kernel(og_ref, cb_ref, cc_ref, cu_ref, ccp_ref, cup_ref, ccn_ref, cun_ref, gr_ref, gc_ref,
                cw_ref, wr_ref, wc_ref, o_ref, z_ref, *, tiles_per_seq):
    i = pl.program_id(0)

    @pl.when(pl.program_id(1) == 0)
    def _():
        tm = cc_ref.shape[0]
        u = cc_ref[...].astype(F32) * cu_ref[...].astype(F32)
        pos = i % tiles_per_seq
        last = BF16_SUBLANES - 1
        u_before = ccp_ref[last:last + 1, :].astype(F32) * cup_ref[last:last + 1, :].astype(F32)
        u_before = jnp.where(pos == 0, 0.0, u_before)
        u_after = ccn_ref[0:1, :].astype(F32) * cun_ref[0:1, :].astype(F32)
        u_after = jnp.where(pos == tiles_per_seq - 1, 0.0, u_after)
        row = lax.broadcasted_iota(I32, u.shape, 0)
        u_prev = jnp.where(row == 0, u_before, pltpu.roll(u, 1, 0))
        u_next = jnp.where(row == tm - 1, u_after, pltpu.roll(u, tm - 1, 0))
        cw = cw_ref[...]
        y = cw[0:1, :] * u_prev + cw[1:2, :] * u + cw[2:3, :] * u_next
        z_ref[...] = (cb_ref[...].astype(F32) * y).astype(BF16)

    yr = jnp.dot(og_ref[...], wr_ref[...], preferred_element_type=F32)
    yc = jnp.dot(z_ref[...], wc_ref[...], preferred_element_type=F32)
    merged = _sigmoid(gr_ref[...].astype(F32)) * yr + _sigmoid(gc_ref[...].astype(F32)) * yc
    o_ref[...] = merged.astype(o_ref.dtype)


def _mix(og, proj, conv_w, w_ret_o_b, w_conv_o_b, seq):
    t, dr = og.shape
    d = w_ret_o_b.shape[1]
    tm = min(512, seq)
    tn = dr
    tiles_per_seq = seq // tm
    hb = tm // BF16_SUBLANES
    n_hblk = t // BF16_SUBLANES
    wide = lambda off: pl.BlockSpec((tm, dr), lambda i, j: (i, off))
    before = lambda off: pl.BlockSpec((BF16_SUBLANES, dr), lambda i, j: (jnp.maximum(i * hb - 1, 0), off))
    after = lambda off: pl.BlockSpec((BF16_SUBLANES, dr), lambda i, j: (jnp.minimum((i + 1) * hb, n_hblk - 1), off))
    gate = lambda off: pl.BlockSpec((tm, tn), lambda i, j: (i, off + j))
    return pl.pallas_call(
        functools.partial(_mix_kernel, tiles_per_seq=tiles_per_seq), grid=(t // tm, d // tn),
        in_specs=[pl.BlockSpec((tm, dr), lambda i, j: (i, 0)),
                  wide(4), wide(5), wide(6), before(5), before(6), after(5), after(6),
                  gate(7), gate(7 + d // tn),
                  pl.BlockSpec((3, dr), lambda i, j: (0, 0)),
                  pl.BlockSpec((dr, tn), lambda i, j: (0, j)),
                  pl.BlockSpec((dr, tn), lambda i, j: (0, j))],
        out_specs=pl.BlockSpec((tm, tn), lambda i, j: (i, j)),
        out_shape=jax.ShapeDtypeStruct((t, d), BF16),
        scratch_shapes=[pltpu.VMEM((tm, dr), BF16)],
        compiler_params=_params(2), name="mix",
    )(og, proj, proj, proj, proj, proj, proj, proj, proj, proj, conv_w, w_ret_o_b, w_conv_o_b)


def _outproj_kernel(m_ref, x_ref, mod_ref, nw_ref, w_ref, rwt_ref, x1_ref, hp_ref, lg_ref, words_ref):
    y = jnp.dot(m_ref[...], w_ref[...], preferred_element_type=F32)
    m = mod_ref[0]
    x1 = x_ref[...] + m[2:3, :] * y
    x1_ref[...] = x1
    hn = x1 * lax.rsqrt(jnp.mean(x1 * x1, axis=-1, keepdims=True) + EPS) * nw_ref[...]
    h = hn * (1.0 + m[4:5, :]) + m[3:4, :]
    _store_tokens(hp_ref, h, words_ref)
    h_hi = h.astype(BF16)
    h_lo = (h - h_hi.astype(F32)).astype(BF16)
    rw = rwt_ref[...]
    r_hi = rw.astype(BF16)
    r_lo = (rw - r_hi.astype(F32)).astype(BF16)
    nt = (((1,), (1,)), ((), ()))
    dot = functools.partial(lax.dot_general, dimension_numbers=nt, preferred_element_type=F32)
    lg_ref[...] = dot(r_hi, h_hi) + dot(r_hi, h_lo) + dot(r_lo, h_hi)


def _outproj(merged, x2, mod3, norm_w, w_out_b, router_wt, seq):
    t, d = x2.shape
    e = router_wt.shape[0]
    tm = min(512, seq)
    tiles_per_seq = seq // tm
    return pl.pallas_call(
        _outproj_kernel, grid=(t // tm,),
        in_specs=[pl.BlockSpec((tm, d), lambda i: (i, 0)),
                  pl.BlockSpec((tm, d), lambda i: (i, 0)),
                  pl.BlockSpec((1, 6, d), lambda i: (i // tiles_per_seq, 0, 0)),
                  pl.BlockSpec((1, d), lambda i: (0, 0)),
                  pl.BlockSpec((d, d), lambda i: (0, 0)),
                  pl.BlockSpec((e, d), lambda i: (0, 0))],
        out_specs=[pl.BlockSpec((tm, d), lambda i: (i, 0)),
                   pl.BlockSpec((TOKEN_ROWS * tm, LANES), lambda i: (i, 0)),
                   pl.BlockSpec((e, tm), lambda i: (0, i))],
        out_shape=[jax.ShapeDtypeStruct((t, d), F32),
                   jax.ShapeDtypeStruct((TOKEN_ROWS * t, LANES), BF16),
                   jax.ShapeDtypeStruct((e, t), F32)],
        scratch_shapes=[_words_scratch(tm)],
        compiler_params=_params(1), name="outproj",
    )(merged, x2, mod3, norm_w, w_out_b, router_wt)


def _first_max(x, iota, sentinel):
    m = jnp.max(x, axis=0, keepdims=True)
    idx = jnp.min(jnp.where(x == m, iota, sentinel), axis=0, keepdims=True)
    return m, idx, iota == idx


def _route_kernel(lg_ref, bias_ref, ids_ref, w_ref, rank_ref, cnt_ref, carry_ref):
    i = pl.program_id(0)
    e, tk = lg_ref.shape

    @pl.when(i == 0)
    def _():
        carry_ref[...] = jnp.zeros_like(carry_ref)

    s = _sigmoid(lg_ref[...])
    biased = s + bias_ref[...]
    sub = lax.broadcasted_iota(I32, (GROUP_SIZE, tk), 0)
    group_rows = []
    for g in range(N_GROUPS):
        xg = biased[g * GROUP_SIZE:(g + 1) * GROUP_SIZE, :]
        m1, _, pick = _first_max(xg, sub, GROUP_SIZE)
        m2 = jnp.max(jnp.where(pick, NEG_INF, xg), axis=0, keepdims=True)
        group_rows.append(m1 + m2)
    gs = jnp.concatenate(group_rows, axis=0)
    gsub = lax.broadcasted_iota(I32, (N_GROUPS, tk), 0)
    sel = jnp.zeros((N_GROUPS, tk), F32)
    for _ in range(TOPK_GROUPS):
        _, _, pick = _first_max(gs, gsub, N_GROUPS)
        sel = jnp.where(pick, 1.0, sel)
        gs = jnp.where(pick, NEG_INF, gs)
    masked_rows = []
    for g in range(N_GROUPS):
        xg = biased[g * GROUP_SIZE:(g + 1) * GROUP_SIZE, :]
        masked_rows.append(jnp.where(sel[g:g + 1, :] > 0.5, xg, NEG_INF))
    masked = jnp.concatenate(masked_rows, axis=0)

    eio = lax.broadcasted_iota(I32, (e, tk), 0)
    chosen = jnp.zeros((e, tk), F32)
    ids, top_s = [], []
    for _ in range(TOP_K):
        _, idx, pick = _first_max(masked, eio, e)
        ids.append(idx)
        top_s.append(jnp.sum(jnp.where(pick, s, 0.0), axis=0, keepdims=True))
        chosen = jnp.where(pick, 1.0, chosen)
        masked = jnp.where(pick, NEG_INF, masked)
    total = top_s[0]
    for ts in top_s[1:]:
        total = total + ts

    before = (lax.broadcasted_iota(I32, (tk, tk), 0) < lax.broadcasted_iota(I32, (tk, tk), 1))
    upper = jnp.where(before, 1.0, 0.0).astype(BF16)
    rank = jnp.dot(chosen.astype(BF16), upper, preferred_element_type=F32) + carry_ref[:, 0:1]
    for k in range(TOP_K):
        ids_ref[k:k + 1, :] = ids[k]
        w_ref[k:k + 1, :] = top_s[k] / total * ROUTED_SCALE
        rk = jnp.sum(jnp.where(eio == ids[k], rank, 0.0), axis=0, keepdims=True)
        rank_ref[k:k + 1, :] = rk.astype(I32)
    carry_ref[...] = carry_ref[...] + jnp.sum(chosen, axis=1, keepdims=True)
    cnt_ref[...] = carry_ref[...].astype(I32)


def _route(logits_t, bias_col):
    e, t = logits_t.shape
    tk = min(512, t)
    row8 = lambda dt: jax.ShapeDtypeStruct((TOP_K, t), dt)
    blk8 = pl.BlockSpec((TOP_K, tk), lambda i: (0, i))
    return pl.pallas_call(
        _route_kernel, grid=(t // tk,),
        in_specs=[pl.BlockSpec((e, tk), lambda i: (0, i)),
                  pl.BlockSpec((e, 1), lambda i: (0, 0))],
        out_specs=[blk8, blk8, blk8, pl.BlockSpec((e, 128), lambda i: (0, 0))],
        out_shape=[row8(I32), row8(F32), row8(I32), jax.ShapeDtypeStruct((e, 128), I32)],
        scratch_shapes=[pltpu.VMEM((e, 128), F32)],
        compiler_params=_params(1), name="route",
    )(logits_t, bias_col)


def _dispatch_kernel(pad_ref, cnt_ref, hp_ref, ids_ref, rk_ref, xs_ref, zero_ref, sem, zsem, *, bm):
    i = pl.program_id(0)
    td = hp_ref.shape[0] // TOKEN_ROWS
    n_blocks = xs_ref.shape[0] // (bm * TOKEN_ROWS)

    @pl.when(i == 0)
    def _():
        zero_ref[...] = jnp.zeros_like(zero_ref)
        last_e = N_EXPERTS - 1
        used_blocks = pad_ref[last_e] // bm + (cnt_ref[last_e] + bm - 1) // bm

        def zero_block(j):
            n = bm * TOKEN_ROWS
            return pltpu.make_async_copy(zero_ref, xs_ref.at[pl.ds(pl.multiple_of(j * n, n), n)], zsem)

        def block_start(j, c):
            zero_block(j).start()
            return c

        def block_wait(j, c):
            zero_block(j).wait()
            return c

        lax.fori_loop(used_blocks, n_blocks, block_start, 0)
        lax.fori_loop(used_blocks, n_blocks, block_wait, 0)

        def zero_row(row):
            return pltpu.make_async_copy(zero_ref.at[_token(0)], xs_ref.at[_token(row)], zsem)

        def per_expert(e, c):
            first = pad_ref[e] + cnt_ref[e]
            last = pad_ref[e] + (cnt_ref[e] + bm - 1) // bm * bm

            def start_body(row, cc):
                zero_row(row).start()
                return cc

            def wait_body(row, cc):
                zero_row(row).wait()
                return cc

            lax.fori_loop(first, last, start_body, 0)
            lax.fori_loop(first, last, wait_body, 0)
            return c

        lax.fori_loop(0, N_EXPERTS, per_expert, 0)

    def body(t, c):
        for k in range(TOP_K):
            slot = pad_ref[ids_ref[k, t]] + rk_ref[k, t]
            pltpu.make_async_copy(hp_ref.at[_token(t)], xs_ref.at[_token(slot)], sem).start()
        return c

    lax.fori_loop(0, td, body, 0)
    for k in range(TOP_K):
        pltpu.make_async_copy(hp_ref, xs_ref.at[pl.ds(0, td * TOKEN_ROWS)], sem).wait()


def _dispatch(pad_start, counts, hp, ids_t, rank_t, n_rows, bm):
    t = hp.shape[0] // TOKEN_ROWS
    td = min(256, t)
    smem8 = pl.BlockSpec((TOP_K, td), lambda i, *_: (0, i), memory_space=pltpu.SMEM)
    grid_spec = pltpu.PrefetchScalarGridSpec(
        num_scalar_prefetch=2, grid=(t // td,),
        in_specs=[pl.BlockSpec((td * TOKEN_ROWS, LANES), lambda i, *_: (i, 0)), smem8, smem8],
        out_specs=pl.BlockSpec(memory_space=pl.ANY),
        scratch_shapes=[pltpu.VMEM((bm * TOKEN_ROWS, LANES), BF16),
                        pltpu.SemaphoreType.DMA(()), pltpu.SemaphoreType.DMA(())])
    return pl.pallas_call(
        functools.partial(_dispatch_kernel, bm=bm), grid_spec=grid_spec,
        out_shape=jax.ShapeDtypeStruct((n_rows * TOKEN_ROWS, LANES), BF16),
        compiler_params=_params(1), name="dispatch",
    )(pad_start, counts, hp, ids_t, rank_t)


def _expert_kernel(be_ref, nu_ref, x_ref, wg_ref, wu_ref, wd_ref, o_ref, words_ref):
    i = pl.program_id(0)

    @pl.when(i < nu_ref[0])
    def _():
        x = _load_tokens(x_ref[...], words_ref).astype(BF16)
        g = jnp.dot(x, wg_ref[0], preferred_element_type=F32)
        u = jnp.dot(x, wu_ref[0], preferred_element_type=F32)
        mid = (_silu(g) * u).astype(BF16)
        _store_tokens(o_ref, jnp.dot(mid, wd_ref[0], preferred_element_type=F32), words_ref)

    @pl.when(i >= nu_ref[0])
    def _():
        o_ref[...] = jnp.zeros_like(o_ref)


def _experts(blk_e, n_used, xs, wg_b, wu_b, wd_b, n_blocks, bm):
    _, d, de = wg_b.shape
    x_map = lambda i, be, nu: (jnp.minimum(i, nu[0] - 1), 0)
    w_map = lambda i, be, nu: (be[i], 0, 0)
    grid_spec = pltpu.PrefetchScalarGridSpec(
        num_scalar_prefetch=2, grid=(n_blocks,),
        in_specs=[pl.BlockSpec((bm * TOKEN_ROWS, LANES), x_map),
                  pl.BlockSpec((1, d, de), w_map),
                  pl.BlockSpec((1, d, de), w_map),
                  pl.BlockSpec((1, de, d), w_map)],
        out_specs=pl.BlockSpec((bm * TOKEN_ROWS, LANES), lambda i, be, nu: (i, 0)),
        scratch_shapes=[_words_scratch(bm)])
    return pl.pallas_call(
        _expert_kernel, grid_spec=grid_spec,
        out_shape=jax.ShapeDtypeStruct((n_blocks * bm * TOKEN_ROWS, LANES), BF16),
        compiler_params=_params(1), name="experts",
    )(blk_e, n_used, xs, wg_b, wu_b, wd_b)


def _combine_kernel(pad_ref, x1_ref, hp_ref, w_ref, mod_ref, fw_ref, wsg_ref, wsu_ref, wsd_ref,
                    idc_ref, rkc_ref, idn_ref, rkn_ref, ys_ref, o_ref, ybuf, words_ref, sem, *, n_tiles):
    i = pl.program_id(0)
    tc = x1_ref.shape[0]

    def issue(ids_ref, rk_ref, buf):
        def body(t, c):
            for k in range(TOP_K):
                slot = pad_ref[ids_ref[k, t]] + rk_ref[k, t]
                pltpu.make_async_copy(ys_ref.at[_token(slot)], ybuf.at[buf, k, _token(t)],
                                      sem.at[buf]).start()
            return c
        lax.fori_loop(0, tc, body, 0)

    @pl.when(i == 0)
    def _():
        issue(idc_ref, rkc_ref, 0)

    @pl.when(i + 1 < n_tiles)
    def _():
        issue(idn_ref, rkn_ref, (i + 1) % 2)

    buf = i % 2
    for k in range(TOP_K):
        pltpu.make_async_copy(ys_ref.at[pl.ds(0, tc * TOKEN_ROWS)], ybuf.at[buf, k], sem.at[buf]).wait()

    h = _load_tokens(hp_ref[...], words_ref).astype(BF16)
    sg = jnp.dot(h, wsg_ref[...], preferred_element_type=F32)
    su = jnp.dot(h, wsu_ref[...], preferred_element_type=F32)
    shared = jnp.dot((_silu(sg) * su).astype(BF16), wsd_ref[...], preferred_element_type=F32)
    w = w_ref[...]
    routed = w[:, 0:1] * _load_tokens(ybuf[buf, 0], words_ref)
    for k in range(1, TOP_K):
        routed = routed + w[:, k:k + 1] * _load_tokens(ybuf[buf, k], words_ref)
    x = x1_ref[...] + mod_ref[0][5:6, :] * (routed + shared)
    o_ref[...] = x * lax.rsqrt(jnp.mean(x * x, axis=-1, keepdims=True) + EPS) * fw_ref[...]


def _combine(pad_start, x1, hp, w_tok, mod3, final_w, wsg_b, wsu_b, wsd_b, ids_t, rank_t, ys, seq):
    t, d = x1.shape
    ds = wsg_b.shape[1]
    tc = min(256, seq)
    n_tiles = t // tc
    tiles_per_seq = seq // tc
    cur8 = pl.BlockSpec((TOP_K, tc), lambda i, *_: (0, i), memory_space=pltpu.SMEM)
    nxt8 = pl.BlockSpec((TOP_K, tc), lambda i, *_: (0, jnp.minimum(i + 1, n_tiles - 1)),
                        memory_space=pltpu.SMEM)
    full = lambda shape: pl.BlockSpec(shape, lambda i, *_: (0,) * len(shape))
    grid_spec = pltpu.PrefetchScalarGridSpec(
        num_scalar_prefetch=1, grid=(n_tiles,),
        in_specs=[pl.BlockSpec((tc, d), lambda i, *_: (i, 0)),
                  pl.BlockSpec((tc * TOKEN_ROWS, LANES), lambda i, *_: (i, 0)),
                  pl.BlockSpec((tc, TOP_K), lambda i, *_: (i, 0)),
                  pl.BlockSpec((1, 6, d), lambda i, *_: (i // tiles_per_seq, 0, 0)),
                  full((1, d)), full((d, ds)), full((d, ds)), full((ds, d)),
                  cur8, cur8, nxt8, nxt8,
                  pl.BlockSpec(memory_space=pl.ANY)],
        out_specs=pl.BlockSpec((tc, d), lambda i, *_: (i, 0)),
        scratch_shapes=[pltpu.VMEM((2, TOP_K, tc * TOKEN_ROWS, LANES), BF16), _words_scratch(tc),
                        pltpu.SemaphoreType.DMA((2,))])
    return pl.pallas_call(
        functools.partial(_combine_kernel, n_tiles=n_tiles), grid_spec=grid_spec,
        out_shape=jax.ShapeDtypeStruct((t, d), F32),
        compiler_params=_params(1), name="combine",
    )(pad_start, x1, hp, w_tok, mod3, final_w, wsg_b, wsu_b, wsd_b, ids_t, rank_t, ids_t, rank_t, ys)


def _rope_tables(seq, d):
    half = d // 2
    inv = ROPE_BASE ** (-jnp.arange(half, dtype=F32) / half)
    ang = jnp.arange(seq, dtype=F32)[:, None] * inv[None, :]
    cos, sin = jnp.cos(ang), jnp.sin(ang)
    return jnp.concatenate([cos, cos], axis=1), jnp.concatenate([-sin, sin], axis=1)


def kernel(x, c, w_ada, b_ada, norm1_w, w_in, ret_decay_fwd, ret_decay_bwd, ret_norm_w, w_ret_o, conv_w,
           w_conv_o, w_out, norm2_w, router_w, router_bias, w_gate, w_up, w_down, ws_gate, ws_up, ws_down,
           final_norm_w):
    batch, seq, d = x.shape
    depth = w_ada.shape[0]
    t = batch * seq
    bm = EXPERT_BLOCK
    n_blocks = (t * TOP_K + N_EXPERTS * (bm - 1)) // bm
    n_rows = n_blocks * bm
    cos, sin = _rope_tables(seq, HEAD_DIM)
    c_pad = jnp.pad(c, ((0, BF16_SUBLANES - batch % BF16_SUBLANES), (0, 0)))
    x2 = x.reshape(t, d)

    assert depth == 1, "the final norm is fused into the last stage of a single layer"
    for l in range(depth):
        mod = _ada(c_pad, w_ada[l], b_ada[l][None, :])[:batch]
        mod3 = mod.reshape(batch, 6, d)
        proj = _inproj(x2, mod3, norm1_w[l][None, :], w_in[l].astype(BF16), seq)
        lane_bcast = lambda v: jnp.broadcast_to(v[:, None, None], (RET_HEADS, 1, HEAD_DIM))
        og = _retention(proj, cos, sin, lane_bcast(ret_decay_fwd[l]), lane_bcast(ret_decay_bwd[l]),
                        ret_norm_w[l].reshape(RET_HEADS, 1, HEAD_DIM), batch, seq)
        merged = _mix(og, proj, conv_w[l], w_ret_o[l].astype(BF16), w_conv_o[l].astype(BF16), seq)
        x1, hp, logits_t = _outproj(merged, x2, mod3, norm2_w[l][None, :], w_out[l].astype(BF16),
                                    router_w[l].T, seq)
        ids_t, w_t, rank_t, cnt = _route(logits_t, router_bias[l][:, None])

        counts = cnt[:, 0]
        nblk = (counts + bm - 1) // bm
        blk_end = jnp.cumsum(nblk)
        pad_start = ((blk_end - nblk) * bm).astype(I32)
        n_used = blk_end[-1:].astype(I32)
        blk_ids = jnp.arange(n_blocks, dtype=I32)
        blk_e = jnp.minimum(jnp.sum((blk_ids[:, None] >= blk_end[None, :]).astype(I32), axis=1),
                            N_EXPERTS - 1)

        xs = _dispatch(pad_start, counts, hp, ids_t, rank_t, n_rows, bm)
        ys = _experts(blk_e, n_used, xs, w_gate[l].astype(BF16), w_up[l].astype(BF16),
                      w_down[l].astype(BF16), n_blocks, bm)
        x2 = _combine(pad_start, x1, hp, w_t.T, mod3, final_norm_w[None, :], ws_gate[l].astype(BF16),
                      ws_up[l].astype(BF16), ws_down[l].astype(BF16), ids_t, rank_t, ys, seq)
    return x2.reshape(batch, seq, d)
```

```python
import functools

import jax
import jax.numpy as jnp
from jax import lax
from jax.experimental import pallas as pl
from jax.experimental.pallas import tpu as pltpu

F32 = jnp.float32
BF16 = jnp.bfloat16
I32 = jnp.int32

EPS = 1e-6
RET_HEADS = 8
HEAD_DIM = 128
ROPE_BASE = 10000.0
N_EXPERTS = 64
TOP_K = 8
N_GROUPS = 8
TOPK_GROUPS = 4
GROUP_SIZE = N_EXPERTS // N_GROUPS
ROUTED_SCALE = 2.5

V7X_VMEM_BYTES = 64 * 1024 * 1024
VMEM_LIMIT = V7X_VMEM_BYTES - 8 * 1024 * 1024
BF16_SUBLANES = 16

RET_CHUNK = 512
EXPERT_BLOCK = 256
NEG_INF = float("-inf")


def _params(n_axes):
    return pltpu.CompilerParams(dimension_semantics=("arbitrary",) * n_axes,
                                vmem_limit_bytes=VMEM_LIMIT)


def _sigmoid(x):
    return 1.0 / (1.0 + jnp.exp(-x))


def _silu(x):
    return x * _sigmoid(x)


LANES = 128
TOKEN_WORD_ROWS = 8
TOKEN_ROWS = 2 * TOKEN_WORD_ROWS


def _store_tokens(ref, x, words_ref):
    n, m = x.shape[0], x.shape[1] // 2
    assert m == TOKEN_WORD_ROWS * LANES
    packed = pltpu.pack_elementwise([x[:, :m], x[:, m:]], packed_dtype=BF16)
    for s in range(TOKEN_WORD_ROWS):
        words_ref[pl.ds(s, n, stride=TOKEN_WORD_ROWS), :] = packed[:, s * LANES:(s + 1) * LANES]
    ref[...] = pltpu.bitcast(words_ref[...], BF16)


def _load_tokens(tiles, words_ref):
    n = tiles.shape[0] // TOKEN_ROWS
    words_ref[...] = pltpu.bitcast(tiles, jnp.uint32)
    p = jnp.concatenate([words_ref[pl.ds(s, n, stride=TOKEN_WORD_ROWS), :]
                         for s in range(TOKEN_WORD_ROWS)], axis=1)
    a = pltpu.unpack_elementwise(p, index=0, packed_dtype=BF16, unpacked_dtype=F32)
    b = pltpu.unpack_elementwise(p, index=1, packed_dtype=BF16, unpacked_dtype=F32)
    return jnp.concatenate([a, b], axis=1)


def _token(row):
    return pl.ds(pl.multiple_of(row * TOKEN_ROWS, TOKEN_ROWS), TOKEN_ROWS)


def _words_scratch(n_tokens):
    return pltpu.VMEM((n_tokens * TOKEN_WORD_ROWS, LANES), jnp.uint32)


def _ada_kernel(c_ref, w_ref, b_ref, o_ref):
    s = _silu(c_ref[...]).astype(BF16)
    o_ref[...] = jnp.dot(s, w_ref[...].astype(BF16), preferred_element_type=F32) + b_ref[...]


def _ada(c_pad, w_ada, b_ada):
    m, d = c_pad.shape
    n = w_ada.shape[1]
    tn = 1024
    return pl.pallas_call(
        _ada_kernel, grid=(n // tn,),
        in_specs=[pl.BlockSpec((m, d), lambda j: (0, 0)),
                  pl.BlockSpec((d, tn), lambda j: (0, j)),
                  pl.BlockSpec((1, tn), lambda j: (0, j))],
        out_specs=pl.BlockSpec((m, tn), lambda j: (0, j)),
        out_shape=jax.ShapeDtypeStruct((m, n), F32),
        compiler_params=_params(1), name="ada")(c_pad, w_ada, b_ada)


def _inproj_kernel(x_ref, mod_ref, nw_ref, w_ref, o_ref, h_ref):
    @pl.when(pl.program_id(1) == 0)
    def _():
        x = x_ref[...]
        y = x * lax.rsqrt(jnp.mean(x * x, axis=-1, keepdims=True) + EPS) * nw_ref[...]
        m = mod_ref[0]
        h_ref[...] = (y * (1.0 + m[1:2, :]) + m[0:1, :]).astype(BF16)

    o_ref[...] = jnp.dot(h_ref[...], w_ref[...], preferred_element_type=F32).astype(o_ref.dtype)


def _inproj(x2, mod3, norm_w, w_in_b, seq):
    t, d = x2.shape
    n = w_in_b.shape[1]
    tm, tn = min(1024, seq), 1024
    tiles_per_seq = seq // tm
    return pl.pallas_call(
        _inproj_kernel, grid=(t // tm, n // tn),
        in_specs=[pl.BlockSpec((tm, d), lambda i, j: (i, 0)),
                  pl.BlockSpec((1, 6, d), lambda i, j: (i // tiles_per_seq, 0, 0)),
                  pl.BlockSpec((1, d), lambda i, j: (0, 0)),
                  pl.BlockSpec((d, tn), lambda i, j: (0, j))],
        out_specs=pl.BlockSpec((tm, tn), lambda i, j: (i, j)),
        out_shape=jax.ShapeDtypeStruct((t, n), BF16),
        scratch_shapes=[pltpu.VMEM((tm, d), BF16)],
        compiler_params=_params(2), name="inproj")(x2, mod3, norm_w, w_in_b)


def _log_sigmoid(x):
    return jnp.minimum(x, 0.0) - jnp.log1p(jnp.exp(-jnp.abs(x)))


def _ret_kernel(q_ref, k_ref, v_ref, g_ref, cos_ref, sin_ref, df_ref, db_ref, nw_ref, o_ref,
                mask_ref, qf_ref, qb_ref, kf_ref, kb_ref, dec_ref, qr_ref, kr_ref, acc_ref, *, chunk):
    seq, d = q_ref.shape
    n = seq // chunk
    c = chunk

    @pl.when(pl.program_id(1) == 0)
    def _():
        lgf = _log_sigmoid(df_ref[0])
        lgb = _log_sigmoid(db_ref[0])
        ii = lax.broadcasted_iota(I32, (c, c), 0)
        jj = lax.broadcasted_iota(I32, (c, c), 1)
        diff = (ii - jj).astype(F32)
        lgf_c = jnp.concatenate([lgf] * (c // d), axis=1)
        lgb_c = jnp.concatenate([lgb] * (c // d), axis=1)
        mask_ref[...] = jnp.where(diff >= 0.0,
                                  jnp.exp(lgf_c * jnp.maximum(diff, 0.0)),
                                  jnp.exp(lgb_c * jnp.maximum(-diff, 0.0)))
        pos = lax.broadcasted_iota(I32, (c, d), 0).astype(F32)
        qf_ref[...] = jnp.exp(lgf * (pos + 1.0))
        qb_ref[...] = jnp.exp(lgb * (c - pos))
        kf_ref[...] = jnp.exp(lgf * (c - 1.0 - pos))
        kb_ref[...] = jnp.exp(lgb * pos)
        dec_ref[0:1, :] = jnp.exp(lgf * c)
        dec_ref[1:2, :] = jnp.exp(lgb * c)

    scale = d ** -0.5
    nt = (((1,), (1,)), ((), ()))
    sls = [pl.ds(i * c, c) for i in range(n)]

    for sl in sls:
        cs = cos_ref[sl, :]
        sn = sin_ref[sl, :]
        q = q_ref[sl, :].astype(F32)
        k = k_ref[sl, :].astype(F32)
        qr_ref[sl, :] = q * cs + pltpu.roll(q, d // 2, 1) * sn
        kr_ref[sl, :] = (k * cs + pltpu.roll(k, d // 2, 1) * sn) * scale

    for sl in sls:
        s = lax.dot_general(qr_ref[sl, :].astype(BF16), kr_ref[sl, :].astype(BF16), nt,
                            preferred_element_type=F32)
        p = (s * mask_ref[...]).astype(BF16)
        acc_ref[sl, :] = jnp.dot(p, v_ref[sl, :], preferred_element_type=F32)

    def kv_state(sl, kw_ref):
        kw = (kr_ref[sl, :] * kw_ref[...]).T.astype(BF16)
        return jnp.dot(kw, v_ref[sl, :], preferred_element_type=F32)

    def scan(order, qw_ref, kw_ref, dec):
        st = jnp.zeros((d, d), F32)
        for idx, ci in enumerate(order):
            sl = sls[ci]
            if idx > 0:
                qw = (qr_ref[sl, :] * qw_ref[...]).astype(BF16)
                acc_ref[sl, :] += jnp.dot(qw, st.astype(BF16), preferred_element_type=F32)
            if idx < n - 1:
                st = dec * st + kv_state(sl, kw_ref)

    scan(list(range(n)), qf_ref, kf_ref, dec_ref[0:1, :])
    scan(list(range(n - 1, -1, -1)), qb_ref, kb_ref, dec_ref[1:2, :])

    nw = nw_ref[0]
    for sl in sls:
        o = acc_ref[sl, :]
        on = o * lax.rsqrt(jnp.mean(o * o, axis=-1, keepdims=True) + EPS) * nw
        g = g_ref[sl, :].astype(F32)
        o_ref[sl, :] = (_silu(g) * on).astype(o_ref.dtype)


def _retention(proj, cos, sin, dec_f, dec_b, ret_norm_w, batch, seq):
    h, d = RET_HEADS, HEAD_DIM
    c = min(RET_CHUNK, seq)
    col = lambda off: pl.BlockSpec((seq, d), lambda hh, b: (b, off + hh))
    per_head = pl.BlockSpec((1, 1, d), lambda hh, b: (hh, 0, 0))
    table = pl.BlockSpec((seq, d), lambda hh, b: (0, 0))
    vm = lambda shape, dt=F32: pltpu.VMEM(shape, dt)
    return pl.pallas_call(
        functools.partial(_ret_kernel, chunk=c), grid=(h, batch),
        in_specs=[col(0), col(h), col(2 * h), col(3 * h), table, table, per_head, per_head, per_head],
        out_specs=pl.BlockSpec((seq, d), lambda hh, b: (b, hh)),
        out_shape=jax.ShapeDtypeStruct((batch * seq, h * d), BF16),
        scratch_shapes=[vm((c, c)), vm((c, d)), vm((c, d)), vm((c, d)), vm((c, d)), vm((8, d)),
                        vm((seq, d)), vm((seq, d)), vm((seq, d))],
        compiler_params=_params(2), name="retention",
    )(proj, proj, proj, proj, cos, sin, dec_f, dec_b, ret_norm_w)


def _mix_kernel(og_ref, cb_ref, cc_ref, cu_ref, ccp_ref, cup_ref, ccn_ref, cun_ref, gr_ref, gc_ref,
                cw_ref, wr_ref, wc_ref, o_ref, z_ref, *, tiles_per_seq):
    i = pl.program_id(0)

    @pl.when(pl.program_id(1) == 0)
    def _():
        tm = cc_ref.shape[0]
        u = cc_ref[...].astype(F32) * cu_ref[...].astype(F32)
        pos = i % tiles_per_seq
        last = BF16_SUBLANES - 1
        u_before = ccp_ref[last:last + 1, :].astype(F32) * cup_ref[last:last + 1, :].astype(F32)
        u_before = jnp.where(pos == 0, 0.0, u_before)
        u_after = ccn_ref[0:1, :].astype(F32) * cun_ref[0:1, :].astype(F32)
        u_after = jnp.where(pos == tiles_per_seq - 1, 0.0, u_after)
        row = lax.broadcasted_iota(I32, u.shape, 0)
        u_prev = jnp.where(row == 0, u_before, pltpu.roll(u, 1, 0))
        u_next = jnp.where(row == tm - 1, u_after, pltpu.roll(u, tm - 1, 0))
        cw = cw_ref[...]
        y = cw[0:1, :] * u_prev + cw[1:2, :] * u + cw[2:3, :] * u_next
        z_ref[...] = (cb_ref[...].astype(F32) * y).astype(BF16)

    yr = jnp.dot(og_ref[...], wr_ref[...], preferred_element_type=F32)
    yc = jnp.dot(z_ref[...], wc_ref[...], preferred_element_type=F32)
    merged = _sigmoid(gr_ref[...].astype(F32)) * yr + _sigmoid(gc_ref[...].astype(F32)) * yc
    o_ref[...] = merged.astype(o_ref.dtype)


def _mix(og, proj, conv_w, w_ret_o_b, w_conv_o_b, seq):
    t, dr = og.shape
    d = w_ret_o_b.shape[1]
    tm = min(512, seq)
    tn = dr
    tiles_per_seq = seq // tm
    hb = tm // BF16_SUBLANES
    n_hblk = t // BF16_SUBLANES
    wide = lambda off: pl.BlockSpec((tm, dr), lambda i, j: (i, off))
    before = lambda off: pl.BlockSpec((BF16_SUBLANES, dr), lambda i, j: (jnp.maximum(i * hb - 1, 0), off))
    after = lambda off: pl.BlockSpec((BF16_SUBLANES, dr), lambda i, j: (jnp.minimum((i + 1) * hb, n_hblk - 1), off))
    gate = lambda off: pl.BlockSpec((tm, tn), lambda i, j: (i, off + j))
    return pl.pallas_call(
        functools.partial(_mix_kernel, tiles_per_seq=tiles_per_seq), grid=(t // tm, d // tn),
        in_specs=[pl.BlockSpec((tm, dr), lambda i, j: (i, 0)),
                  wide(4), wide(5), wide(6), before(5), before(6), after(5), after(6),
                  gate(7), gate(7 + d // tn),
                  pl.BlockSpec((3, dr), lambda i, j: (0, 0)),
                  pl.BlockSpec((dr, tn), lambda i, j: (0, j)),
                  pl.BlockSpec((dr, tn), lambda i, j: (0, j))],
        out_specs=pl.BlockSpec((tm, tn), lambda i, j: (i, j)),
        out_shape=jax.ShapeDtypeStruct((t, d), BF16),
        scratch_shapes=[pltpu.VMEM((tm, dr), BF16)],
        compiler_params=_params(2), name="mix",
    )(og, proj, proj, proj, proj, proj, proj, proj, proj, proj, conv_w, w_ret_o_b, w_conv_o_b)


def _outproj_kernel(m_ref, x_ref, mod_ref, nw_ref, w_ref, rwt_ref, x1_ref, hp_ref, lg_ref, words_ref):
    y = jnp.dot(m_ref[...], w_ref[...], preferred_element_type=F32)
    m = mod_ref[0]
    x1 = x_ref[...] + m[2:3, :] * y
    x1_ref[...] = x1
    hn = x1 * lax.rsqrt(jnp.mean(x1 * x1, axis=-1, keepdims=True) + EPS) * nw_ref[...]
    h = hn * (1.0 + m[4:5, :]) + m[3:4, :]
    _store_tokens(hp_ref, h, words_ref)
    h_hi = h.astype(BF16)
    h_lo = (h - h_hi.astype(F32)).astype(BF16)
    rw = rwt_ref[...]
    r_hi = rw.astype(BF16)
    r_lo = (rw - r_hi.astype(F32)).astype(BF16)
    nt = (((1,), (1,)), ((), ()))
    dot = functools.partial(lax.dot_general, dimension_numbers=nt, preferred_element_type=F32)
    lg_ref[...] = dot(r_hi, h_hi) + dot(r_hi, h_lo) + dot(r_lo, h_hi)


def _outproj(merged, x2, mod3, norm_w, w_out_b, router_wt, seq):
    t, d = x2.shape
    e = router_wt.shape[0]
    tm = min(512, seq)
    tiles_per_seq = seq // tm
    return pl.pallas_call(
        _outproj_kernel, grid=(t // tm,),
        in_specs=[pl.BlockSpec((tm, d), lambda i: (i, 0)),
                  pl.BlockSpec((tm, d), lambda i: (i, 0)),
                  pl.BlockSpec((1, 6, d), lambda i: (i // tiles_per_seq, 0, 0)),
                  pl.BlockSpec((1, d), lambda i: (0, 0)),
                  pl.BlockSpec((d, d), lambda i: (0, 0)),
                  pl.BlockSpec((e, d), lambda i: (0, 0))],
        out_specs=[pl.BlockSpec((tm, d), lambda i: (i, 0)),
                   pl.BlockSpec((TOKEN_ROWS * tm, LANES), lambda i: (i, 0)),
                   pl.BlockSpec((e, tm), lambda i: (0, i))],
        out_shape=[jax.ShapeDtypeStruct((t, d), F32),
                   jax.ShapeDtypeStruct((TOKEN_ROWS * t, LANES), BF16),
                   jax.ShapeDtypeStruct((e, t), F32)],
        scratch_shapes=[_words_scratch(tm)],
        compiler_params=_params(1), name="outproj",
    )(merged, x2, mod3, norm_w, w_out_b, router_wt)


def _first_max(x, iota, sentinel):
    m = jnp.max(x, axis=0, keepdims=True)
    idx = jnp.min(jnp.where(x == m, iota, sentinel), axis=0, keepdims=True)
    return m, idx, iota == idx


def _route_kernel(lg_ref, bias_ref, ids_ref, w_ref, rank_ref, cnt_ref, carry_ref):
    i = pl.program_id(0)
    e, tk = lg_ref.shape

    @pl.when(i == 0)
    def _():
        carry_ref[...] = jnp.zeros_like(carry_ref)

    s = _sigmoid(lg_ref[...])
    biased = s + bias_ref[...]
    sub = lax.broadcasted_iota(I32, (GROUP_SIZE, tk), 0)
    group_rows = []
    for g in range(N_GROUPS):
        xg = biased[g * GROUP_SIZE:(g + 1) * GROUP_SIZE, :]
        m1, _, pick = _first_max(xg, sub, GROUP_SIZE)
        m2 = jnp.max(jnp.where(pick, NEG_INF, xg), axis=0, keepdims=True)
        group_rows.append(m1 + m2)
    gs = jnp.concatenate(group_rows, axis=0)
    gsub = lax.broadcasted_iota(I32, (N_GROUPS, tk), 0)
    sel = jnp.zeros((N_GROUPS, tk), F32)
    for _ in range(TOPK_GROUPS):
        _, _, pick = _first_max(gs, gsub, N_GROUPS)
        sel = jnp.where(pick, 1.0, sel)
        gs = jnp.where(pick, NEG_INF, gs)
    masked_rows = []
    for g in range(N_GROUPS):
        xg = biased[g * GROUP_SIZE:(g + 1) * GROUP_SIZE, :]
        masked_rows.append(jnp.where(sel[g:g + 1, :] > 0.5, xg, NEG_INF))
    masked = jnp.concatenate(masked_rows, axis=0)

    eio = lax.broadcasted_iota(I32, (e, tk), 0)
    chosen = jnp.zeros((e, tk), F32)
    ids, top_s = [], []
    for _ in range(TOP_K):
        _, idx, pick = _first_max(masked, eio, e)
        ids.append(idx)
        top_s.append(jnp.sum(jnp.where(pick, s, 0.0), axis=0, keepdims=True))
        chosen = jnp.where(pick, 1.0, chosen)
        masked = jnp.where(pick, NEG_INF, masked)
    total = top_s[0]
    for ts in top_s[1:]:
        total = total + ts

    before = (lax.broadcasted_iota(I32, (tk, tk), 0) < lax.broadcasted_iota(I32, (tk, tk), 1))
    upper = jnp.where(before, 1.0, 0.0).astype(BF16)
    rank = jnp.dot(chosen.astype(BF16), upper, preferred_element_type=F32) + carry_ref[:, 0:1]
    for k in range(TOP_K):
        ids_ref[k:k + 1, :] = ids[k]
        w_ref[k:k + 1, :] = top_s[k] / total * ROUTED_SCALE
        rk = jnp.sum(jnp.where(eio == ids[k], rank, 0.0), axis=0, keepdims=True)
        rank_ref[k:k + 1, :] = rk.astype(I32)
    carry_ref[...] = carry_ref[...] + jnp.sum(chosen, axis=1, keepdims=True)
    cnt_ref[...] = carry_ref[...].astype(I32)


def _route(logits_t, bias_col):
    e, t = logits_t.shape
    tk = min(512, t)
    row8 = lambda dt: jax.ShapeDtypeStruct((TOP_K, t), dt)
    blk8 = pl.BlockSpec((TOP_K, tk), lambda i: (0, i))
    return pl.pallas_call(
        _route_kernel, grid=(t // tk,),
        in_specs=[pl.BlockSpec((e, tk), lambda i: (0, i)),
                  pl.BlockSpec((e, 1), lambda i: (0, 0))],
        out_specs=[blk8, blk8, blk8, pl.BlockSpec((e, 128), lambda i: (0, 0))],
        out_shape=[row8(I32), row8(F32), row8(I32), jax.ShapeDtypeStruct((e, 128), I32)],
        scratch_shapes=[pltpu.VMEM((e, 128), F32)],
        compiler_params=_params(1), name="route",
    )(logits_t, bias_col)


def _dispatch_kernel(pad_ref, cnt_ref, hp_ref, ids_ref, rk_ref, xs_ref, zero_ref, sem, zsem, *, bm):
    i = pl.program_id(0)
    td = hp_ref.shape[0] // TOKEN_ROWS
    n_blocks = xs_ref.shape[0] // (bm * TOKEN_ROWS)

    @pl.when(i == 0)
    def _():
        zero_ref[...] = jnp.zeros_like(zero_ref)
        last_e = N_EXPERTS - 1
        used_blocks = pad_ref[last_e] // bm + (cnt_ref[last_e] + bm - 1) // bm

        def zero_block(j):
            n = bm * TOKEN_ROWS
            return pltpu.make_async_copy(zero_ref, xs_ref.at[pl.ds(pl.multiple_of(j * n, n), n)], zsem)

        def block_start(j, c):
            zero_block(j).start()
            return c

        def block_wait(j, c):
            zero_block(j).wait()
            return c

        lax.fori_loop(used_blocks, n_blocks, block_start, 0)
        lax.fori_loop(used_blocks, n_blocks, block_wait, 0)

        def zero_row(row):
            return pltpu.make_async_copy(zero_ref.at[_token(0)], xs_ref.at[_token(row)], zsem)

        def per_expert(e, c):
            first = pad_ref[e] + cnt_ref[e]
            last = pad_ref[e] + (cnt_ref[e] + bm - 1) // bm * bm

            def start_body(row, cc):
                zero_row(row).start()
                return cc

            def wait_body(row, cc):
                zero_row(row).wait()
                return cc

            lax.fori_loop(first, last, start_body, 0)
            lax.fori_loop(first, last, wait_body, 0)
            return c

        lax.fori_loop(0, N_EXPERTS, per_expert, 0)

    def body(t, c):
        for k in range(TOP_K):
            slot = pad_ref[ids_ref[k, t]] + rk_ref[k, t]
            pltpu.make_async_copy(hp_ref.at[_token(t)], xs_ref.at[_token(slot)], sem).start()
        return c

    lax.fori_loop(0, td, body, 0)
    for k in range(TOP_K):
        pltpu.make_async_copy(hp_ref, xs_ref.at[pl.ds(0, td * TOKEN_ROWS)], sem).wait()


def _dispatch(pad_start, counts, hp, ids_t, rank_t, n_rows, bm):
    t = hp.shape[0] // TOKEN_ROWS
    td = min(1024, t)
    smem8 = pl.BlockSpec((TOP_K, td), lambda i, *_: (0, i), memory_space=pltpu.SMEM)
    grid_spec = pltpu.PrefetchScalarGridSpec(
        num_scalar_prefetch=2, grid=(t // td,),
        in_specs=[pl.BlockSpec((td * TOKEN_ROWS, LANES), lambda i, *_: (i, 0)), smem8, smem8],
        out_specs=pl.BlockSpec(memory_space=pl.ANY),
        scratch_shapes=[pltpu.VMEM((bm * TOKEN_ROWS, LANES), BF16),
                        pltpu.SemaphoreType.DMA(()), pltpu.SemaphoreType.DMA(())])
    return pl.pallas_call(
        functools.partial(_dispatch_kernel, bm=bm), grid_spec=grid_spec,
        out_shape=jax.ShapeDtypeStruct((n_rows * TOKEN_ROWS, LANES), BF16),
        compiler_params=_params(1), name="dispatch",
    )(pad_start, counts, hp, ids_t, rank_t)


def _expert_kernel(be_ref, first_ref, slot_ref, nxt_ref, nu_ref, x_ref, wg_hbm, wu_hbm, wd_hbm, o_ref,
                   words_ref, wg32, wu32, wd32, wg16, wu16, wd16, wsem):
    i = pl.program_id(0)

    def weight_copies(e, s):
        return (pltpu.make_async_copy(wg_hbm.at[e], wg32.at[s], wsem.at[s, 0]),
                pltpu.make_async_copy(wu_hbm.at[e], wu32.at[s], wsem.at[s, 1]),
                pltpu.make_async_copy(wd_hbm.at[e], wd32.at[s], wsem.at[s, 2]))

    @pl.when(i == 0)
    def _():
        for cp in weight_copies(be_ref[0], 0):
            cp.start()

    @pl.when(jnp.logical_and(first_ref[i] == 1, i < nu_ref[0]))
    def _():
        s = slot_ref[i]
        for cp in weight_copies(be_ref[i], s):
            cp.wait()

        @pl.when(nxt_ref[i] >= 0)
        def _():
            for cp in weight_copies(nxt_ref[i], 1 - s):
                cp.start()

        wg16[...] = wg32[s].astype(BF16)
        wu16[...] = wu32[s].astype(BF16)
        wd16[...] = wd32[s].astype(BF16)

    @pl.when(i < nu_ref[0])
    def _():
        x = _load_tokens(x_ref[...], words_ref).astype(BF16)
        g = jnp.dot(x, wg16[...], preferred_element_type=F32)
        u = jnp.dot(x, wu16[...], preferred_element_type=F32)
        mid = (_silu(g) * u).astype(BF16)
        _store_tokens(o_ref, jnp.dot(mid, wd16[...], preferred_element_type=F32), words_ref)

    @pl.when(i >= nu_ref[0])
    def _():
        o_ref[...] = jnp.zeros_like(o_ref)


def _experts(blk_e, blk_first, blk_slot, blk_next, n_used, xs, w_gate, w_up, w_down, n_blocks, bm):
    _, d, de = w_gate.shape
    x_map = lambda i, be, fi, sl, nx, nu: (jnp.minimum(i, nu[0] - 1), 0)
    hbm = pl.BlockSpec(memory_space=pl.ANY)
    grid_spec = pltpu.PrefetchScalarGridSpec(
        num_scalar_prefetch=5, grid=(n_blocks,),
        in_specs=[pl.BlockSpec((bm * TOKEN_ROWS, LANES), x_map), hbm, hbm, hbm],
        out_specs=pl.BlockSpec((bm * TOKEN_ROWS, LANES), lambda i, *_: (i, 0)),
        scratch_shapes=[_words_scratch(bm),
                        pltpu.VMEM((2, d, de), F32), pltpu.VMEM((2, d, de), F32), pltpu.VMEM((2, de, d), F32),
                        pltpu.VMEM((d, de), BF16), pltpu.VMEM((d, de), BF16), pltpu.VMEM((de, d), BF16),
                        pltpu.SemaphoreType.DMA((2, 3))])
    return pl.pallas_call(
        _expert_kernel, grid_spec=grid_spec,
        out_shape=jax.ShapeDtypeStruct((n_blocks * bm * TOKEN_ROWS, LANES), BF16),
        compiler_params=_params(1), name="experts",
    )(blk_e, blk_first, blk_slot, blk_next, n_used, xs, w_gate, w_up, w_down)


def _combine_kernel(pad_ref, x1_ref, hp_ref, w_ref, mod_ref, fw_ref, wsg_ref, wsu_ref, wsd_ref,
                    idc_ref, rkc_ref, idn_ref, rkn_ref, ys_ref, o_ref, ybuf, words_ref, sem, *, n_tiles):
    i = pl.program_id(0)
    tc = x1_ref.shape[0]

    def issue(ids_ref, rk_ref, buf):
        def body(t, c):
            for k in range(TOP_K):
                slot = pad_ref[ids_ref[k, t]] + rk_ref[k, t]
                pltpu.make_async_copy(ys_ref.at[_token(slot)], ybuf.at[buf, k, _token(t)],
                                      sem.at[buf]).start()
            return c
        lax.fori_loop(0, tc, body, 0)

    @pl.when(i == 0)
    def _():
        issue(idc_ref, rkc_ref, 0)

    @pl.when(i + 1 < n_tiles)
    def _():
        issue(idn_ref, rkn_ref, (i + 1) % 2)

    buf = i % 2
    for k in range(TOP_K):
        pltpu.make_async_copy(ys_ref.at[pl.ds(0, tc * TOKEN_ROWS)], ybuf.at[buf, k], sem.at[buf]).wait()

    h = _load_tokens(hp_ref[...], words_ref).astype(BF16)
    sg = jnp.dot(h, wsg_ref[...], preferred_element_type=F32)
    su = jnp.dot(h, wsu_ref[...], preferred_element_type=F32)
    shared = jnp.dot((_silu(sg) * su).astype(BF16), wsd_ref[...], preferred_element_type=F32)
    w = w_ref[...]
    routed = w[:, 0:1] * _load_tokens(ybuf[buf, 0], words_ref)
    for k in range(1, TOP_K):
        routed = routed + w[:, k:k + 1] * _load_tokens(ybuf[buf, k], words_ref)
    x = x1_ref[...] + mod_ref[0][5:6, :] * (routed + shared)
    o_ref[...] = x * lax.rsqrt(jnp.mean(x * x, axis=-1, keepdims=True) + EPS) * fw_ref[...]


def _combine(pad_start, x1, hp, w_tok, mod3, final_w, wsg_b, wsu_b, wsd_b, ids_t, rank_t, ys, seq):
    t, d = x1.shape
    ds = wsg_b.shape[1]
    tc = min(256, seq)
    n_tiles = t // tc
    tiles_per_seq = seq // tc
    cur8 = pl.BlockSpec((TOP_K, tc), lambda i, *_: (0, i), memory_space=pltpu.SMEM)
    nxt8 = pl.BlockSpec((TOP_K, tc), lambda i, *_: (0, jnp.minimum(i + 1, n_tiles - 1)),
                        memory_space=pltpu.SMEM)
    full = lambda shape: pl.BlockSpec(shape, lambda i, *_: (0,) * len(shape))
    grid_spec = pltpu.PrefetchScalarGridSpec(
        num_scalar_prefetch=1, grid=(n_tiles,),
        in_specs=[pl.BlockSpec((tc, d), lambda i, *_: (i, 0)),
                  pl.BlockSpec((tc * TOKEN_ROWS, LANES), lambda i, *_: (i, 0)),
                  pl.BlockSpec((tc, TOP_K), lambda i, *_: (i, 0)),
                  pl.BlockSpec((1, 6, d), lambda i, *_: (i // tiles_per_seq, 0, 0)),
                  full((1, d)), full((d, ds)), full((d, ds)), full((ds, d)),
                  cur8, cur8, nxt8, nxt8,
                  pl.BlockSpec(memory_space=pl.ANY)],
        out_specs=pl.BlockSpec((tc, d), lambda i, *_: (i, 0)),
        scratch_shapes=[pltpu.VMEM((2, TOP_K, tc * TOKEN_ROWS, LANES), BF16), _words_scratch(tc),
                        pltpu.SemaphoreType.DMA((2,))])
    return pl.pallas_call(
        functools.partial(_combine_kernel, n_tiles=n_tiles), grid_spec=grid_spec,
        out_shape=jax.ShapeDtypeStruct((t, d), F32),
        compiler_params=_params(1), name="combine",
    )(pad_start, x1, hp, w_tok, mod3, final_w, wsg_b, wsu_b, wsd_b, ids_t, rank_t, ids_t, rank_t, ys)


def _rope_tables(seq, d):
    half = d // 2
    inv = ROPE_BASE ** (-jnp.arange(half, dtype=F32) / half)
    ang = jnp.arange(seq, dtype=F32)[:, None] * inv[None, :]
    cos, sin = jnp.cos(ang), jnp.sin(ang)
    return jnp.concatenate([cos, cos], axis=1), jnp.concatenate([-sin, sin], axis=1)


def kernel(x, c, w_ada, b_ada, norm1_w, w_in, ret_decay_fwd, ret_decay_bwd, ret_norm_w, w_ret_o, conv_w,
           w_conv_o, w_out, norm2_w, router_w, router_bias, w_gate, w_up, w_down, ws_gate, ws_up, ws_down,
           final_norm_w):
    batch, seq, d = x.shape
    depth = w_ada.shape[0]
    t = batch * seq
    bm = EXPERT_BLOCK
    n_blocks = (t * TOP_K + N_EXPERTS * (bm - 1)) // bm
    n_rows = n_blocks * bm
    cos, sin = _rope_tables(seq, HEAD_DIM)
    c_pad = jnp.pad(c, ((0, BF16_SUBLANES - batch % BF16_SUBLANES), (0, 0)))
    x2 = x.reshape(t, d)

    assert depth == 1, "the final norm is fused into the last stage of a single layer"
    for l in range(depth):
        mod = _ada(c_pad, w_ada[l], b_ada[l][None, :])[:batch]
        mod3 = mod.reshape(batch, 6, d)
        proj = _inproj(x2, mod3, norm1_w[l][None, :], w_in[l].astype(BF16), seq)
        lane_bcast = lambda v: jnp.broadcast_to(v[:, None, None], (RET_HEADS, 1, HEAD_DIM))
        og = _retention(proj, cos, sin, lane_bcast(ret_decay_fwd[l]), lane_bcast(ret_decay_bwd[l]),
                        ret_norm_w[l].reshape(RET_HEADS, 1, HEAD_DIM), batch, seq)
        merged = _mix(og, proj, conv_w[l], w_ret_o[l].astype(BF16), w_conv_o[l].astype(BF16), seq)
        x1, hp, logits_t = _outproj(merged, x2, mod3, norm2_w[l][None, :], w_out[l].astype(BF16),
                                    router_w[l].T, seq)
        ids_t, w_t, rank_t, cnt = _route(logits_t, router_bias[l][:, None])

        counts = cnt[:, 0]
        nblk = (counts + bm - 1) // bm
        blk_end = jnp.cumsum(nblk)
        pad_start = ((blk_end - nblk) * bm).astype(I32)
        n_used = blk_end[-1:].astype(I32)
        blk_ids = jnp.arange(n_blocks, dtype=I32)
        blk_e = jnp.minimum(jnp.sum((blk_ids[:, None] >= blk_end[None, :]).astype(I32), axis=1),
                            N_EXPERTS - 1)

        blk_first = jnp.concatenate([jnp.ones((1,), I32), (blk_e[1:] != blk_e[:-1]).astype(I32)])
        blk_slot = (jnp.cumsum(blk_first) - 1) % 2
        after = blk_end[blk_e]
        blk_next = jnp.where(after < n_used[0], blk_e[jnp.minimum(after, n_blocks - 1)], -1).astype(I32)

        xs = _dispatch(pad_start, counts, hp, ids_t, rank_t, n_rows, bm)
        ys = _experts(blk_e, blk_first, blk_slot.astype(I32), blk_next, n_used, xs,
                      w_gate[l], w_up[l], w_down[l], n_blocks, bm)
        x2 = _combine(pad_start, x1, hp, w_t.T, mod3, final_norm_w[None, :], ws_gate[l].astype(BF16),
                      ws_up[l].astype(BF16), ws_down[l].astype(BF16), ids_t, rank_t, ys, seq)
    return x2.reshape(batch, seq, d)
```

```python
import functools

import jax
import jax.numpy as jnp
from jax import lax
from jax.experimental import pallas as pl
from jax.experimental.pallas import tpu as pltpu

F32 = jnp.float32
BF16 = jnp.bfloat16
I32 = jnp.int32

EPS = 1e-6
RET_HEADS = 8
HEAD_DIM = 128
ROPE_BASE = 10000.0
N_EXPERTS = 64
TOP_K = 8
N_GROUPS = 8
TOPK_GROUPS = 4
GROUP_SIZE = N_EXPERTS // N_GROUPS
ROUTED_SCALE = 2.5

V7X_VMEM_BYTES = 64 * 1024 * 1024
VMEM_LIMIT = V7X_VMEM_BYTES - 8 * 1024 * 1024
BF16_SUBLANES = 16

RET_CHUNK = 512
EXPERT_BLOCK = 256
NEG_INF = float("-inf")


def _params(n_axes):
    return pltpu.CompilerParams(dimension_semantics=("arbitrary",) * n_axes,
                                vmem_limit_bytes=VMEM_LIMIT)


def _sigmoid(x):
    return 1.0 / (1.0 + jnp.exp(-x))


def _silu(x):
    return x * _sigmoid(x)


LANES = 128
TOKEN_WORD_ROWS = 8
TOKEN_ROWS = 2 * TOKEN_WORD_ROWS


def _store_tokens(ref, x, words_ref):
    n, m = x.shape[0], x.shape[1] // 2
    assert m == TOKEN_WORD_ROWS * LANES
    packed = pltpu.pack_elementwise([x[:, :m], x[:, m:]], packed_dtype=BF16)
    for s in range(TOKEN_WORD_ROWS):
        words_ref[pl.ds(s, n, stride=TOKEN_WORD_ROWS), :] = packed[:, s * LANES:(s + 1) * LANES]
    ref[...] = pltpu.bitcast(words_ref[...], BF16)


def _load_tokens(tiles, words_ref):
    n = tiles.shape[0] // TOKEN_ROWS
    words_ref[...] = pltpu.bitcast(tiles, jnp.uint32)
    p = jnp.concatenate([words_ref[pl.ds(s, n, stride=TOKEN_WORD_ROWS), :]
                         for s in range(TOKEN_WORD_ROWS)], axis=1)
    a = pltpu.unpack_elementwise(p, index=0, packed_dtype=BF16, unpacked_dtype=F32)
    b = pltpu.unpack_elementwise(p, index=1, packed_dtype=BF16, unpacked_dtype=F32)
    return jnp.concatenate([a, b], axis=1)


def _token(row):
    return pl.ds(pl.multiple_of(row * TOKEN_ROWS, TOKEN_ROWS), TOKEN_ROWS)


def _words_scratch(n_tokens):
    return pltpu.VMEM((n_tokens * TOKEN_WORD_ROWS, LANES), jnp.uint32)


def _ada_kernel(c_ref, w_ref, b_ref, o_ref):
    s = _silu(c_ref[...]).astype(BF16)
    o_ref[...] = jnp.dot(s, w_ref[...].astype(BF16), preferred_element_type=F32) + b_ref[...]


def _ada(c_pad, w_ada, b_ada):
    m, d = c_pad.shape
    n = w_ada.shape[1]
    tn = 1024
    return pl.pallas_call(
        _ada_kernel, grid=(n // tn,),
        in_specs=[pl.BlockSpec((m, d), lambda j: (0, 0)),
                  pl.BlockSpec((d, tn), lambda j: (0, j)),
                  pl.BlockSpec((1, tn), lambda j: (0, j))],
        out_specs=pl.BlockSpec((m, tn), lambda j: (0, j)),
        out_shape=jax.ShapeDtypeStruct((m, n), F32),
        compiler_params=_params(1), name="ada")(c_pad, w_ada, b_ada)


def _inproj_kernel(x_ref, mod_ref, nw_ref, w_ref, o_ref, h_ref):
    @pl.when(pl.program_id(1) == 0)
    def _():
        x = x_ref[...]
        y = x * lax.rsqrt(jnp.mean(x * x, axis=-1, keepdims=True) + EPS) * nw_ref[...]
        m = mod_ref[0]
        h_ref[...] = (y * (1.0 + m[1:2, :]) + m[0:1, :]).astype(BF16)

    o_ref[...] = jnp.dot(h_ref[...], w_ref[...], preferred_element_type=F32).astype(o_ref.dtype)


def _inproj(x2, mod3, norm_w, w_in_b, seq):
    t, d = x2.shape
    n = w_in_b.shape[1]
    tm, tn = min(1024, seq), 1024
    tiles_per_seq = seq // tm
    return pl.pallas_call(
        _inproj_kernel, grid=(t // tm, n // tn),
        in_specs=[pl.BlockSpec((tm, d), lambda i, j: (i, 0)),
                  pl.BlockSpec((1, 6, d), lambda i, j: (i // tiles_per_seq, 0, 0)),
                  pl.BlockSpec((1, d), lambda i, j: (0, 0)),
                  pl.BlockSpec((d, tn), lambda i, j: (0, j))],
        out_specs=pl.BlockSpec((tm, tn), lambda i, j: (i, j)),
        out_shape=jax.ShapeDtypeStruct((t, n), BF16),
        scratch_shapes=[pltpu.VMEM((tm, d), BF16)],
        compiler_params=_params(2), name="inproj")(x2, mod3, norm_w, w_in_b)


def _log_sigmoid(x):
    return jnp.minimum(x, 0.0) - jnp.log1p(jnp.exp(-jnp.abs(x)))


def _ret_kernel(q_ref, k_ref, v_ref, g_ref, cos_ref, sin_ref, df_ref, db_ref, nw_ref, o_ref,
                mask_ref, qf_ref, qb_ref, kf_ref, kb_ref, dec_ref, qr_ref, kr_ref, acc_ref, *, chunk):
    seq, d = q_ref.shape
    n = seq // chunk
    c = chunk

    @pl.when(pl.program_id(1) == 0)
    def _():
        lgf = _log_sigmoid(df_ref[0])
        lgb = _log_sigmoid(db_ref[0])
        ii = lax.broadcasted_iota(I32, (c, c), 0)
        jj = lax.broadcasted_iota(I32, (c, c), 1)
        diff = (ii - jj).astype(F32)
        lgf_c = jnp.concatenate([lgf] * (c // d), axis=1)
        lgb_c = jnp.concatenate([lgb] * (c // d), axis=1)
        mask_ref[...] = jnp.where(diff >= 0.0,
                                  jnp.exp(lgf_c * jnp.maximum(diff, 0.0)),
                                  jnp.exp(lgb_c * jnp.maximum(-diff, 0.0)))
        pos = lax.broadcasted_iota(I32, (c, d), 0).astype(F32)
        qf_ref[...] = jnp.exp(lgf * (pos + 1.0))
        qb_ref[...] = jnp.exp(lgb * (c - pos))
        kf_ref[...] = jnp.exp(lgf * (c - 1.0 - pos))
        kb_ref[...] = jnp.exp(lgb * pos)
        dec_ref[0:1, :] = jnp.exp(lgf * c)
        dec_ref[1:2, :] = jnp.exp(lgb * c)

    scale = d ** -0.5
    nt = (((1,), (1,)), ((), ()))
    sls = [pl.ds(i * c, c) for i in range(n)]

    for sl in sls:
        cs = cos_ref[sl, :]
        sn = sin_ref[sl, :]
        q = q_ref[sl, :].astype(F32)
        k = k_ref[sl, :].astype(F32)
        qr_ref[sl, :] = q * cs + pltpu.roll(q, d // 2, 1) * sn
        kr_ref[sl, :] = (k * cs + pltpu.roll(k, d // 2, 1) * sn) * scale

    for sl in sls:
        s = lax.dot_general(qr_ref[sl, :].astype(BF16), kr_ref[sl, :].astype(BF16), nt,
                            preferred_element_type=F32)
        p = (s * mask_ref[...]).astype(BF16)
        acc_ref[sl, :] = jnp.dot(p, v_ref[sl, :], preferred_element_type=F32)

    def kv_state(sl, kw_ref):
        kw = (kr_ref[sl, :] * kw_ref[...]).T.astype(BF16)
        return jnp.dot(kw, v_ref[sl, :], preferred_element_type=F32)

    def scan(order, qw_ref, kw_ref, dec):
        st = jnp.zeros((d, d), F32)
        for idx, ci in enumerate(order):
            sl = sls[ci]
            if idx > 0:
                qw = (qr_ref[sl, :] * qw_ref[...]).astype(BF16)
                acc_ref[sl, :] += jnp.dot(qw, st.astype(BF16), preferred_element_type=F32)
            if idx < n - 1:
                st = dec * st + kv_state(sl, kw_ref)

    scan(list(range(n)), qf_ref, kf_ref, dec_ref[0:1, :])
    scan(list(range(n - 1, -1, -1)), qb_ref, kb_ref, dec_ref[1:2, :])

    nw = nw_ref[0]
    for sl in sls:
        o = acc_ref[sl, :]
        on = o * lax.rsqrt(jnp.mean(o * o, axis=-1, keepdims=True) + EPS) * nw
        g = g_ref[sl, :].astype(F32)
        o_ref[sl, :] = (_silu(g) * on).astype(o_ref.dtype)


def _retention(proj, cos, sin, dec_f, dec_b, ret_norm_w, batch, seq):
    h, d = RET_HEADS, HEAD_DIM
    c = min(RET_CHUNK, seq)
    col = lambda off: pl.BlockSpec((seq, d), lambda hh, b: (b, off + hh))
    per_head = pl.BlockSpec((1, 1, d), lambda hh, b: (hh, 0, 0))
    table = pl.BlockSpec((seq, d), lambda hh, b: (0, 0))
    vm = lambda shape, dt=F32: pltpu.VMEM(shape, dt)
    return pl.pallas_call(
        functools.partial(_ret_kernel, chunk=c), grid=(h, batch),
        in_specs=[col(0), col(h), col(2 * h), col(3 * h), table, table, per_head, per_head, per_head],
        out_specs=pl.BlockSpec((seq, d), lambda hh, b: (b, hh)),
        out_shape=jax.ShapeDtypeStruct((batch * seq, h * d), BF16),
        scratch_shapes=[vm((c, c)), vm((c, d)), vm((c, d)), vm((c, d)), vm((c, d)), vm((8, d)),
                        vm((seq, d)), vm((seq, d)), vm((seq, d))],
        compiler_params=_params(2), name="retention",
    )(proj, proj, proj, proj, cos, sin, dec_f, dec_b, ret_norm_w)


def _mix_kernel(og_ref, cb_ref, cc_ref, cu_ref, ccp_ref, cup_ref, ccn_ref, cun_ref, gr_ref, gc_ref,
                cw_ref, wr_ref, wc_ref, o_ref, z_ref, *, tiles_per_seq):
    i = pl.program_id(0)

    @pl.when(pl.program_id(1) == 0)
    def _():
        tm = cc_ref.shape[0]
        u = cc_ref[...].astype(F32) * cu_ref[...].astype(F32)
        pos = i % tiles_per_seq
        last = BF16_SUBLANES - 1
        u_before = ccp_ref[last:last + 1, :].astype(F32) * cup_ref[last:last + 1, :].astype(F32)
        u_before = jnp.where(pos == 0, 0.0, u_before)
        u_after = ccn_ref[0:1, :].astype(F32) * cun_ref[0:1, :].astype(F32)
        u_after = jnp.where(pos == tiles_per_seq - 1, 0.0, u_after)
        row = lax.broadcasted_iota(I32, u.shape, 0)
        u_prev = jnp.where(row == 0, u_before, pltpu.roll(u, 1, 0))
        u_next = jnp.where(row == tm - 1, u_after, pltpu.roll(u, tm - 1, 0))
        cw = cw_ref[...]
        y = cw[0:1, :] * u_prev + cw[1:2, :] * u + cw[2:3, :] * u_next
        z_ref[...] = (cb_ref[...].astype(F32) * y).astype(BF16)

    yr = jnp.dot(og_ref[...], wr_ref[...], preferred_element_type=F32)
    yc = jnp.dot(z_ref[...], wc_ref[...], preferred_element_type=F32)
    merged = _sigmoid(gr_ref[...].astype(F32)) * yr + _sigmoid(gc_ref[...].astype(F32)) * yc
    o_ref[...] = merged.astype(o_ref.dtype)


def _mix(og, proj, conv_w, w_ret_o_b, w_conv_o_b, seq):
    t, dr = og.shape
    d = w_ret_o_b.shape[1]
    tm = min(512, seq)
    tn = dr
    tiles_per_seq = seq // tm
    hb = tm // BF16_SUBLANES
    n_hblk = t // BF16_SUBLANES
    wide = lambda off: pl.BlockSpec((tm, dr), lambda i, j: (i, off))
    before = lambda off: pl.BlockSpec((BF16_SUBLANES, dr), lambda i, j: (jnp.maximum(i * hb - 1, 0), off))
    after = lambda off: pl.BlockSpec((BF16_SUBLANES, dr), lambda i, j: (jnp.minimum((i + 1) * hb, n_hblk - 1), off))
    gate = lambda off: pl.BlockSpec((tm, tn), lambda i, j: (i, off + j))
    return pl.pallas_call(
        functools.partial(_mix_kernel, tiles_per_seq=tiles_per_seq), grid=(t // tm, d // tn),
        in_specs=[pl.BlockSpec((tm, dr), lambda i, j: (i, 0)),
                  wide(4), wide(5), wide(6), before(5), before(6), after(5), after(6),
                  gate(7), gate(7 + d // tn),
                  pl.BlockSpec((3, dr), lambda i, j: (0, 0)),
                  pl.BlockSpec((dr, tn), lambda i, j: (0, j)),
                  pl.BlockSpec((dr, tn), lambda i, j: (0, j))],
        out_specs=pl.BlockSpec((tm, tn), lambda i, j: (i, j)),
        out_shape=jax.ShapeDtypeStruct((t, d), BF16),
        scratch_shapes=[pltpu.VMEM((tm, dr), BF16)],
        compiler_params=_params(2), name="mix",
    )(og, proj, proj, proj, proj, proj, proj, proj, proj, proj, conv_w, w_ret_o_b, w_conv_o_b)


def _outproj_kernel(m_ref, x_ref, mod_ref, nw_ref, w_ref, rwt_ref, x1_ref, hp_ref, lg_ref, words_ref):
    y = jnp.dot(m_ref[...], w_ref[...], preferred_element_type=F32)
    m = mod_ref[0]
    x1 = x_ref[...] + m[2:3, :] * y
    x1_ref[...] = x1
    hn = x1 * lax.rsqrt(jnp.mean(x1 * x1, axis=-1, keepdims=True) + EPS) * nw_ref[...]
    h = hn * (1.0 + m[4:5, :]) + m[3:4, :]
    _store_tokens(hp_ref, h, words_ref)
    h_hi = h.astype(BF16)
    h_lo = (h - h_hi.astype(F32)).astype(BF16)
    rw = rwt_ref[...]
    r_hi = rw.astype(BF16)
    r_lo = (rw - r_hi.astype(F32)).astype(BF16)
    nt = (((1,), (1,)), ((), ()))
    dot = functools.partial(lax.dot_general, dimension_numbers=nt, preferred_element_type=F32)
    lg_ref[...] = dot(r_hi, h_hi) + dot(r_hi, h_lo) + dot(r_lo, h_hi)


def _outproj(merged, x2, mod3, norm_w, w_out_b, router_wt, seq):
    t, d = x2.shape
    e = router_wt.shape[0]
    tm = min(512, seq)
    tiles_per_seq = seq // tm
    return pl.pallas_call(
        _outproj_kernel, grid=(t // tm,),
        in_specs=[pl.BlockSpec((tm, d), lambda i: (i, 0)),
                  pl.BlockSpec((tm, d), lambda i: (i, 0)),
                  pl.BlockSpec((1, 6, d), lambda i: (i // tiles_per_seq, 0, 0)),
                  pl.BlockSpec((1, d), lambda i: (0, 0)),
                  pl.BlockSpec((d, d), lambda i: (0, 0)),
                  pl.BlockSpec((e, d), lambda i: (0, 0))],
        out_specs=[pl.BlockSpec((tm, d), lambda i: (i, 0)),
                   pl.BlockSpec((TOKEN_ROWS * tm, LANES), lambda i: (i, 0)),
                   pl.BlockSpec((e, tm), lambda i: (0, i))],
        out_shape=[jax.ShapeDtypeStruct((t, d), F32),
                   jax.ShapeDtypeStruct((TOKEN_ROWS * t, LANES), BF16),
                   jax.ShapeDtypeStruct((e, t), F32)],
        scratch_shapes=[_words_scratch(tm)],
        compiler_params=_params(1), name="outproj",
    )(merged, x2, mod3, norm_w, w_out_b, router_wt)


def _first_max(x, iota, sentinel):
    m = jnp.max(x, axis=0, keepdims=True)
    idx = jnp.min(jnp.where(x == m, iota, sentinel), axis=0, keepdims=True)
    return m, idx, iota == idx


def _route_kernel(lg_ref, bias_ref, ids_ref, w_ref, rank_ref, cnt_ref, carry_ref):
    i = pl.program_id(0)
    e, tk = lg_ref.shape

    @pl.when(i == 0)
    def _():
        carry_ref[...] = jnp.zeros_like(carry_ref)

    s = _sigmoid(lg_ref[...])
    biased = s + bias_ref[...]
    sub = lax.broadcasted_iota(I32, (GROUP_SIZE, tk), 0)
    group_rows = []
    for g in range(N_GROUPS):
        xg = biased[g * GROUP_SIZE:(g + 1) * GROUP_SIZE, :]
        m1, _, pick = _first_max(xg, sub, GROUP_SIZE)
        m2 = jnp.max(jnp.where(pick, NEG_INF, xg), axis=0, keepdims=True)
        group_rows.append(m1 + m2)
    gs = jnp.concatenate(group_rows, axis=0)
    gsub = lax.broadcasted_iota(I32, (N_GROUPS, tk), 0)
    sel = jnp.zeros((N_GROUPS, tk), F32)
    for _ in range(TOPK_GROUPS):
        _, _, pick = _first_max(gs, gsub, N_GROUPS)
        sel = jnp.where(pick, 1.0, sel)
        gs = jnp.where(pick, NEG_INF, gs)
    masked_rows = []
    for g in range(N_GROUPS):
        xg = biased[g * GROUP_SIZE:(g + 1) * GROUP_SIZE, :]
        masked_rows.append(jnp.where(sel[g:g + 1, :] > 0.5, xg, NEG_INF))
    masked = jnp.concatenate(masked_rows, axis=0)

    eio = lax.broadcasted_iota(I32, (e, tk), 0)
    chosen = jnp.zeros((e, tk), F32)
    ids, top_s = [], []
    for _ in range(TOP_K):
        _, idx, pick = _first_max(masked, eio, e)
        ids.append(idx)
        top_s.append(jnp.sum(jnp.where(pick, s, 0.0), axis=0, keepdims=True))
        chosen = jnp.where(pick, 1.0, chosen)
        masked = jnp.where(pick, NEG_INF, masked)
    total = top_s[0]
    for ts in top_s[1:]:
        total = total + ts

    before = (lax.broadcasted_iota(I32, (tk, tk), 0) < lax.broadcasted_iota(I32, (tk, tk), 1))
    upper = jnp.where(before, 1.0, 0.0).astype(BF16)
    rank = jnp.dot(chosen.astype(BF16), upper, preferred_element_type=F32) + carry_ref[:, 0:1]
    for k in range(TOP_K):
        ids_ref[k:k + 1, :] = ids[k]
        w_ref[k:k + 1, :] = top_s[k] / total * ROUTED_SCALE
        rk = jnp.sum(jnp.where(eio == ids[k], rank, 0.0), axis=0, keepdims=True)
        rank_ref[k:k + 1, :] = rk.astype(I32)
    carry_ref[...] = carry_ref[...] + jnp.sum(chosen, axis=1, keepdims=True)
    cnt_ref[...] = carry_ref[...].astype(I32)


def _route(logits_t, bias_col):
    e, t = logits_t.shape
    tk = min(512, t)
    row8 = lambda dt: jax.ShapeDtypeStruct((TOP_K, t), dt)
    blk8 = pl.BlockSpec((TOP_K, tk), lambda i: (0, i))
    return pl.pallas_call(
        _route_kernel, grid=(t // tk,),
        in_specs=[pl.BlockSpec((e, tk), lambda i: (0, i)),
                  pl.BlockSpec((e, 1), lambda i: (0, 0))],
        out_specs=[blk8, blk8, blk8, pl.BlockSpec((e, 128), lambda i: (0, 0))],
        out_shape=[row8(I32), row8(F32), row8(I32), jax.ShapeDtypeStruct((e, 128), I32)],
        scratch_shapes=[pltpu.VMEM((e, 128), F32)],
        compiler_params=_params(1), name="route",
    )(logits_t, bias_col)


def _dispatch_kernel(pad_ref, cnt_ref, hp_ref, slot_ref, xs_ref, zero_ref, sem, zsem, *, bm):
    i = pl.program_id(0)
    td = hp_ref.shape[0] // TOKEN_ROWS
    n_blocks = xs_ref.shape[0] // (bm * TOKEN_ROWS)

    @pl.when(i == 0)
    def _():
        zero_ref[...] = jnp.zeros_like(zero_ref)
        last_e = N_EXPERTS - 1
        used_blocks = pad_ref[last_e] // bm + (cnt_ref[last_e] + bm - 1) // bm

        def zero_block(j):
            n = bm * TOKEN_ROWS
            return pltpu.make_async_copy(zero_ref, xs_ref.at[pl.ds(pl.multiple_of(j * n, n), n)], zsem)

        def block_start(j, c):
            zero_block(j).start()
            return c

        def block_wait(j, c):
            zero_block(j).wait()
            return c

        lax.fori_loop(used_blocks, n_blocks, block_start, 0)
        lax.fori_loop(used_blocks, n_blocks, block_wait, 0)

        def zero_row(row):
            return pltpu.make_async_copy(zero_ref.at[_token(0)], xs_ref.at[_token(row)], zsem)

        def per_expert(e, c):
            first = pad_ref[e] + cnt_ref[e]
            last = pad_ref[e] + (cnt_ref[e] + bm - 1) // bm * bm

            def start_body(row, cc):
                zero_row(row).start()
                return cc

            def wait_body(row, cc):
                zero_row(row).wait()
                return cc

            lax.fori_loop(first, last, start_body, 0)
            lax.fori_loop(first, last, wait_body, 0)
            return c

        lax.fori_loop(0, N_EXPERTS, per_expert, 0)

    def body(t, c):
        for k in range(TOP_K):
            slot = slot_ref[t * TOP_K + k]
            pltpu.make_async_copy(hp_ref.at[_token(t)], xs_ref.at[_token(slot)], sem).start(priority=k % 2)
        return c

    lax.fori_loop(0, td, body, 0)
    for k in range(TOP_K):
        pltpu.make_async_copy(hp_ref, xs_ref.at[pl.ds(0, td * TOKEN_ROWS)], sem).wait()


def _dispatch(pad_start, counts, hp, slots, n_rows, bm):
    t = hp.shape[0] // TOKEN_ROWS
    td = min(1024, t)
    grid_spec = pltpu.PrefetchScalarGridSpec(
        num_scalar_prefetch=2, grid=(t // td,),
        in_specs=[pl.BlockSpec((td * TOKEN_ROWS, LANES), lambda i, *_: (i, 0)),
                  pl.BlockSpec((td * TOP_K,), lambda i, *_: (i,), memory_space=pltpu.SMEM)],
        out_specs=pl.BlockSpec(memory_space=pl.ANY),
        scratch_shapes=[pltpu.VMEM((bm * TOKEN_ROWS, LANES), BF16),
                        pltpu.SemaphoreType.DMA(()), pltpu.SemaphoreType.DMA(())])
    return pl.pallas_call(
        functools.partial(_dispatch_kernel, bm=bm), grid_spec=grid_spec,
        out_shape=jax.ShapeDtypeStruct((n_rows * TOKEN_ROWS, LANES), BF16),
        compiler_params=_params(1), name="dispatch",
    )(pad_start, counts, hp, slots)


def _expert_kernel(be_ref, first_ref, slot_ref, nxt_ref, nu_ref, x_ref, wg_hbm, wu_hbm, wd_hbm, o_ref,
                   words_ref, wg32, wu32, wd32, wg16, wu16, wd16, wsem):
    i = pl.program_id(0)

    def weight_copies(e, s):
        return (pltpu.make_async_copy(wg_hbm.at[e], wg32.at[s], wsem.at[s, 0]),
                pltpu.make_async_copy(wu_hbm.at[e], wu32.at[s], wsem.at[s, 1]),
                pltpu.make_async_copy(wd_hbm.at[e], wd32.at[s], wsem.at[s, 2]))

    @pl.when(i == 0)
    def _():
        for cp in weight_copies(be_ref[0], 0):
            cp.start()

    @pl.when(jnp.logical_and(first_ref[i] == 1, i < nu_ref[0]))
    def _():
        s = slot_ref[i]
        for cp in weight_copies(be_ref[i], s):
            cp.wait()

        @pl.when(nxt_ref[i] >= 0)
        def _():
            for cp in weight_copies(nxt_ref[i], 1 - s):
                cp.start()

        wg16[...] = wg32[s].astype(BF16)
        wu16[...] = wu32[s].astype(BF16)
        wd16[...] = wd32[s].astype(BF16)

    @pl.when(i < nu_ref[0])
    def _():
        x = _load_tokens(x_ref[...], words_ref).astype(BF16)
        g = jnp.dot(x, wg16[...], preferred_element_type=F32)
        u = jnp.dot(x, wu16[...], preferred_element_type=F32)
        mid = (_silu(g) * u).astype(BF16)
        _store_tokens(o_ref, jnp.dot(mid, wd16[...], preferred_element_type=F32), words_ref)

    @pl.when(i >= nu_ref[0])
    def _():
        o_ref[...] = jnp.zeros_like(o_ref)


def _experts(blk_e, blk_first, blk_slot, blk_next, n_used, xs, w_gate, w_up, w_down, n_blocks, bm):
    _, d, de = w_gate.shape
    x_map = lambda i, be, fi, sl, nx, nu: (jnp.minimum(i, nu[0] - 1), 0)
    hbm = pl.BlockSpec(memory_space=pl.ANY)
    grid_spec = pltpu.PrefetchScalarGridSpec(
        num_scalar_prefetch=5, grid=(n_blocks,),
        in_specs=[pl.BlockSpec((bm * TOKEN_ROWS, LANES), x_map), hbm, hbm, hbm],
        out_specs=pl.BlockSpec((bm * TOKEN_ROWS, LANES), lambda i, *_: (i, 0)),
        scratch_shapes=[_words_scratch(bm),
                        pltpu.VMEM((2, d, de), F32), pltpu.VMEM((2, d, de), F32), pltpu.VMEM((2, de, d), F32),
                        pltpu.VMEM((d, de), BF16), pltpu.VMEM((d, de), BF16), pltpu.VMEM((de, d), BF16),
                        pltpu.SemaphoreType.DMA((2, 3))])
    return pl.pallas_call(
        _expert_kernel, grid_spec=grid_spec,
        out_shape=jax.ShapeDtypeStruct((n_blocks * bm * TOKEN_ROWS, LANES), BF16),
        compiler_params=_params(1), name="experts",
    )(blk_e, blk_first, blk_slot, blk_next, n_used, xs, w_gate, w_up, w_down)


def _combine_kernel(x1_ref, hp_ref, w_ref, mod_ref, fw_ref, wsg_ref, wsu_ref, wsd_ref,
                    cur_ref, nxt_ref, ys_ref, o_ref, ybuf, words_ref, sem, *, n_tiles):
    i = pl.program_id(0)
    tc = x1_ref.shape[0]

    def issue(slot_ref, buf):
        def body(t, c):
            for k in range(TOP_K):
                slot = slot_ref[t * TOP_K + k]
                pltpu.make_async_copy(ys_ref.at[_token(slot)], ybuf.at[buf, k, _token(t)],
                                      sem.at[buf]).start(priority=k % 2)
            return c
        lax.fori_loop(0, tc, body, 0)

    @pl.when(i == 0)
    def _():
        issue(cur_ref, 0)

    @pl.when(i + 1 < n_tiles)
    def _():
        issue(nxt_ref, (i + 1) % 2)

    buf = i % 2
    for k in range(TOP_K):
        pltpu.make_async_copy(ys_ref.at[pl.ds(0, tc * TOKEN_ROWS)], ybuf.at[buf, k], sem.at[buf]).wait()

    h = _load_tokens(hp_ref[...], words_ref).astype(BF16)
    sg = jnp.dot(h, wsg_ref[...], preferred_element_type=F32)
    su = jnp.dot(h, wsu_ref[...], preferred_element_type=F32)
    shared = jnp.dot((_silu(sg) * su).astype(BF16), wsd_ref[...], preferred_element_type=F32)
    w = w_ref[...]
    routed = w[:, 0:1] * _load_tokens(ybuf[buf, 0], words_ref)
    for k in range(1, TOP_K):
        routed = routed + w[:, k:k + 1] * _load_tokens(ybuf[buf, k], words_ref)
    x = x1_ref[...] + mod_ref[0][5:6, :] * (routed + shared)
    o_ref[...] = x * lax.rsqrt(jnp.mean(x * x, axis=-1, keepdims=True) + EPS) * fw_ref[...]


def _combine(x1, hp, w_tok, mod3, final_w, wsg_b, wsu_b, wsd_b, slots, ys, seq):
    t, d = x1.shape
    ds = wsg_b.shape[1]
    tc = min(256, seq)
    n_tiles = t // tc
    tiles_per_seq = seq // tc
    cur = pl.BlockSpec((tc * TOP_K,), lambda i: (i,), memory_space=pltpu.SMEM)
    nxt = pl.BlockSpec((tc * TOP_K,), lambda i: (jnp.minimum(i + 1, n_tiles - 1),), memory_space=pltpu.SMEM)
    full = lambda shape: pl.BlockSpec(shape, lambda i, *_: (0,) * len(shape))
    grid_spec = pltpu.PrefetchScalarGridSpec(
        num_scalar_prefetch=0, grid=(n_tiles,),
        in_specs=[pl.BlockSpec((tc, d), lambda i, *_: (i, 0)),
                  pl.BlockSpec((tc * TOKEN_ROWS, LANES), lambda i, *_: (i, 0)),
                  pl.BlockSpec((tc, TOP_K), lambda i, *_: (i, 0)),
                  pl.BlockSpec((1, 6, d), lambda i, *_: (i // tiles_per_seq, 0, 0)),
                  full((1, d)), full((d, ds)), full((d, ds)), full((ds, d)),
                  cur, nxt,
                  pl.BlockSpec(memory_space=pl.ANY)],
        out_specs=pl.BlockSpec((tc, d), lambda i, *_: (i, 0)),
        scratch_shapes=[pltpu.VMEM((2, TOP_K, tc * TOKEN_ROWS, LANES), BF16), _words_scratch(tc),
                        pltpu.SemaphoreType.DMA((2,))])
    return pl.pallas_call(
        functools.partial(_combine_kernel, n_tiles=n_tiles), grid_spec=grid_spec,
        out_shape=jax.ShapeDtypeStruct((t, d), F32),
        compiler_params=_params(1), name="combine",
    )(x1, hp, w_tok, mod3, final_w, wsg_b, wsu_b, wsd_b, slots, slots, ys)


def _rope_tables(seq, d):
    half = d // 2
    inv = ROPE_BASE ** (-jnp.arange(half, dtype=F32) / half)
    ang = jnp.arange(seq, dtype=F32)[:, None] * inv[None, :]
    cos, sin = jnp.cos(ang), jnp.sin(ang)
    return jnp.concatenate([cos, cos], axis=1), jnp.concatenate([-sin, sin], axis=1)


def kernel(x, c, w_ada, b_ada, norm1_w, w_in, ret_decay_fwd, ret_decay_bwd, ret_norm_w, w_ret_o, conv_w,
           w_conv_o, w_out, norm2_w, router_w, router_bias, w_gate, w_up, w_down, ws_gate, ws_up, ws_down,
           final_norm_w):
    batch, seq, d = x.shape
    depth = w_ada.shape[0]
    t = batch * seq
    bm = EXPERT_BLOCK
    n_blocks = (t * TOP_K + N_EXPERTS * (bm - 1)) // bm
    n_rows = n_blocks * bm
    cos, sin = _rope_tables(seq, HEAD_DIM)
    c_pad = jnp.pad(c, ((0, BF16_SUBLANES - batch % BF16_SUBLANES), (0, 0)))
    x2 = x.reshape(t, d)

    assert depth == 1, "the final norm is fused into the last stage of a single layer"
    for l in range(depth):
        mod = _ada(c_pad, w_ada[l], b_ada[l][None, :])[:batch]
        mod3 = mod.reshape(batch, 6, d)
        proj = _inproj(x2, mod3, norm1_w[l][None, :], w_in[l].astype(BF16), seq)
        lane_bcast = lambda v: jnp.broadcast_to(v[:, None, None], (RET_HEADS, 1, HEAD_DIM))
        og = _retention(proj, cos, sin, lane_bcast(ret_decay_fwd[l]), lane_bcast(ret_decay_bwd[l]),
                        ret_norm_w[l].reshape(RET_HEADS, 1, HEAD_DIM), batch, seq)
        merged = _mix(og, proj, conv_w[l], w_ret_o[l].astype(BF16), w_conv_o[l].astype(BF16), seq)
        x1, hp, logits_t = _outproj(merged, x2, mod3, norm2_w[l][None, :], w_out[l].astype(BF16),
                                    router_w[l].T, seq)
        ids_t, w_t, rank_t, cnt = _route(logits_t, router_bias[l][:, None])

        counts = cnt[:, 0]
        nblk = (counts + bm - 1) // bm
        blk_end = jnp.cumsum(nblk)
        pad_start = ((blk_end - nblk) * bm).astype(I32)
        n_used = blk_end[-1:].astype(I32)
        blk_ids = jnp.arange(n_blocks, dtype=I32)
        blk_e = jnp.minimum(jnp.sum((blk_ids[:, None] >= blk_end[None, :]).astype(I32), axis=1),
                            N_EXPERTS - 1)

        blk_first = jnp.concatenate([jnp.ones((1,), I32), (blk_e[1:] != blk_e[:-1]).astype(I32)])
        blk_slot = (jnp.cumsum(blk_first) - 1) % 2
        after = blk_end[blk_e]
        blk_next = jnp.where(after < n_used[0], blk_e[jnp.minimum(after, n_blocks - 1)], -1).astype(I32)

        onehot = ids_t[:, :, None] == jnp.arange(N_EXPERTS, dtype=I32)
        slots_t = rank_t + jnp.sum(jnp.where(onehot, pad_start, 0), axis=-1)
        slots = slots_t.T.reshape(t * TOP_K)
        xs = _dispatch(pad_start, counts, hp, slots, n_rows, bm)
        ys = _experts(blk_e, blk_first, blk_slot.astype(I32), blk_next, n_used, xs,
                      w_gate[l], w_up[l], w_down[l], n_blocks, bm)
        x2 = _combine(x1, hp, w_t.T, mod3, final_norm_w[None, :], ws_gate[l].astype(BF16),
                      ws_up[l].astype(BF16), ws_down[l].astype(BF16), slots, ys, seq)
    return x2.reshape(batch, seq, d)
```

```python
import functools

import jax
import jax.numpy as jnp
from jax import lax
from jax.experimental import pallas as pl
from jax.experimental.pallas import tpu as pltpu

F32 = jnp.float32
BF16 = jnp.bfloat16
I32 = jnp.int32

EPS = 1e-6
RET_HEADS = 8
HEAD_DIM = 128
ROPE_BASE = 10000.0
N_EXPERTS = 64
TOP_K = 8
N_GROUPS = 8
TOPK_GROUPS = 4
GROUP_SIZE = N_EXPERTS // N_GROUPS
ROUTED_SCALE = 2.5

V7X_VMEM_BYTES = 64 * 1024 * 1024
VMEM_LIMIT = V7X_VMEM_BYTES - 8 * 1024 * 1024
BF16_SUBLANES = 16

RET_CHUNK = 512
EXPERT_BLOCK = 512
NEG_INF = float("-inf")


def _params(n_axes):
    return pltpu.CompilerParams(dimension_semantics=("arbitrary",) * n_axes,
                                vmem_limit_bytes=VMEM_LIMIT)


def _sigmoid(x):
    return 1.0 / (1.0 + jnp.exp(-x))


def _silu(x):
    return x * _sigmoid(x)


LANES = 128
TOKEN_WORD_ROWS = 8
TOKEN_ROWS = 2 * TOKEN_WORD_ROWS


def _store_tokens(ref, x, words_ref):
    n, m = x.shape[0], x.shape[1] // 2
    assert m == TOKEN_WORD_ROWS * LANES
    packed = pltpu.pack_elementwise([x[:, :m], x[:, m:]], packed_dtype=BF16)
    for s in range(TOKEN_WORD_ROWS):
        words_ref[pl.ds(s, n, stride=TOKEN_WORD_ROWS), :] = packed[:, s * LANES:(s + 1) * LANES]
    ref[...] = pltpu.bitcast(words_ref[...], BF16)


def _load_tokens(tiles, words_ref):
    n = tiles.shape[0] // TOKEN_ROWS
    words_ref[...] = pltpu.bitcast(tiles, jnp.uint32)
    p = jnp.concatenate([words_ref[pl.ds(s, n, stride=TOKEN_WORD_ROWS), :]
                         for s in range(TOKEN_WORD_ROWS)], axis=1)
    a = pltpu.unpack_elementwise(p, index=0, packed_dtype=BF16, unpacked_dtype=F32)
    b = pltpu.unpack_elementwise(p, index=1, packed_dtype=BF16, unpacked_dtype=F32)
    return jnp.concatenate([a, b], axis=1)


def _token(row):
    return pl.ds(pl.multiple_of(row * TOKEN_ROWS, TOKEN_ROWS), TOKEN_ROWS)


def _words_scratch(n_tokens):
    return pltpu.VMEM((n_tokens * TOKEN_WORD_ROWS, LANES), jnp.uint32)


def _ada_kernel(c_ref, w_ref, b_ref, o_ref):
    s = _silu(c_ref[...]).astype(BF16)
    o_ref[...] = jnp.dot(s, w_ref[...].astype(BF16), preferred_element_type=F32) + b_ref[...]


def _ada(c_pad, w_ada, b_ada):
    m, d = c_pad.shape
    n = w_ada.shape[1]
    tn = 1024
    return pl.pallas_call(
        _ada_kernel, grid=(n // tn,),
        in_specs=[pl.BlockSpec((m, d), lambda j: (0, 0)),
                  pl.BlockSpec((d, tn), lambda j: (0, j)),
                  pl.BlockSpec((1, tn), lambda j: (0, j))],
        out_specs=pl.BlockSpec((m, tn), lambda j: (0, j)),
        out_shape=jax.ShapeDtypeStruct((m, n), F32),
        compiler_params=_params(1), name="ada")(c_pad, w_ada, b_ada)


def _inproj_kernel(x_ref, mod_ref, nw_ref, w_ref, o_ref, h_ref):
    @pl.when(pl.program_id(1) == 0)
    def _():
        x = x_ref[...]
        y = x * lax.rsqrt(jnp.mean(x * x, axis=-1, keepdims=True) + EPS) * nw_ref[...]
        m = mod_ref[0]
        h_ref[...] = (y * (1.0 + m[1:2, :]) + m[0:1, :]).astype(BF16)

    o_ref[...] = jnp.dot(h_ref[...], w_ref[...], preferred_element_type=F32).astype(o_ref.dtype)


def _inproj(x2, mod3, norm_w, w_in_b, seq):
    t, d = x2.shape
    n = w_in_b.shape[1]
    tm, tn = min(1024, seq), 1024
    tiles_per_seq = seq // tm
    return pl.pallas_call(
        _inproj_kernel, grid=(t // tm, n // tn),
        in_specs=[pl.BlockSpec((tm, d), lambda i, j: (i, 0)),
                  pl.BlockSpec((1, 6, d), lambda i, j: (i // tiles_per_seq, 0, 0)),
                  pl.BlockSpec((1, d), lambda i, j: (0, 0)),
                  pl.BlockSpec((d, tn), lambda i, j: (0, j))],
        out_specs=pl.BlockSpec((tm, tn), lambda i, j: (i, j)),
        out_shape=jax.ShapeDtypeStruct((t, n), BF16),
        scratch_shapes=[pltpu.VMEM((tm, d), BF16)],
        compiler_params=_params(2), name="inproj")(x2, mod3, norm_w, w_in_b)


def _log_sigmoid(x):
    return jnp.minimum(x, 0.0) - jnp.log1p(jnp.exp(-jnp.abs(x)))


def _ret_kernel(q_ref, k_ref, v_ref, g_ref, cos_ref, sin_ref, df_ref, db_ref, nw_ref, o_ref,
                mask_ref, qf_ref, qb_ref, kf_ref, kb_ref, dec_ref, qr_ref, kr_ref, acc_ref, *, chunk):
    seq, d = q_ref.shape
    n = seq // chunk
    c = chunk

    @pl.when(pl.program_id(1) == 0)
    def _():
        lgf = _log_sigmoid(df_ref[0])
        lgb = _log_sigmoid(db_ref[0])
        ii = lax.broadcasted_iota(I32, (c, c), 0)
        jj = lax.broadcasted_iota(I32, (c, c), 1)
        diff = (ii - jj).astype(F32)
        lgf_c = jnp.concatenate([lgf] * (c // d), axis=1)
        lgb_c = jnp.concatenate([lgb] * (c // d), axis=1)
        mask_ref[...] = jnp.where(diff >= 0.0,
                                  jnp.exp(lgf_c * jnp.maximum(diff, 0.0)),
                                  jnp.exp(lgb_c * jnp.maximum(-diff, 0.0)))
        pos = lax.broadcasted_iota(I32, (c, d), 0).astype(F32)
        qf_ref[...] = jnp.exp(lgf * (pos + 1.0))
        qb_ref[...] = jnp.exp(lgb * (c - pos))
        kf_ref[...] = jnp.exp(lgf * (c - 1.0 - pos))
        kb_ref[...] = jnp.exp(lgb * pos)
        dec_ref[0:1, :] = jnp.exp(lgf * c)
        dec_ref[1:2, :] = jnp.exp(lgb * c)

    scale = d ** -0.5
    nt = (((1,), (1,)), ((), ()))
    sls = [pl.ds(i * c, c) for i in range(n)]

    for sl in sls:
        cs = cos_ref[sl, :]
        sn = sin_ref[sl, :]
        q = q_ref[sl, :].astype(F32)
        k = k_ref[sl, :].astype(F32)
        qr_ref[sl, :] = q * cs + pltpu.roll(q, d // 2, 1) * sn
        kr_ref[sl, :] = (k * cs + pltpu.roll(k, d // 2, 1) * sn) * scale

    for sl in sls:
        s = lax.dot_general(qr_ref[sl, :].astype(BF16), kr_ref[sl, :].astype(BF16), nt,
                            preferred_element_type=F32)
        p = (s * mask_ref[...]).astype(BF16)
        acc_ref[sl, :] = jnp.dot(p, v_ref[sl, :], preferred_element_type=F32)

    def kv_state(sl, kw_ref):
        kw = (kr_ref[sl, :] * kw_ref[...]).T.astype(BF16)
        return jnp.dot(kw, v_ref[sl, :], preferred_element_type=F32)

    def scan(order, qw_ref, kw_ref, dec):
        st = jnp.zeros((d, d), F32)
        for idx, ci in enumerate(order):
            sl = sls[ci]
            if idx > 0:
                qw = (qr_ref[sl, :] * qw_ref[...]).astype(BF16)
                acc_ref[sl, :] += jnp.dot(qw, st.astype(BF16), preferred_element_type=F32)
            if idx < n - 1:
                st = dec * st + kv_state(sl, kw_ref)

    scan(list(range(n)), qf_ref, kf_ref, dec_ref[0:1, :])
    scan(list(range(n - 1, -1, -1)), qb_ref, kb_ref, dec_ref[1:2, :])

    nw = nw_ref[0]
    for sl in sls:
        o = acc_ref[sl, :]
        on = o * lax.rsqrt(jnp.mean(o * o, axis=-1, keepdims=True) + EPS) * nw
        g = g_ref[sl, :].astype(F32)
        o_ref[sl, :] = (_silu(g) * on).astype(o_ref.dtype)


def _retention(proj, cos, sin, dec_f, dec_b, ret_norm_w, batch, seq):
    h, d = RET_HEADS, HEAD_DIM
    c = min(RET_CHUNK, seq)
    col = lambda off: pl.BlockSpec((seq, d), lambda hh, b: (b, off + hh))
    per_head = pl.BlockSpec((1, 1, d), lambda hh, b: (hh, 0, 0))
    table = pl.BlockSpec((seq, d), lambda hh, b: (0, 0))
    vm = lambda shape, dt=F32: pltpu.VMEM(shape, dt)
    return pl.pallas_call(
        functools.partial(_ret_kernel, chunk=c), grid=(h, batch),
        in_specs=[col(0), col(h), col(2 * h), col(3 * h), table, table, per_head, per_head, per_head],
        out_specs=pl.BlockSpec((seq, d), lambda hh, b: (b, hh)),
        out_shape=jax.ShapeDtypeStruct((batch * seq, h * d), BF16),
        scratch_shapes=[vm((c, c)), vm((c, d)), vm((c, d)), vm((c, d)), vm((c, d)), vm((8, d)),
                        vm((seq, d)), vm((seq, d)), vm((seq, d))],
        compiler_params=_params(2), name="retention",
    )(proj, proj, proj, proj, cos, sin, dec_f, dec_b, ret_norm_w)


def _mix_kernel(og_ref, cb_ref, cc_ref, cu_ref, ccp_ref, cup_ref, ccn_ref, cun_ref,
                gr0_ref, gr1_ref, gc0_ref, gc1_ref, cw_ref, wr_ref, wc_ref, o_ref, *, tiles_per_seq):
    i = pl.program_id(0)
    tm, dr = cc_ref.shape
    u = cc_ref[...].astype(F32) * cu_ref[...].astype(F32)
    pos = i % tiles_per_seq
    last = BF16_SUBLANES - 1
    u_before = ccp_ref[last:last + 1, :].astype(F32) * cup_ref[last:last + 1, :].astype(F32)
    u_before = jnp.where(pos == 0, 0.0, u_before)
    u_after = ccn_ref[0:1, :].astype(F32) * cun_ref[0:1, :].astype(F32)
    u_after = jnp.where(pos == tiles_per_seq - 1, 0.0, u_after)
    row = lax.broadcasted_iota(I32, u.shape, 0)
    u_prev = jnp.where(row == 0, u_before, pltpu.roll(u, 1, 0))
    u_next = jnp.where(row == tm - 1, u_after, pltpu.roll(u, tm - 1, 0))
    cw = cw_ref[...]
    y = cw[0:1, :] * u_prev + cw[1:2, :] * u + cw[2:3, :] * u_next
    z = (cb_ref[...].astype(F32) * y).astype(BF16)

    yr = jnp.dot(og_ref[...], wr_ref[...], preferred_element_type=F32)
    yc = jnp.dot(z, wc_ref[...], preferred_element_type=F32)
    for lo, gr_ref, gc_ref in ((0, gr0_ref, gc0_ref), (dr, gr1_ref, gc1_ref)):
        merged = (_sigmoid(gr_ref[...].astype(F32)) * yr[:, lo:lo + dr]
                  + _sigmoid(gc_ref[...].astype(F32)) * yc[:, lo:lo + dr])
        o_ref[:, lo:lo + dr] = merged.astype(o_ref.dtype)


def _mix(og, proj, conv_w, w_ret_o_b, w_conv_o_b, seq):
    t, dr = og.shape
    d = w_ret_o_b.shape[1]
    assert d == 2 * dr
    tm = min(512, seq)
    tiles_per_seq = seq // tm
    hb = tm // BF16_SUBLANES
    n_hblk = t // BF16_SUBLANES
    wide = lambda off: pl.BlockSpec((tm, dr), lambda i: (i, off))
    before = lambda off: pl.BlockSpec((BF16_SUBLANES, dr), lambda i: (jnp.maximum(i * hb - 1, 0), off))
    after = lambda off: pl.BlockSpec((BF16_SUBLANES, dr), lambda i: (jnp.minimum((i + 1) * hb, n_hblk - 1), off))
    whole = lambda shape: pl.BlockSpec(shape, lambda i: (0, 0))
    return pl.pallas_call(
        functools.partial(_mix_kernel, tiles_per_seq=tiles_per_seq), grid=(t // tm,),
        in_specs=[wide(0), wide(4), wide(5), wide(6), before(5), before(6), after(5), after(6),
                  wide(7), wide(8), wide(9), wide(10),
                  whole((3, dr)), whole((dr, d)), whole((dr, d))],
        out_specs=pl.BlockSpec((tm, d), lambda i: (i, 0)),
        out_shape=jax.ShapeDtypeStruct((t, d), BF16),
        compiler_params=_params(1), name="mix",
    )(og, proj, proj, proj, proj, proj, proj, proj, proj, proj, proj, proj, conv_w, w_ret_o_b, w_conv_o_b)


def _outproj_kernel(m_ref, x_ref, mod_ref, nw_ref, w_ref, rwt_ref, x1_ref, hp_ref, lg_ref, words_ref):
    y = jnp.dot(m_ref[...], w_ref[...], preferred_element_type=F32)
    m = mod_ref[0]
    x1 = x_ref[...] + m[2:3, :] * y
    x1_ref[...] = x1
    hn = x1 * lax.rsqrt(jnp.mean(x1 * x1, axis=-1, keepdims=True) + EPS) * nw_ref[...]
    h = hn * (1.0 + m[4:5, :]) + m[3:4, :]
    _store_tokens(hp_ref, h, words_ref)
    h_hi = h.astype(BF16)
    h_lo = (h - h_hi.astype(F32)).astype(BF16)
    rw = rwt_ref[...]
    r_hi = rw.astype(BF16)
    r_lo = (rw - r_hi.astype(F32)).astype(BF16)
    nt = (((1,), (1,)), ((), ()))
    dot = functools.partial(lax.dot_general, dimension_numbers=nt, preferred_element_type=F32)
    lg_ref[...] = dot(r_hi, h_hi) + dot(r_hi, h_lo) + dot(r_lo, h_hi)


def _outproj(merged, x2, mod3, norm_w, w_out_b, router_wt, seq):
    t, d = x2.shape
    e = router_wt.shape[0]
    tm = min(512, seq)
    tiles_per_seq = seq // tm
    return pl.pallas_call(
        _outproj_kernel, grid=(t // tm,),
        in_specs=[pl.BlockSpec((tm, d), lambda i: (i, 0)),
                  pl.BlockSpec((tm, d), lambda i: (i, 0)),
                  pl.BlockSpec((1, 6, d), lambda i: (i // tiles_per_seq, 0, 0)),
                  pl.BlockSpec((1, d), lambda i: (0, 0)),
                  pl.BlockSpec((d, d), lambda i: (0, 0)),
                  pl.BlockSpec((e, d), lambda i: (0, 0))],
        out_specs=[pl.BlockSpec((tm, d), lambda i: (i, 0)),
                   pl.BlockSpec((TOKEN_ROWS * tm, LANES), lambda i: (i, 0)),
                   pl.BlockSpec((e, tm), lambda i: (0, i))],
        out_shape=[jax.ShapeDtypeStruct((t, d), F32),
                   jax.ShapeDtypeStruct((TOKEN_ROWS * t, LANES), BF16),
                   jax.ShapeDtypeStruct((e, t), F32)],
        scratch_shapes=[_words_scratch(tm)],
        compiler_params=_params(1), name="outproj",
    )(merged, x2, mod3, norm_w, w_out_b, router_wt)


def _first_max(x, iota, sentinel):
    m = jnp.max(x, axis=0, keepdims=True)
    idx = jnp.min(jnp.where(x == m, iota, sentinel), axis=0, keepdims=True)
    return m, idx, iota == idx


def _route_kernel(lg_ref, bias_ref, ids_ref, w_ref, rank_ref, cnt_ref, carry_ref):
    i = pl.program_id(0)
    e, tk = lg_ref.shape

    @pl.when(i == 0)
    def _():
        carry_ref[...] = jnp.zeros_like(carry_ref)

    s = _sigmoid(lg_ref[...])
    biased = s + bias_ref[...]
    sub = lax.broadcasted_iota(I32, (GROUP_SIZE, tk), 0)
    group_rows = []
    for g in range(N_GROUPS):
        xg = biased[g * GROUP_SIZE:(g + 1) * GROUP_SIZE, :]
        m1, _, pick = _first_max(xg, sub, GROUP_SIZE)
        m2 = jnp.max(jnp.where(pick, NEG_INF, xg), axis=0, keepdims=True)
        group_rows.append(m1 + m2)
    gs = jnp.concatenate(group_rows, axis=0)
    gsub = lax.broadcasted_iota(I32, (N_GROUPS, tk), 0)
    sel = jnp.zeros((N_GROUPS, tk), F32)
    for _ in range(TOPK_GROUPS):
        _, _, pick = _first_max(gs, gsub, N_GROUPS)
        sel = jnp.where(pick, 1.0, sel)
        gs = jnp.where(pick, NEG_INF, gs)
    masked_rows = []
    for g in range(N_GROUPS):
        xg = biased[g * GROUP_SIZE:(g + 1) * GROUP_SIZE, :]
        masked_rows.append(jnp.where(sel[g:g + 1, :] > 0.5, xg, NEG_INF))
    masked = jnp.concatenate(masked_rows, axis=0)

    eio = lax.broadcasted_iota(I32, (e, tk), 0)
    chosen = jnp.zeros((e, tk), F32)
    ids, top_s = [], []
    for _ in range(TOP_K):
        _, idx, pick = _first_max(masked, eio, e)
        ids.append(idx)
        top_s.append(jnp.sum(jnp.where(pick, s, 0.0), axis=0, keepdims=True))
        chosen = jnp.where(pick, 1.0, chosen)
        masked = jnp.where(pick, NEG_INF, masked)
    total = top_s[0]
    for ts in top_s[1:]:
        total = total + ts

    before = (lax.broadcasted_iota(I32, (tk, tk), 0) < lax.broadcasted_iota(I32, (tk, tk), 1))
    upper = jnp.where(before, 1.0, 0.0).astype(BF16)
    rank = jnp.dot(chosen.astype(BF16), upper, preferred_element_type=F32) + carry_ref[:, 0:1]
    for k in range(TOP_K):
        ids_ref[k:k + 1, :] = ids[k]
        w_ref[k:k + 1, :] = top_s[k] / total * ROUTED_SCALE
        rk = jnp.sum(jnp.where(eio == ids[k], rank, 0.0), axis=0, keepdims=True)
        rank_ref[k:k + 1, :] = rk.astype(I32)
    carry_ref[...] = carry_ref[...] + jnp.sum(chosen, axis=1, keepdims=True)
    cnt_ref[...] = carry_ref[...].astype(I32)


def _route(logits_t, bias_col):
    e, t = logits_t.shape
    tk = min(512, t)
    row8 = lambda dt: jax.ShapeDtypeStruct((TOP_K, t), dt)
    blk8 = pl.BlockSpec((TOP_K, tk), lambda i: (0, i))
    return pl.pallas_call(
        _route_kernel, grid=(t // tk,),
        in_specs=[pl.BlockSpec((e, tk), lambda i: (0, i)),
                  pl.BlockSpec((e, 1), lambda i: (0, 0))],
        out_specs=[blk8, blk8, blk8, pl.BlockSpec((e, 128), lambda i: (0, 0))],
        out_shape=[row8(I32), row8(F32), row8(I32), jax.ShapeDtypeStruct((e, 128), I32)],
        scratch_shapes=[pltpu.VMEM((e, 128), F32)],
        compiler_params=_params(1), name="route",
    )(logits_t, bias_col)


def _dispatch_kernel(pad_ref, cnt_ref, hp_ref, slot_ref, xs_ref, zero_ref, sem, zsem, *, bm):
    i = pl.program_id(0)
    td = hp_ref.shape[0] // TOKEN_ROWS
    n_blocks = xs_ref.shape[0] // (bm * TOKEN_ROWS)

    @pl.when(i == 0)
    def _():
        zero_ref[...] = jnp.zeros_like(zero_ref)
        last_e = N_EXPERTS - 1
        used_blocks = pad_ref[last_e] // bm + (cnt_ref[last_e] + bm - 1) // bm

        def zero_block(j):
            n = bm * TOKEN_ROWS
            return pltpu.make_async_copy(zero_ref, xs_ref.at[pl.ds(pl.multiple_of(j * n, n), n)], zsem)

        def block_start(j, c):
            zero_block(j).start()
            return c

        def block_wait(j, c):
            zero_block(j).wait()
            return c

        lax.fori_loop(used_blocks, n_blocks, block_start, 0)
        lax.fori_loop(used_blocks, n_blocks, block_wait, 0)

        def zero_row(row):
            return pltpu.make_async_copy(zero_ref.at[_token(0)], xs_ref.at[_token(row)], zsem)

        def per_expert(e, c):
            first = pad_ref[e] + cnt_ref[e]
            last = pad_ref[e] + (cnt_ref[e] + bm - 1) // bm * bm

            def start_body(row, cc):
                zero_row(row).start()
                return cc

            def wait_body(row, cc):
                zero_row(row).wait()
                return cc

            lax.fori_loop(first, last, start_body, 0)
            lax.fori_loop(first, last, wait_body, 0)
            return c

        lax.fori_loop(0, N_EXPERTS, per_expert, 0)

    def body(t, c):
        for k in range(TOP_K):
            slot = slot_ref[t * TOP_K + k]
            pltpu.make_async_copy(hp_ref.at[_token(t)], xs_ref.at[_token(slot)], sem).start(priority=k % 2)
        return c

    lax.fori_loop(0, td, body, 0)
    for k in range(TOP_K):
        pltpu.make_async_copy(hp_ref, xs_ref.at[pl.ds(0, td * TOKEN_ROWS)], sem).wait()


def _dispatch(pad_start, counts, hp, slots, n_rows, bm):
    t = hp.shape[0] // TOKEN_ROWS
    td = min(1024, t)
    grid_spec = pltpu.PrefetchScalarGridSpec(
        num_scalar_prefetch=2, grid=(t // td,),
        in_specs=[pl.BlockSpec((td * TOKEN_ROWS, LANES), lambda i, *_: (i, 0)),
                  pl.BlockSpec((td * TOP_K,), lambda i, *_: (i,), memory_space=pltpu.SMEM)],
        out_specs=pl.BlockSpec(memory_space=pl.ANY),
        scratch_shapes=[pltpu.VMEM((bm * TOKEN_ROWS, LANES), BF16),
                        pltpu.SemaphoreType.DMA(()), pltpu.SemaphoreType.DMA(())])
    return pl.pallas_call(
        functools.partial(_dispatch_kernel, bm=bm), grid_spec=grid_spec,
        out_shape=jax.ShapeDtypeStruct((n_rows * TOKEN_ROWS, LANES), BF16),
        compiler_params=_params(1), name="dispatch",
    )(pad_start, counts, hp, slots)


def _expert_kernel(be_ref, first_ref, slot_ref, nxt_ref, nu_ref, x_ref, wg_hbm, wu_hbm, wd_hbm, o_ref,
                   words_ref, wg32, wu32, wd32, wg16, wu16, wd16, wsem):
    i = pl.program_id(0)

    def weight_copies(e, s):
        return (pltpu.make_async_copy(wg_hbm.at[e], wg32.at[s], wsem.at[s, 0]),
                pltpu.make_async_copy(wu_hbm.at[e], wu32.at[s], wsem.at[s, 1]),
                pltpu.make_async_copy(wd_hbm.at[e], wd32.at[s], wsem.at[s, 2]))

    @pl.when(i == 0)
    def _():
        for cp in weight_copies(be_ref[0], 0):
            cp.start()

    @pl.when(jnp.logical_and(first_ref[i] == 1, i < nu_ref[0]))
    def _():
        s = slot_ref[i]
        for cp in weight_copies(be_ref[i], s):
            cp.wait()

        @pl.when(nxt_ref[i] >= 0)
        def _():
            for cp in weight_copies(nxt_ref[i], 1 - s):
                cp.start()

        wg16[...] = wg32[s].astype(BF16)
        wu16[...] = wu32[s].astype(BF16)
        wd16[...] = wd32[s].astype(BF16)

    @pl.when(i < nu_ref[0])
    def _():
        x = _load_tokens(x_ref[...], words_ref).astype(BF16)
        g = jnp.dot(x, wg16[...], preferred_element_type=F32)
        u = jnp.dot(x, wu16[...], preferred_element_type=F32)
        mid = (_silu(g) * u).astype(BF16)
        _store_tokens(o_ref, jnp.dot(mid, wd16[...], preferred_element_type=F32), words_ref)

    @pl.when(i >= nu_ref[0])
    def _():
        o_ref[...] = jnp.zeros_like(o_ref)


def _experts(blk_e, blk_first, blk_slot, blk_next, n_used, xs, w_gate, w_up, w_down, n_blocks, bm):
    _, d, de = w_gate.shape
    x_map = lambda i, be, fi, sl, nx, nu: (jnp.minimum(i, nu[0] - 1), 0)
    hbm = pl.BlockSpec(memory_space=pl.ANY)
    grid_spec = pltpu.PrefetchScalarGridSpec(
        num_scalar_prefetch=5, grid=(n_blocks,),
        in_specs=[pl.BlockSpec((bm * TOKEN_ROWS, LANES), x_map), hbm, hbm, hbm],
        out_specs=pl.BlockSpec((bm * TOKEN_ROWS, LANES), lambda i, *_: (i, 0)),
        scratch_shapes=[_words_scratch(bm),
                        pltpu.VMEM((2, d, de), F32), pltpu.VMEM((2, d, de), F32), pltpu.VMEM((2, de, d), F32),
                        pltpu.VMEM((d, de), BF16), pltpu.VMEM((d, de), BF16), pltpu.VMEM((de, d), BF16),
                        pltpu.SemaphoreType.DMA((2, 3))])
    return pl.pallas_call(
        _expert_kernel, grid_spec=grid_spec,
        out_shape=jax.ShapeDtypeStruct((n_blocks * bm * TOKEN_ROWS, LANES), BF16),
        compiler_params=_params(1), name="experts",
    )(blk_e, blk_first, blk_slot, blk_next, n_used, xs, w_gate, w_up, w_down)


def _combine_kernel(x1_ref, hp_ref, w_ref, mod_ref, fw_ref, wsg_ref, wsu_ref, wsd_ref,
                    cur_ref, nxt_ref, ys_ref, o_ref, ybuf, words_ref, sem, *, n_tiles):
    i = pl.program_id(0)
    tc = x1_ref.shape[0]

    def issue(slot_ref, buf):
        def body(t, c):
            for k in range(TOP_K):
                slot = slot_ref[t * TOP_K + k]
                pltpu.make_async_copy(ys_ref.at[_token(slot)], ybuf.at[buf, k, _token(t)],
                                      sem.at[buf]).start(priority=k % 2)
            return c
        lax.fori_loop(0, tc, body, 0)

    @pl.when(i == 0)
    def _():
        issue(cur_ref, 0)

    @pl.when(i + 1 < n_tiles)
    def _():
        issue(nxt_ref, (i + 1) % 2)

    buf = i % 2
    for k in range(TOP_K):
        pltpu.make_async_copy(ys_ref.at[pl.ds(0, tc * TOKEN_ROWS)], ybuf.at[buf, k], sem.at[buf]).wait()

    h = _load_tokens(hp_ref[...], words_ref).astype(BF16)
    sg = jnp.dot(h, wsg_ref[...], preferred_element_type=F32)
    su = jnp.dot(h, wsu_ref[...], preferred_element_type=F32)
    shared = jnp.dot((_silu(sg) * su).astype(BF16), wsd_ref[...], preferred_element_type=F32)
    w = w_ref[...]
    routed = w[:, 0:1] * _load_tokens(ybuf[buf, 0], words_ref)
    for k in range(1, TOP_K):
        routed = routed + w[:, k:k + 1] * _load_tokens(ybuf[buf, k], words_ref)
    x = x1_ref[...] + mod_ref[0][5:6, :] * (routed + shared)
    o_ref[...] = x * lax.rsqrt(jnp.mean(x * x, axis=-1, keepdims=True) + EPS) * fw_ref[...]


def _combine(x1, hp, w_tok, mod3, final_w, wsg_b, wsu_b, wsd_b, slots, ys, seq):
    t, d = x1.shape
    ds = wsg_b.shape[1]
    tc = min(256, seq)
    n_tiles = t // tc
    tiles_per_seq = seq // tc
    cur = pl.BlockSpec((tc * TOP_K,), lambda i: (i,), memory_space=pltpu.SMEM)
    nxt = pl.BlockSpec((tc * TOP_K,), lambda i: (jnp.minimum(i + 1, n_tiles - 1),), memory_space=pltpu.SMEM)
    full = lambda shape: pl.BlockSpec(shape, lambda i, *_: (0,) * len(shape))
    grid_spec = pltpu.PrefetchScalarGridSpec(
        num_scalar_prefetch=0, grid=(n_tiles,),
        in_specs=[pl.BlockSpec((tc, d), lambda i, *_: (i, 0)),
                  pl.BlockSpec((tc * TOKEN_ROWS, LANES), lambda i, *_: (i, 0)),
                  pl.BlockSpec((tc, TOP_K), lambda i, *_: (i, 0)),
                  pl.BlockSpec((1, 6, d), lambda i, *_: (i // tiles_per_seq, 0, 0)),
                  full((1, d)), full((d, ds)), full((d, ds)), full((ds, d)),
                  cur, nxt,
                  pl.BlockSpec(memory_space=pl.ANY)],
        out_specs=pl.BlockSpec((tc, d), lambda i, *_: (i, 0)),
        scratch_shapes=[pltpu.VMEM((2, TOP_K, tc * TOKEN_ROWS, LANES), BF16), _words_scratch(tc),
                        pltpu.SemaphoreType.DMA((2,))])
    return pl.pallas_call(
        functools.partial(_combine_kernel, n_tiles=n_tiles), grid_spec=grid_spec,
        out_shape=jax.ShapeDtypeStruct((t, d), F32),
        compiler_params=_params(1), name="combine",
    )(x1, hp, w_tok, mod3, final_w, wsg_b, wsu_b, wsd_b, slots, slots, ys)


def _rope_tables(seq, d):
    half = d // 2
    inv = ROPE_BASE ** (-jnp.arange(half, dtype=F32) / half)
    ang = jnp.arange(seq, dtype=F32)[:, None] * inv[None, :]
    cos, sin = jnp.cos(ang), jnp.sin(ang)
    return jnp.concatenate([cos, cos], axis=1), jnp.concatenate([-sin, sin], axis=1)


def kernel(x, c, w_ada, b_ada, norm1_w, w_in, ret_decay_fwd, ret_decay_bwd, ret_norm_w, w_ret_o, conv_w,
           w_conv_o, w_out, norm2_w, router_w, router_bias, w_gate, w_up, w_down, ws_gate, ws_up, ws_down,
           final_norm_w):
    batch, seq, d = x.shape
    depth = w_ada.shape[0]
    t = batch * seq
    bm = EXPERT_BLOCK
    n_blocks = (t * TOP_K + N_EXPERTS * (bm - 1)) // bm
    n_rows = n_blocks * bm
    cos, sin = _rope_tables(seq, HEAD_DIM)
    c_pad = jnp.pad(c, ((0, BF16_SUBLANES - batch % BF16_SUBLANES), (0, 0)))
    x2 = x.reshape(t, d)

    assert depth == 1, "the final norm is fused into the last stage of a single layer"
    for l in range(depth):
        mod = _ada(c_pad, w_ada[l], b_ada[l][None, :])[:batch]
        mod3 = mod.reshape(batch, 6, d)
        proj = _inproj(x2, mod3, norm1_w[l][None, :], w_in[l].astype(BF16), seq)
        lane_bcast = lambda v: jnp.broadcast_to(v[:, None, None], (RET_HEADS, 1, HEAD_DIM))
        og = _retention(proj, cos, sin, lane_bcast(ret_decay_fwd[l]), lane_bcast(ret_decay_bwd[l]),
                        ret_norm_w[l].reshape(RET_HEADS, 1, HEAD_DIM), batch, seq)
        merged = _mix(og, proj, conv_w[l], w_ret_o[l].astype(BF16), w_conv_o[l].astype(BF16), seq)
        x1, hp, logits_t = _outproj(merged, x2, mod3, norm2_w[l][None, :], w_out[l].astype(BF16),
                                    router_w[l].T, seq)
        ids_t, w_t, rank_t, cnt = _route(logits_t, router_bias[l][:, None])

        counts = cnt[:, 0]
        nblk = (counts + bm - 1) // bm
        blk_end = jnp.cumsum(nblk)
        pad_start = ((blk_end - nblk) * bm).astype(I32)
        n_used = blk_end[-1:].astype(I32)
        blk_ids = jnp.arange(n_blocks, dtype=I32)
        blk_e = jnp.minimum(jnp.sum((blk_ids[:, None] >= blk_end[None, :]).astype(I32), axis=1),
                            N_EXPERTS - 1)

        blk_first = jnp.concatenate([jnp.ones((1,), I32), (blk_e[1:] != blk_e[:-1]).astype(I32)])
        blk_slot = (jnp.cumsum(blk_first) - 1) % 2
        after = blk_end[blk_e]
        blk_next = jnp.where(after < n_used[0], blk_e[jnp.minimum(after, n_blocks - 1)], -1).astype(I32)

        onehot = ids_t[:, :, None] == jnp.arange(N_EXPERTS, dtype=I32)
        slots_t = rank_t + jnp.sum(jnp.where(onehot, pad_start, 0), axis=-1)
        slots = slots_t.T.reshape(t * TOP_K)
        xs = _dispatch(pad_start, counts, hp, slots, n_rows, bm)
        ys = _experts(blk_e, blk_first, blk_slot.astype(I32), blk_next, n_used, xs,
                      w_gate[l], w_up[l], w_down[l], n_blocks, bm)
        x2 = _combine(x1, hp, w_t.T, mod3, final_norm_w[None, :], ws_gate[l].astype(BF16),
                      ws_up[l].astype(BF16), ws_down[l].astype(BF16), slots, ys, seq)
    return x2.reshape(batch, seq, d)
```

```python
import functools

import jax
import jax.numpy as jnp
from jax import lax
from jax.experimental import pallas as pl
from jax.experimental.pallas import tpu as pltpu

F32 = jnp.float32
BF16 = jnp.bfloat16
I32 = jnp.int32

EPS = 1e-6
RET_HEADS = 8
HEAD_DIM = 128
ROPE_BASE = 10000.0
N_EXPERTS = 64
TOP_K = 8
N_GROUPS = 8
TOPK_GROUPS = 4
GROUP_SIZE = N_EXPERTS // N_GROUPS
ROUTED_SCALE = 2.5

V7X_VMEM_BYTES = 64 * 1024 * 1024
VMEM_LIMIT = V7X_VMEM_BYTES - 8 * 1024 * 1024
BF16_SUBLANES = 16

RET_CHUNK = 512
EXPERT_BLOCK = 512
NEG_INF = float("-inf")


def _params(n_axes):
    return pltpu.CompilerParams(dimension_semantics=("arbitrary",) * n_axes,
                                vmem_limit_bytes=VMEM_LIMIT)


def _sigmoid(x):
    return 1.0 / (1.0 + jnp.exp(-x))


def _silu(x):
    return x * _sigmoid(x)


LANES = 128
TOKEN_WORD_ROWS = 8
TOKEN_ROWS = 2 * TOKEN_WORD_ROWS


def _store_tokens(ref, x, words_ref):
    n, m = x.shape[0], x.shape[1] // 2
    assert m == TOKEN_WORD_ROWS * LANES
    packed = pltpu.pack_elementwise([x[:, :m], x[:, m:]], packed_dtype=BF16)
    for s in range(TOKEN_WORD_ROWS):
        words_ref[pl.ds(s, n, stride=TOKEN_WORD_ROWS), :] = packed[:, s * LANES:(s + 1) * LANES]
    ref[...] = pltpu.bitcast(words_ref[...], BF16)


def _load_tokens(tiles, words_ref):
    n = tiles.shape[0] // TOKEN_ROWS
    words_ref[...] = pltpu.bitcast(tiles, jnp.uint32)
    p = jnp.concatenate([words_ref[pl.ds(s, n, stride=TOKEN_WORD_ROWS), :]
                         for s in range(TOKEN_WORD_ROWS)], axis=1)
    a = pltpu.unpack_elementwise(p, index=0, packed_dtype=BF16, unpacked_dtype=F32)
    b = pltpu.unpack_elementwise(p, index=1, packed_dtype=BF16, unpacked_dtype=F32)
    return jnp.concatenate([a, b], axis=1)


def _token(row):
    return pl.ds(pl.multiple_of(row * TOKEN_ROWS, TOKEN_ROWS), TOKEN_ROWS)


def _words_scratch(n_tokens):
    return pltpu.VMEM((n_tokens * TOKEN_WORD_ROWS, LANES), jnp.uint32)


def _ada_kernel(c_ref, w_ref, b_ref, o_ref):
    s = _silu(c_ref[...]).astype(BF16)
    o_ref[...] = jnp.dot(s, w_ref[...].astype(BF16), preferred_element_type=F32) + b_ref[...]


def _ada(c_pad, w_ada, b_ada):
    m, d = c_pad.shape
    n = w_ada.shape[1]
    tn = 1024
    return pl.pallas_call(
        _ada_kernel, grid=(n // tn,),
        in_specs=[pl.BlockSpec((m, d), lambda j: (0, 0)),
                  pl.BlockSpec((d, tn), lambda j: (0, j)),
                  pl.BlockSpec((1, tn), lambda j: (0, j))],
        out_specs=pl.BlockSpec((m, tn), lambda j: (0, j)),
        out_shape=jax.ShapeDtypeStruct((m, n), F32),
        compiler_params=_params(1), name="ada")(c_pad, w_ada, b_ada)


def _inproj_kernel(x_ref, mod_ref, nw_ref, w_ref, o_ref, h_ref):
    @pl.when(pl.program_id(1) == 0)
    def _():
        x = x_ref[...]
        y = x * lax.rsqrt(jnp.mean(x * x, axis=-1, keepdims=True) + EPS) * nw_ref[...]
        m = mod_ref[0]
        h_ref[...] = (y * (1.0 + m[1:2, :]) + m[0:1, :]).astype(BF16)

    o_ref[...] = jnp.dot(h_ref[...], w_ref[...], preferred_element_type=F32).astype(o_ref.dtype)


def _inproj(x2, mod3, norm_w, w_in_b, seq):
    t, d = x2.shape
    n = w_in_b.shape[1]
    tm, tn = min(1024, seq), 1024
    tiles_per_seq = seq // tm
    return pl.pallas_call(
        _inproj_kernel, grid=(t // tm, n // tn),
        in_specs=[pl.BlockSpec((tm, d), lambda i, j: (i, 0)),
                  pl.BlockSpec((1, 6, d), lambda i, j: (i // tiles_per_seq, 0, 0)),
                  pl.BlockSpec((1, d), lambda i, j: (0, 0)),
                  pl.BlockSpec((d, tn), lambda i, j: (0, j))],
        out_specs=pl.BlockSpec((tm, tn), lambda i, j: (i, j)),
        out_shape=jax.ShapeDtypeStruct((t, n), BF16),
        scratch_shapes=[pltpu.VMEM((tm, d), BF16)],
        compiler_params=_params(2), name="inproj")(x2, mod3, norm_w, w_in_b)


def _log_sigmoid(x):
    return jnp.minimum(x, 0.0) - jnp.log1p(jnp.exp(-jnp.abs(x)))


def _ret_kernel(q_ref, k_ref, v_ref, g_ref, cos_ref, sin_ref, df_ref, db_ref, nw_ref, o_ref,
                mask_ref, qf_ref, qb_ref, kf_ref, kb_ref, dec_ref, qr_ref, kr_ref, acc_ref, *, chunk):
    seq, d = q_ref.shape
    n = seq // chunk
    c = chunk

    @pl.when(pl.program_id(1) == 0)
    def _():
        lgf = _log_sigmoid(df_ref[0])
        lgb = _log_sigmoid(db_ref[0])
        ii = lax.broadcasted_iota(I32, (c, c), 0)
        jj = lax.broadcasted_iota(I32, (c, c), 1)
        diff = (ii - jj).astype(F32)
        lgf_c = jnp.concatenate([lgf] * (c // d), axis=1)
        lgb_c = jnp.concatenate([lgb] * (c // d), axis=1)
        mask_ref[...] = jnp.where(diff >= 0.0,
                                  jnp.exp(lgf_c * jnp.maximum(diff, 0.0)),
                                  jnp.exp(lgb_c * jnp.maximum(-diff, 0.0)))
        pos = lax.broadcasted_iota(I32, (c, d), 0).astype(F32)
        qf_ref[...] = jnp.exp(lgf * (pos + 1.0))
        qb_ref[...] = jnp.exp(lgb * (c - pos))
        kf_ref[...] = jnp.exp(lgf * (c - 1.0 - pos))
        kb_ref[...] = jnp.exp(lgb * pos)
        dec_ref[0:1, :] = jnp.exp(lgf * c)
        dec_ref[1:2, :] = jnp.exp(lgb * c)

    scale = d ** -0.5
    nt = (((1,), (1,)), ((), ()))
    sls = [pl.ds(i * c, c) for i in range(n)]

    for sl in sls:
        cs = cos_ref[sl, :]
        sn = sin_ref[sl, :]
        q = q_ref[sl, :].astype(F32)
        k = k_ref[sl, :].astype(F32)
        qr_ref[sl, :] = q * cs + pltpu.roll(q, d // 2, 1) * sn
        kr_ref[sl, :] = (k * cs + pltpu.roll(k, d // 2, 1) * sn) * scale

    for sl in sls:
        s = lax.dot_general(qr_ref[sl, :].astype(BF16), kr_ref[sl, :].astype(BF16), nt,
                            preferred_element_type=F32)
        p = (s * mask_ref[...]).astype(BF16)
        acc_ref[sl, :] = jnp.dot(p, v_ref[sl, :], preferred_element_type=F32)

    def kv_state(sl, kw_ref):
        kw = (kr_ref[sl, :] * kw_ref[...]).T.astype(BF16)
        return jnp.dot(kw, v_ref[sl, :], preferred_element_type=F32)

    def scan(order, qw_ref, kw_ref, dec):
        st = jnp.zeros((d, d), F32)
        for idx, ci in enumerate(order):
            sl = sls[ci]
            if idx > 0:
                qw = (qr_ref[sl, :] * qw_ref[...]).astype(BF16)
                acc_ref[sl, :] += jnp.dot(qw, st.astype(BF16), preferred_element_type=F32)
            if idx < n - 1:
                st = dec * st + kv_state(sl, kw_ref)

    scan(list(range(n)), qf_ref, kf_ref, dec_ref[0:1, :])
    scan(list(range(n - 1, -1, -1)), qb_ref, kb_ref, dec_ref[1:2, :])

    nw = nw_ref[0]
    for sl in sls:
        o = acc_ref[sl, :]
        on = o * lax.rsqrt(jnp.mean(o * o, axis=-1, keepdims=True) + EPS) * nw
        g = g_ref[sl, :].astype(F32)
        o_ref[sl, :] = (_silu(g) * on).astype(o_ref.dtype)


def _retention(proj, cos, sin, dec_f, dec_b, ret_norm_w, batch, seq):
    h, d = RET_HEADS, HEAD_DIM
    c = min(RET_CHUNK, seq)
    col = lambda off: pl.BlockSpec((seq, d), lambda hh, b: (b, off + hh))
    per_head = pl.BlockSpec((1, 1, d), lambda hh, b: (hh, 0, 0))
    table = pl.BlockSpec((seq, d), lambda hh, b: (0, 0))
    vm = lambda shape, dt=F32: pltpu.VMEM(shape, dt)
    return pl.pallas_call(
        functools.partial(_ret_kernel, chunk=c), grid=(h, batch),
        in_specs=[col(0), col(h), col(2 * h), col(3 * h), table, table, per_head, per_head, per_head],
        out_specs=pl.BlockSpec((seq, d), lambda hh, b: (b, hh)),
        out_shape=jax.ShapeDtypeStruct((batch * seq, h * d), BF16),
        scratch_shapes=[vm((c, c)), vm((c, d)), vm((c, d)), vm((c, d)), vm((c, d)), vm((8, d)),
                        vm((seq, d)), vm((seq, d)), vm((seq, d))],
        compiler_params=_params(2), name="retention",
    )(proj, proj, proj, proj, cos, sin, dec_f, dec_b, ret_norm_w)


def _mix_kernel(og_ref, cb_ref, cc_ref, cu_ref, ccp_ref, cup_ref, ccn_ref, cun_ref,
                gr0_ref, gr1_ref, gc0_ref, gc1_ref, cw_ref, wr_ref, wc_ref, o_ref, *, tiles_per_seq):
    i = pl.program_id(0)
    tm, dr = cc_ref.shape
    u = cc_ref[...].astype(F32) * cu_ref[...].astype(F32)
    pos = i % tiles_per_seq
    last = BF16_SUBLANES - 1
    u_before = ccp_ref[last:last + 1, :].astype(F32) * cup_ref[last:last + 1, :].astype(F32)
    u_before = jnp.where(pos == 0, 0.0, u_before)
    u_after = ccn_ref[0:1, :].astype(F32) * cun_ref[0:1, :].astype(F32)
    u_after = jnp.where(pos == tiles_per_seq - 1, 0.0, u_after)
    row = lax.broadcasted_iota(I32, u.shape, 0)
    u_prev = jnp.where(row == 0, u_before, pltpu.roll(u, 1, 0))
    u_next = jnp.where(row == tm - 1, u_after, pltpu.roll(u, tm - 1, 0))
    cw = cw_ref[...]
    y = cw[0:1, :] * u_prev + cw[1:2, :] * u + cw[2:3, :] * u_next
    z = (cb_ref[...].astype(F32) * y).astype(BF16)

    yr = jnp.dot(og_ref[...], wr_ref[...], preferred_element_type=F32)
    yc = jnp.dot(z, wc_ref[...], preferred_element_type=F32)
    for lo, gr_ref, gc_ref in ((0, gr0_ref, gc0_ref), (dr, gr1_ref, gc1_ref)):
        merged = (_sigmoid(gr_ref[...].astype(F32)) * yr[:, lo:lo + dr]
                  + _sigmoid(gc_ref[...].astype(F32)) * yc[:, lo:lo + dr])
        o_ref[:, lo:lo + dr] = merged.astype(o_ref.dtype)


def _mix(og, proj, conv_w, w_ret_o_b, w_conv_o_b, seq):
    t, dr = og.shape
    d = w_ret_o_b.shape[1]
    assert d == 2 * dr
    tm = min(512, seq)
    tiles_per_seq = seq // tm
    hb = tm // BF16_SUBLANES
    n_hblk = t // BF16_SUBLANES
    wide = lambda off: pl.BlockSpec((tm, dr), lambda i: (i, off))
    before = lambda off: pl.BlockSpec((BF16_SUBLANES, dr), lambda i: (jnp.maximum(i * hb - 1, 0), off))
    after = lambda off: pl.BlockSpec((BF16_SUBLANES, dr), lambda i: (jnp.minimum((i + 1) * hb, n_hblk - 1), off))
    whole = lambda shape: pl.BlockSpec(shape, lambda i: (0, 0))
    return pl.pallas_call(
        functools.partial(_mix_kernel, tiles_per_seq=tiles_per_seq), grid=(t // tm,),
        in_specs=[wide(0), wide(4), wide(5), wide(6), before(5), before(6), after(5), after(6),
                  wide(7), wide(8), wide(9), wide(10),
                  whole((3, dr)), whole((dr, d)), whole((dr, d))],
        out_specs=pl.BlockSpec((tm, d), lambda i: (i, 0)),
        out_shape=jax.ShapeDtypeStruct((t, d), BF16),
        compiler_params=_params(1), name="mix",
    )(og, proj, proj, proj, proj, proj, proj, proj, proj, proj, proj, proj, conv_w, w_ret_o_b, w_conv_o_b)


def _outproj_kernel(m_ref, x_ref, mod_ref, nw_ref, w_ref, rwt_ref, x1_ref, hp_ref, lg_ref, words_ref):
    y = jnp.dot(m_ref[...], w_ref[...], preferred_element_type=F32)
    m = mod_ref[0]
    x1 = x_ref[...] + m[2:3, :] * y
    x1_ref[...] = x1
    hn = x1 * lax.rsqrt(jnp.mean(x1 * x1, axis=-1, keepdims=True) + EPS) * nw_ref[...]
    h = hn * (1.0 + m[4:5, :]) + m[3:4, :]
    _store_tokens(hp_ref, h, words_ref)
    h_hi = h.astype(BF16)
    h_lo = (h - h_hi.astype(F32)).astype(BF16)
    rw = rwt_ref[...]
    r_hi = rw.astype(BF16)
    r_lo = (rw - r_hi.astype(F32)).astype(BF16)
    nt = (((1,), (1,)), ((), ()))
    dot = functools.partial(lax.dot_general, dimension_numbers=nt, preferred_element_type=F32)
    lg_ref[...] = dot(r_hi, h_hi) + dot(r_hi, h_lo) + dot(r_lo, h_hi)


def _outproj(merged, x2, mod3, norm_w, w_out_b, router_wt, seq):
    t, d = x2.shape
    e = router_wt.shape[0]
    tm = min(512, seq)
    tiles_per_seq = seq // tm
    return pl.pallas_call(
        _outproj_kernel, grid=(t // tm,),
        in_specs=[pl.BlockSpec((tm, d), lambda i: (i, 0)),
                  pl.BlockSpec((tm, d), lambda i: (i, 0)),
                  pl.BlockSpec((1, 6, d), lambda i: (i // tiles_per_seq, 0, 0)),
                  pl.BlockSpec((1, d), lambda i: (0, 0)),
                  pl.BlockSpec((d, d), lambda i: (0, 0)),
                  pl.BlockSpec((e, d), lambda i: (0, 0))],
        out_specs=[pl.BlockSpec((tm, d), lambda i: (i, 0)),
                   pl.BlockSpec((TOKEN_ROWS * tm, LANES), lambda i: (i, 0)),
                   pl.BlockSpec((e, tm), lambda i: (0, i))],
        out_shape=[jax.ShapeDtypeStruct((t, d), F32),
                   jax.ShapeDtypeStruct((TOKEN_ROWS * t, LANES), BF16),
                   jax.ShapeDtypeStruct((e, t), F32)],
        scratch_shapes=[_words_scratch(tm)],
        compiler_params=_params(1), name="outproj",
    )(merged, x2, mod3, norm_w, w_out_b, router_wt)


def _first_max(x, iota, sentinel):
    m = jnp.max(x, axis=0, keepdims=True)
    idx = jnp.min(jnp.where(x == m, iota, sentinel), axis=0, keepdims=True)
    return m, idx, iota == idx


def _route_kernel(lg_ref, bias_ref, ids_ref, w_ref, rank_ref, cnt_ref, carry_ref):
    i = pl.program_id(0)
    e, tk = lg_ref.shape

    @pl.when(i == 0)
    def _():
        carry_ref[...] = jnp.zeros_like(carry_ref)

    s = _sigmoid(lg_ref[...])
    biased = s + bias_ref[...]
    sub = lax.broadcasted_iota(I32, (GROUP_SIZE, tk), 0)
    group_rows = []
    for g in range(N_GROUPS):
        xg = biased[g * GROUP_SIZE:(g + 1) * GROUP_SIZE, :]
        m1, _, pick = _first_max(xg, sub, GROUP_SIZE)
        m2 = jnp.max(jnp.where(pick, NEG_INF, xg), axis=0, keepdims=True)
        group_rows.append(m1 + m2)
    gs = jnp.concatenate(group_rows, axis=0)
    gsub = lax.broadcasted_iota(I32, (N_GROUPS, tk), 0)
    sel = jnp.zeros((N_GROUPS, tk), F32)
    for _ in range(TOPK_GROUPS):
        _, _, pick = _first_max(gs, gsub, N_GROUPS)
        sel = jnp.where(pick, 1.0, sel)
        gs = jnp.where(pick, NEG_INF, gs)
    masked_rows = []
    for g in range(N_GROUPS):
        xg = biased[g * GROUP_SIZE:(g + 1) * GROUP_SIZE, :]
        masked_rows.append(jnp.where(sel[g:g + 1, :] > 0.5, xg, NEG_INF))
    masked = jnp.concatenate(masked_rows, axis=0)

    eio = lax.broadcasted_iota(I32, (e, tk), 0)
    chosen = jnp.zeros((e, tk), F32)
    ids, top_s = [], []
    for _ in range(TOP_K):
        _, idx, pick = _first_max(masked, eio, e)
        ids.append(idx)
        top_s.append(jnp.sum(jnp.where(pick, s, 0.0), axis=0, keepdims=True))
        chosen = jnp.where(pick, 1.0, chosen)
        masked = jnp.where(pick, NEG_INF, masked)
    total = top_s[0]
    for ts in top_s[1:]:
        total = total + ts

    before = (lax.broadcasted_iota(I32, (tk, tk), 0) < lax.broadcasted_iota(I32, (tk, tk), 1))
    upper = jnp.where(before, 1.0, 0.0).astype(BF16)
    rank = jnp.dot(chosen.astype(BF16), upper, preferred_element_type=F32) + carry_ref[:, 0:1]
    for k in range(TOP_K):
        ids_ref[k:k + 1, :] = ids[k]
        w_ref[k:k + 1, :] = top_s[k] / total * ROUTED_SCALE
        rk = jnp.sum(jnp.where(eio == ids[k], rank, 0.0), axis=0, keepdims=True)
        rank_ref[k:k + 1, :] = rk.astype(I32)
    carry_ref[...] = carry_ref[...] + jnp.sum(chosen, axis=1, keepdims=True)
    cnt_ref[...] = carry_ref[...].astype(I32)


def _route(logits_t, bias_col):
    e, t = logits_t.shape
    tk = min(512, t)
    row8 = lambda dt: jax.ShapeDtypeStruct((TOP_K, t), dt)
    blk8 = pl.BlockSpec((TOP_K, tk), lambda i: (0, i))
    return pl.pallas_call(
        _route_kernel, grid=(t // tk,),
        in_specs=[pl.BlockSpec((e, tk), lambda i: (0, i)),
                  pl.BlockSpec((e, 1), lambda i: (0, 0))],
        out_specs=[blk8, blk8, blk8, pl.BlockSpec((e, 128), lambda i: (0, 0))],
        out_shape=[row8(I32), row8(F32), row8(I32), jax.ShapeDtypeStruct((e, 128), I32)],
        scratch_shapes=[pltpu.VMEM((e, 128), F32)],
        compiler_params=_params(1), name="route",
    )(logits_t, bias_col)


def _dispatch_kernel(pad_ref, cnt_ref, hp_ref, slot_ref, wsg_ref, wsu_ref, wsd_ref, xs_ref, sh_ref,
                     zero_ref, words_ref, sem, zsem, *, bm):
    i = pl.program_id(0)
    td = hp_ref.shape[0] // TOKEN_ROWS
    n_rows = xs_ref.shape[0] // TOKEN_ROWS
    n_blocks = n_rows // bm
    block = bm * TOKEN_ROWS

    @pl.when(i == 0)
    def _():
        zero_ref[...] = jnp.zeros_like(zero_ref)
        last_e = N_EXPERTS - 1
        used_blocks = pad_ref[last_e] // bm + (cnt_ref[last_e] + bm - 1) // bm

        def tail_copy(j):
            return pltpu.make_async_copy(zero_ref, xs_ref.at[pl.ds(pl.multiple_of(j * block, block), block)], zsem)

        def tail(j, c, wait):
            cp = tail_copy(j)
            cp.wait() if wait else cp.start()
            return c

        def pad(e, c, wait):
            row = pad_ref[e] + cnt_ref[e]
            n = (cnt_ref[e] + bm - 1) // bm * bm - cnt_ref[e]
            bit = bm // 2
            while bit:
                take = (n & bit) != 0
                size = bit * TOKEN_ROWS
                cp = pltpu.make_async_copy(
                    zero_ref.at[pl.ds(0, size)],
                    xs_ref.at[pl.ds(pl.multiple_of(row * TOKEN_ROWS, TOKEN_ROWS), size)], zsem)

                @pl.when(take)
                def _():
                    cp.wait() if wait else cp.start()

                row = row + jnp.where(take, bit, 0)
                bit //= 2
            return c

        for wait in (False, True):
            lax.fori_loop(used_blocks, n_blocks, functools.partial(tail, wait=wait), 0)
            lax.fori_loop(0, N_EXPERTS, functools.partial(pad, wait=wait), 0)

    def body(t, c):
        for k in range(TOP_K):
            slot = slot_ref[t * TOP_K + k]
            pltpu.make_async_copy(hp_ref.at[_token(t)], xs_ref.at[_token(slot)], sem).start(priority=k % 2)
        return c

    lax.fori_loop(0, td, body, 0)

    h = _load_tokens(hp_ref[...], words_ref).astype(BF16)
    sg = jnp.dot(h, wsg_ref[...], preferred_element_type=F32)
    su = jnp.dot(h, wsu_ref[...], preferred_element_type=F32)
    shared = jnp.dot((_silu(sg) * su).astype(BF16), wsd_ref[...], preferred_element_type=F32)
    sh_ref[...] = shared.astype(sh_ref.dtype)

    for k in range(TOP_K):
        pltpu.make_async_copy(hp_ref, xs_ref.at[pl.ds(0, td * TOKEN_ROWS)], sem).wait()


def _dispatch(pad_start, counts, hp, slots, wsg_b, wsu_b, wsd_b, n_rows, bm):
    t = hp.shape[0] // TOKEN_ROWS
    d, ds = wsg_b.shape
    td = min(512, t)
    whole = lambda shape: pl.BlockSpec(shape, lambda i, *_: (0, 0))
    grid_spec = pltpu.PrefetchScalarGridSpec(
        num_scalar_prefetch=2, grid=(t // td,),
        in_specs=[pl.BlockSpec((td * TOKEN_ROWS, LANES), lambda i, *_: (i, 0)),
                  pl.BlockSpec((td * TOP_K,), lambda i, *_: (i,), memory_space=pltpu.SMEM),
                  whole((d, ds)), whole((d, ds)), whole((ds, d))],
        out_specs=[pl.BlockSpec(memory_space=pl.ANY),
                   pl.BlockSpec((td, d), lambda i, *_: (i, 0))],
        scratch_shapes=[pltpu.VMEM((bm * TOKEN_ROWS, LANES), BF16), _words_scratch(td),
                        pltpu.SemaphoreType.DMA(()), pltpu.SemaphoreType.DMA(())])
    return pl.pallas_call(
        functools.partial(_dispatch_kernel, bm=bm), grid_spec=grid_spec,
        out_shape=[jax.ShapeDtypeStruct((n_rows * TOKEN_ROWS, LANES), BF16),
                   jax.ShapeDtypeStruct((t, d), BF16)],
        compiler_params=_params(1), name="dispatch",
    )(pad_start, counts, hp, slots, wsg_b, wsu_b, wsd_b)


def _expert_kernel(be_ref, first_ref, slot_ref, nxt_ref, nu_ref, x_ref, wg_hbm, wu_hbm, wd_hbm, o_ref,
                   words_ref, wg32, wu32, wd32, wg16, wu16, wd16, wsem):
    i = pl.program_id(0)

    def weight_copies(e, s):
        return (pltpu.make_async_copy(wg_hbm.at[e], wg32.at[s], wsem.at[s, 0]),
                pltpu.make_async_copy(wu_hbm.at[e], wu32.at[s], wsem.at[s, 1]),
                pltpu.make_async_copy(wd_hbm.at[e], wd32.at[s], wsem.at[s, 2]))

    @pl.when(i == 0)
    def _():
        for cp in weight_copies(be_ref[0], 0):
            cp.start()

    @pl.when(jnp.logical_and(first_ref[i] == 1, i < nu_ref[0]))
    def _():
        s = slot_ref[i]
        for cp in weight_copies(be_ref[i], s):
            cp.wait()

        @pl.when(nxt_ref[i] >= 0)
        def _():
            for cp in weight_copies(nxt_ref[i], 1 - s):
                cp.start()

        wg16[...] = wg32[s].astype(BF16)
        wu16[...] = wu32[s].astype(BF16)
        wd16[...] = wd32[s].astype(BF16)

    @pl.when(i < nu_ref[0])
    def _():
        x = _load_tokens(x_ref[...], words_ref).astype(BF16)
        g = jnp.dot(x, wg16[...], preferred_element_type=F32)
        u = jnp.dot(x, wu16[...], preferred_element_type=F32)
        mid = (_silu(g) * u).astype(BF16)
        _store_tokens(o_ref, jnp.dot(mid, wd16[...], preferred_element_type=F32), words_ref)

    @pl.when(i >= nu_ref[0])
    def _():
        o_ref[...] = jnp.zeros_like(o_ref)


def _experts(blk_e, blk_first, blk_slot, blk_next, n_used, xs, w_gate, w_up, w_down, n_blocks, bm):
    _, d, de = w_gate.shape
    x_map = lambda i, be, fi, sl, nx, nu: (jnp.minimum(i, nu[0] - 1), 0)
    hbm = pl.BlockSpec(memory_space=pl.ANY)
    grid_spec = pltpu.PrefetchScalarGridSpec(
        num_scalar_prefetch=5, grid=(n_blocks,),
        in_specs=[pl.BlockSpec((bm * TOKEN_ROWS, LANES), x_map), hbm, hbm, hbm],
        out_specs=pl.BlockSpec((bm * TOKEN_ROWS, LANES), lambda i, *_: (i, 0)),
        scratch_shapes=[_words_scratch(bm),
                        pltpu.VMEM((2, d, de), F32), pltpu.VMEM((2, d, de), F32), pltpu.VMEM((2, de, d), F32),
                        pltpu.VMEM((d, de), BF16), pltpu.VMEM((d, de), BF16), pltpu.VMEM((de, d), BF16),
                        pltpu.SemaphoreType.DMA((2, 3))])
    return pl.pallas_call(
        _expert_kernel, grid_spec=grid_spec,
        out_shape=jax.ShapeDtypeStruct((n_blocks * bm * TOKEN_ROWS, LANES), BF16),
        compiler_params=_params(1), name="experts",
    )(blk_e, blk_first, blk_slot, blk_next, n_used, xs, w_gate, w_up, w_down)


def _combine_kernel(x1_ref, sh_ref, w_ref, mod_ref, fw_ref, cur_ref, nxt_ref, ys_ref, o_ref,
                    ybuf, acc_ref, sem, *, n_tiles):
    i = pl.program_id(0)
    tc = x1_ref.shape[0]

    def issue(slot_ref, buf):
        def body(t, c):
            for k in range(TOP_K):
                slot = slot_ref[t * TOP_K + k]
                pltpu.make_async_copy(ys_ref.at[_token(slot)], ybuf.at[buf, k, _token(t)],
                                      sem.at[buf]).start(priority=k % 2)
            return c
        lax.fori_loop(0, tc, body, 0)

    @pl.when(i == 0)
    def _():
        issue(cur_ref, 0)

    @pl.when(i + 1 < n_tiles)
    def _():
        issue(nxt_ref, (i + 1) % 2)

    buf = i % 2
    for k in range(TOP_K):
        pltpu.make_async_copy(ys_ref.at[pl.ds(0, tc * TOKEN_ROWS)], ybuf.at[buf, k], sem.at[buf]).wait()

    w = w_ref[...]
    acc = [None, None]
    for k in range(TOP_K):
        words = pltpu.bitcast(ybuf[buf, k], jnp.uint32)
        for half in range(2):
            part = pltpu.unpack_elementwise(words, index=half, packed_dtype=BF16, unpacked_dtype=F32)
            term = w[:, k:k + 1] * part
            acc[half] = term if k == 0 else acc[half] + term
    halves = []
    for half in range(2):
        acc_ref[half] = acc[half]
        halves.append(jnp.concatenate(
            [acc_ref[half, pl.ds(s, tc, stride=TOKEN_WORD_ROWS), :] for s in range(TOKEN_WORD_ROWS)], axis=1))
    routed = jnp.concatenate(halves, axis=1)
    x = x1_ref[...] + mod_ref[0][5:6, :] * (routed + sh_ref[...].astype(F32))
    o_ref[...] = x * lax.rsqrt(jnp.mean(x * x, axis=-1, keepdims=True) + EPS) * fw_ref[...]


def _combine(x1, shared, w_rows, mod3, final_w, slots, ys, seq):
    t, d = x1.shape
    tc = min(256, seq)
    n_tiles = t // tc
    tiles_per_seq = seq // tc
    cur = pl.BlockSpec((tc * TOP_K,), lambda i: (i,), memory_space=pltpu.SMEM)
    nxt = pl.BlockSpec((tc * TOP_K,), lambda i: (jnp.minimum(i + 1, n_tiles - 1),), memory_space=pltpu.SMEM)
    return pl.pallas_call(
        functools.partial(_combine_kernel, n_tiles=n_tiles), grid=(n_tiles,),
        in_specs=[pl.BlockSpec((tc, d), lambda i: (i, 0)),
                  pl.BlockSpec((tc, d), lambda i: (i, 0)),
                  pl.BlockSpec((tc * TOKEN_WORD_ROWS, TOP_K), lambda i: (i, 0)),
                  pl.BlockSpec((1, 6, d), lambda i: (i // tiles_per_seq, 0, 0)),
                  pl.BlockSpec((1, d), lambda i: (0, 0)),
                  cur, nxt,
                  pl.BlockSpec(memory_space=pl.ANY)],
        out_specs=pl.BlockSpec((tc, d), lambda i: (i, 0)),
        out_shape=jax.ShapeDtypeStruct((t, d), F32),
        scratch_shapes=[pltpu.VMEM((2, TOP_K, tc * TOKEN_ROWS, LANES), BF16),
                        pltpu.VMEM((2, tc * TOKEN_WORD_ROWS, LANES), F32),
                        pltpu.SemaphoreType.DMA((2,))],
        compiler_params=_params(1), name="combine",
    )(x1, shared, w_rows, mod3, final_w, slots, slots, ys)


def _rope_tables(seq, d):
    half = d // 2
    inv = ROPE_BASE ** (-jnp.arange(half, dtype=F32) / half)
    ang = jnp.arange(seq, dtype=F32)[:, None] * inv[None, :]
    cos, sin = jnp.cos(ang), jnp.sin(ang)
    return jnp.concatenate([cos, cos], axis=1), jnp.concatenate([-sin, sin], axis=1)


def kernel(x, c, w_ada, b_ada, norm1_w, w_in, ret_decay_fwd, ret_decay_bwd, ret_norm_w, w_ret_o, conv_w,
           w_conv_o, w_out, norm2_w, router_w, router_bias, w_gate, w_up, w_down, ws_gate, ws_up, ws_down,
           final_norm_w):
    batch, seq, d = x.shape
    depth = w_ada.shape[0]
    t = batch * seq
    bm = EXPERT_BLOCK
    n_blocks = (t * TOP_K + N_EXPERTS * (bm - 1)) // bm
    n_rows = n_blocks * bm
    cos, sin = _rope_tables(seq, HEAD_DIM)
    c_pad = jnp.pad(c, ((0, BF16_SUBLANES - batch % BF16_SUBLANES), (0, 0)))
    x2 = x.reshape(t, d)

    assert depth == 1, "the final norm is fused into the last stage of a single layer"
    for l in range(depth):
        mod = _ada(c_pad, w_ada[l], b_ada[l][None, :])[:batch]
        mod3 = mod.reshape(batch, 6, d)
        proj = _inproj(x2, mod3, norm1_w[l][None, :], w_in[l].astype(BF16), seq)
        lane_bcast = lambda v: jnp.broadcast_to(v[:, None, None], (RET_HEADS, 1, HEAD_DIM))
        og = _retention(proj, cos, sin, lane_bcast(ret_decay_fwd[l]), lane_bcast(ret_decay_bwd[l]),
                        ret_norm_w[l].reshape(RET_HEADS, 1, HEAD_DIM), batch, seq)
        merged = _mix(og, proj, conv_w[l], w_ret_o[l].astype(BF16), w_conv_o[l].astype(BF16), seq)
        x1, hp, logits_t = _outproj(merged, x2, mod3, norm2_w[l][None, :], w_out[l].astype(BF16),
                                    router_w[l].T, seq)
        ids_t, w_t, rank_t, cnt = _route(logits_t, router_bias[l][:, None])

        counts = cnt[:, 0]
        nblk = (counts + bm - 1) // bm
        blk_end = jnp.cumsum(nblk)
        pad_start = ((blk_end - nblk) * bm).astype(I32)
        n_used = blk_end[-1:].astype(I32)
        blk_ids = jnp.arange(n_blocks, dtype=I32)
        blk_e = jnp.minimum(jnp.sum((blk_ids[:, None] >= blk_end[None, :]).astype(I32), axis=1),
                            N_EXPERTS - 1)

        blk_first = jnp.concatenate([jnp.ones((1,), I32), (blk_e[1:] != blk_e[:-1]).astype(I32)])
        blk_slot = (jnp.cumsum(blk_first) - 1) % 2
        after = blk_end[blk_e]
        blk_next = jnp.where(after < n_used[0], blk_e[jnp.minimum(after, n_blocks - 1)], -1).astype(I32)

        onehot = ids_t[:, :, None] == jnp.arange(N_EXPERTS, dtype=I32)
        slots_t = rank_t + jnp.sum(jnp.where(onehot, pad_start, 0), axis=-1)
        slots = slots_t.T.reshape(t * TOP_K)
        xs, shared = _dispatch(pad_start, counts, hp, slots, ws_gate[l].astype(BF16), ws_up[l].astype(BF16),
                               ws_down[l].astype(BF16), n_rows, bm)
        ys = _experts(blk_e, blk_first, blk_slot.astype(I32), blk_next, n_used, xs,
                      w_gate[l], w_up[l], w_down[l], n_blocks, bm)
        w_rows = jnp.repeat(w_t.T, TOKEN_WORD_ROWS, axis=0)
        x2 = _combine(x1, shared, w_rows, mod3, final_norm_w[None, :], slots, ys, seq)
    return x2.reshape(batch, seq, d)
```

```python
import functools

import jax
import jax.numpy as jnp
from jax import lax
from jax.experimental import pallas as pl
from jax.experimental.pallas import tpu as pltpu

F32 = jnp.float32
BF16 = jnp.bfloat16
I32 = jnp.int32

EPS = 1e-6
RET_HEADS = 8
HEAD_DIM = 128
ROPE_BASE = 10000.0
N_EXPERTS = 64
TOP_K = 8
N_GROUPS = 8
TOPK_GROUPS = 4
GROUP_SIZE = N_EXPERTS // N_GROUPS
ROUTED_SCALE = 2.5

V7X_VMEM_BYTES = 64 * 1024 * 1024
VMEM_LIMIT = V7X_VMEM_BYTES - 8 * 1024 * 1024
BF16_SUBLANES = 16

RET_CHUNK = 512
EXPERT_BLOCK = 512
EXPERT_BLOCK_PARTS = 4
NEG_INF = float("-inf")


def _params(n_axes):
    return pltpu.CompilerParams(dimension_semantics=("arbitrary",) * n_axes,
                                vmem_limit_bytes=VMEM_LIMIT)


def _sigmoid(x):
    return 1.0 / (1.0 + jnp.exp(-x))


def _silu(x):
    return x * _sigmoid(x)


LANES = 128
TOKEN_WORD_ROWS = 8
TOKEN_ROWS = 2 * TOKEN_WORD_ROWS


def _store_tokens(ref, x, words_ref):
    n, m = x.shape[0], x.shape[1] // 2
    assert m == TOKEN_WORD_ROWS * LANES
    packed = pltpu.pack_elementwise([x[:, :m], x[:, m:]], packed_dtype=BF16)
    for s in range(TOKEN_WORD_ROWS):
        words_ref[pl.ds(s, n, stride=TOKEN_WORD_ROWS), :] = packed[:, s * LANES:(s + 1) * LANES]
    ref[pl.ds(0, n * TOKEN_ROWS), :] = pltpu.bitcast(words_ref[pl.ds(0, n * TOKEN_WORD_ROWS), :], BF16)


def _load_tokens(tiles, words_ref):
    n = tiles.shape[0] // TOKEN_ROWS
    words_ref[pl.ds(0, n * TOKEN_WORD_ROWS), :] = pltpu.bitcast(tiles, jnp.uint32)
    p = jnp.concatenate([words_ref[pl.ds(s, n, stride=TOKEN_WORD_ROWS), :]
                         for s in range(TOKEN_WORD_ROWS)], axis=1)
    a = pltpu.unpack_elementwise(p, index=0, packed_dtype=BF16, unpacked_dtype=F32)
    b = pltpu.unpack_elementwise(p, index=1, packed_dtype=BF16, unpacked_dtype=F32)
    return jnp.concatenate([a, b], axis=1)


def _token(row):
    return pl.ds(pl.multiple_of(row * TOKEN_ROWS, TOKEN_ROWS), TOKEN_ROWS)


def _words_scratch(n_tokens):
    return pltpu.VMEM((n_tokens * TOKEN_WORD_ROWS, LANES), jnp.uint32)


def _ada_kernel(c_ref, w_ref, b_ref, o_ref):
    s = _silu(c_ref[...]).astype(BF16)
    o_ref[...] = jnp.dot(s, w_ref[...].astype(BF16), preferred_element_type=F32) + b_ref[...]


def _ada(c_pad, w_ada, b_ada):
    m, d = c_pad.shape
    n = w_ada.shape[1]
    tn = 1024
    return pl.pallas_call(
        _ada_kernel, grid=(n // tn,),
        in_specs=[pl.BlockSpec((m, d), lambda j: (0, 0)),
                  pl.BlockSpec((d, tn), lambda j: (0, j)),
                  pl.BlockSpec((1, tn), lambda j: (0, j))],
        out_specs=pl.BlockSpec((m, tn), lambda j: (0, j)),
        out_shape=jax.ShapeDtypeStruct((m, n), F32),
        compiler_params=_params(1), name="ada")(c_pad, w_ada, b_ada)


def _inproj_kernel(x_ref, mod_ref, nw_ref, w_ref, o_ref, h_ref):
    @pl.when(pl.program_id(1) == 0)
    def _():
        x = x_ref[...]
        y = x * lax.rsqrt(jnp.mean(x * x, axis=-1, keepdims=True) + EPS) * nw_ref[...]
        m = mod_ref[0]
        h_ref[...] = (y * (1.0 + m[1:2, :]) + m[0:1, :]).astype(BF16)

    o_ref[...] = jnp.dot(h_ref[...], w_ref[...], preferred_element_type=F32).astype(o_ref.dtype)


def _inproj(x2, mod3, norm_w, w_in_b, seq):
    t, d = x2.shape
    n = w_in_b.shape[1]
    tm, tn = min(1024, seq), 1024
    tiles_per_seq = seq // tm
    return pl.pallas_call(
        _inproj_kernel, grid=(t // tm, n // tn),
        in_specs=[pl.BlockSpec((tm, d), lambda i, j: (i, 0)),
                  pl.BlockSpec((1, 6, d), lambda i, j: (i // tiles_per_seq, 0, 0)),
                  pl.BlockSpec((1, d), lambda i, j: (0, 0)),
                  pl.BlockSpec((d, tn), lambda i, j: (0, j))],
        out_specs=pl.BlockSpec((tm, tn), lambda i, j: (i, j)),
        out_shape=jax.ShapeDtypeStruct((t, n), BF16),
        scratch_shapes=[pltpu.VMEM((tm, d), BF16)],
        compiler_params=_params(2), name="inproj")(x2, mod3, norm_w, w_in_b)


def _log_sigmoid(x):
    return jnp.minimum(x, 0.0) - jnp.log1p(jnp.exp(-jnp.abs(x)))


def _ret_kernel(q_ref, k_ref, v_ref, g_ref, cos_ref, sin_ref, df_ref, db_ref, nw_ref, o_ref,
                mask_ref, qf_ref, qb_ref, kf_ref, kb_ref, dec_ref, qr_ref, kr_ref, acc_ref, *, chunk):
    seq, d = q_ref.shape
    n = seq // chunk
    c = chunk

    @pl.when(pl.program_id(1) == 0)
    def _():
        lgf = _log_sigmoid(df_ref[0])
        lgb = _log_sigmoid(db_ref[0])
        ii = lax.broadcasted_iota(I32, (c, c), 0)
        jj = lax.broadcasted_iota(I32, (c, c), 1)
        diff = (ii - jj).astype(F32)
        lgf_c = jnp.concatenate([lgf] * (c // d), axis=1)
        lgb_c = jnp.concatenate([lgb] * (c // d), axis=1)
        mask_ref[...] = jnp.where(diff >= 0.0,
                                  jnp.exp(lgf_c * jnp.maximum(diff, 0.0)),
                                  jnp.exp(lgb_c * jnp.maximum(-diff, 0.0)))
        pos = lax.broadcasted_iota(I32, (c, d), 0).astype(F32)
        qf_ref[...] = jnp.exp(lgf * (pos + 1.0))
        qb_ref[...] = jnp.exp(lgb * (c - pos))
        kf_ref[...] = jnp.exp(lgf * (c - 1.0 - pos))
        kb_ref[...] = jnp.exp(lgb * pos)
        dec_ref[0:1, :] = jnp.exp(lgf * c)
        dec_ref[1:2, :] = jnp.exp(lgb * c)

    scale = d ** -0.5
    nt = (((1,), (1,)), ((), ()))
    sls = [pl.ds(i * c, c) for i in range(n)]

    for sl in sls:
        cs = cos_ref[sl, :]
        sn = sin_ref[sl, :]
        q = q_ref[sl, :].astype(F32)
        k = k_ref[sl, :].astype(F32)
        qr_ref[sl, :] = q * cs + pltpu.roll(q, d // 2, 1) * sn
        kr_ref[sl, :] = (k * cs + pltpu.roll(k, d // 2, 1) * sn) * scale

    for sl in sls:
        s = lax.dot_general(qr_ref[sl, :].astype(BF16), kr_ref[sl, :].astype(BF16), nt,
                            preferred_element_type=F32)
        p = (s * mask_ref[...]).astype(BF16)
        acc_ref[sl, :] = jnp.dot(p, v_ref[sl, :], preferred_element_type=F32)

    def kv_state(sl, kw_ref):
        kw = (kr_ref[sl, :] * kw_ref[...]).T.astype(BF16)
        return jnp.dot(kw, v_ref[sl, :], preferred_element_type=F32)

    def scan(order, qw_ref, kw_ref, dec):
        st = jnp.zeros((d, d), F32)
        for idx, ci in enumerate(order):
            sl = sls[ci]
            if idx > 0:
                qw = (qr_ref[sl, :] * qw_ref[...]).astype(BF16)
                acc_ref[sl, :] += jnp.dot(qw, st.astype(BF16), preferred_element_type=F32)
            if idx < n - 1:
                st = dec * st + kv_state(sl, kw_ref)

    scan(list(range(n)), qf_ref, kf_ref, dec_ref[0:1, :])
    scan(list(range(n - 1, -1, -1)), qb_ref, kb_ref, dec_ref[1:2, :])

    nw = nw_ref[0]
    for sl in sls:
        o = acc_ref[sl, :]
        on = o * lax.rsqrt(jnp.mean(o * o, axis=-1, keepdims=True) + EPS) * nw
        g = g_ref[sl, :].astype(F32)
        o_ref[sl, :] = (_silu(g) * on).astype(o_ref.dtype)


def _retention(proj, cos, sin, dec_f, dec_b, ret_norm_w, batch, seq):
    h, d = RET_HEADS, HEAD_DIM
    c = min(RET_CHUNK, seq)
    col = lambda off: pl.BlockSpec((seq, d), lambda hh, b: (b, off + hh))
    per_head = pl.BlockSpec((1, 1, d), lambda hh, b: (hh, 0, 0))
    table = pl.BlockSpec((seq, d), lambda hh, b: (0, 0))
    vm = lambda shape, dt=F32: pltpu.VMEM(shape, dt)
    return pl.pallas_call(
        functools.partial(_ret_kernel, chunk=c), grid=(h, batch),
        in_specs=[col(0), col(h), col(2 * h), col(3 * h), table, table, per_head, per_head, per_head],
        out_specs=pl.BlockSpec((seq, d), lambda hh, b: (b, hh)),
        out_shape=jax.ShapeDtypeStruct((batch * seq, h * d), BF16),
        scratch_shapes=[vm((c, c)), vm((c, d)), vm((c, d)), vm((c, d)), vm((c, d)), vm((8, d)),
                        vm((seq, d)), vm((seq, d)), vm((seq, d))],
        compiler_params=_params(2), name="retention",
    )(proj, proj, proj, proj, cos, sin, dec_f, dec_b, ret_norm_w)


def _mix_kernel(og_ref, cb_ref, cc_ref, cu_ref, ccp_ref, cup_ref, ccn_ref, cun_ref,
                gr0_ref, gr1_ref, gc0_ref, gc1_ref, cw_ref, wr_ref, wc_ref, o_ref, *, tiles_per_seq):
    i = pl.program_id(0)
    tm, dr = cc_ref.shape
    u = cc_ref[...].astype(F32) * cu_ref[...].astype(F32)
    pos = i % tiles_per_seq
    last = BF16_SUBLANES - 1
    u_before = ccp_ref[last:last + 1, :].astype(F32) * cup_ref[last:last + 1, :].astype(F32)
    u_before = jnp.where(pos == 0, 0.0, u_before)
    u_after = ccn_ref[0:1, :].astype(F32) * cun_ref[0:1, :].astype(F32)
    u_after = jnp.where(pos == tiles_per_seq - 1, 0.0, u_after)
    row = lax.broadcasted_iota(I32, u.shape, 0)
    u_prev = jnp.where(row == 0, u_before, pltpu.roll(u, 1, 0))
    u_next = jnp.where(row == tm - 1, u_after, pltpu.roll(u, tm - 1, 0))
    cw = cw_ref[...]
    y = cw[0:1, :] * u_prev + cw[1:2, :] * u + cw[2:3, :] * u_next
    z = (cb_ref[...].astype(F32) * y).astype(BF16)

    yr = jnp.dot(og_ref[...], wr_ref[...], preferred_element_type=F32)
    yc = jnp.dot(z, wc_ref[...], preferred_element_type=F32)
    for lo, gr_ref, gc_ref in ((0, gr0_ref, gc0_ref), (dr, gr1_ref, gc1_ref)):
        merged = (_sigmoid(gr_ref[...].astype(F32)) * yr[:, lo:lo + dr]
                  + _sigmoid(gc_ref[...].astype(F32)) * yc[:, lo:lo + dr])
        o_ref[:, lo:lo + dr] = merged.astype(o_ref.dtype)


def _mix(og, proj, conv_w, w_ret_o_b, w_conv_o_b, seq):
    t, dr = og.shape
    d = w_ret_o_b.shape[1]
    assert d == 2 * dr
    tm = min(512, seq)
    tiles_per_seq = seq // tm
    hb = tm // BF16_SUBLANES
    n_hblk = t // BF16_SUBLANES
    wide = lambda off: pl.BlockSpec((tm, dr), lambda i: (i, off))
    before = lambda off: pl.BlockSpec((BF16_SUBLANES, dr), lambda i: (jnp.maximum(i * hb - 1, 0), off))
    after = lambda off: pl.BlockSpec((BF16_SUBLANES, dr), lambda i: (jnp.minimum((i + 1) * hb, n_hblk - 1), off))
    whole = lambda shape: pl.BlockSpec(shape, lambda i: (0, 0))
    return pl.pallas_call(
        functools.partial(_mix_kernel, tiles_per_seq=tiles_per_seq), grid=(t // tm,),
        in_specs=[wide(0), wide(4), wide(5), wide(6), before(5), before(6), after(5), after(6),
                  wide(7), wide(8), wide(9), wide(10),
                  whole((3, dr)), whole((dr, d)), whole((dr, d))],
        out_specs=pl.BlockSpec((tm, d), lambda i: (i, 0)),
        out_shape=jax.ShapeDtypeStruct((t, d), BF16),
        compiler_params=_params(1), name="mix",
    )(og, proj, proj, proj, proj, proj, proj, proj, proj, proj, proj, proj, conv_w, w_ret_o_b, w_conv_o_b)


def _outproj_kernel(m_ref, x_ref, mod_ref, nw_ref, w_ref, rwt_ref, x1_ref, hp_ref, lg_ref, words_ref):
    y = jnp.dot(m_ref[...], w_ref[...], preferred_element_type=F32)
    m = mod_ref[0]
    x1 = x_ref[...] + m[2:3, :] * y
    x1_ref[...] = x1
    hn = x1 * lax.rsqrt(jnp.mean(x1 * x1, axis=-1, keepdims=True) + EPS) * nw_ref[...]
    h = hn * (1.0 + m[4:5, :]) + m[3:4, :]
    _store_tokens(hp_ref, h, words_ref)
    h_hi = h.astype(BF16)
    h_lo = (h - h_hi.astype(F32)).astype(BF16)
    rw = rwt_ref[...]
    r_hi = rw.astype(BF16)
    r_lo = (rw - r_hi.astype(F32)).astype(BF16)
    nt = (((1,), (1,)), ((), ()))
    dot = functools.partial(lax.dot_general, dimension_numbers=nt, preferred_element_type=F32)
    lg_ref[...] = dot(r_hi, h_hi) + dot(r_hi, h_lo) + dot(r_lo, h_hi)


def _outproj(merged, x2, mod3, norm_w, w_out_b, router_wt, seq):
    t, d = x2.shape
    e = router_wt.shape[0]
    tm = min(512, seq)
    tiles_per_seq = seq // tm
    return pl.pallas_call(
        _outproj_kernel, grid=(t // tm,),
        in_specs=[pl.BlockSpec((tm, d), lambda i: (i, 0)),
                  pl.BlockSpec((tm, d), lambda i: (i, 0)),
                  pl.BlockSpec((1, 6, d), lambda i: (i // tiles_per_seq, 0, 0)),
                  pl.BlockSpec((1, d), lambda i: (0, 0)),
                  pl.BlockSpec((d, d), lambda i: (0, 0)),
                  pl.BlockSpec((e, d), lambda i: (0, 0))],
        out_specs=[pl.BlockSpec((tm, d), lambda i: (i, 0)),
                   pl.BlockSpec((TOKEN_ROWS * tm, LANES), lambda i: (i, 0)),
                   pl.BlockSpec((e, tm), lambda i: (0, i))],
        out_shape=[jax.ShapeDtypeStruct((t, d), F32),
                   jax.ShapeDtypeStruct((TOKEN_ROWS * t, LANES), BF16),
                   jax.ShapeDtypeStruct((e, t), F32)],
        scratch_shapes=[_words_scratch(tm)],
        compiler_params=_params(1), name="outproj",
    )(merged, x2, mod3, norm_w, w_out_b, router_wt)


def _first_max(x, iota, sentinel):
    m = jnp.max(x, axis=0, keepdims=True)
    idx = jnp.min(jnp.where(x == m, iota, sentinel), axis=0, keepdims=True)
    return m, idx, iota == idx


def _route_kernel(lg_ref, bias_ref, ids_ref, w_ref, rank_ref, cnt_ref, carry_ref):
    i = pl.program_id(0)
    e, tk = lg_ref.shape

    @pl.when(i == 0)
    def _():
        carry_ref[...] = jnp.zeros_like(carry_ref)

    s = _sigmoid(lg_ref[...])
    biased = s + bias_ref[...]
    sub = lax.broadcasted_iota(I32, (GROUP_SIZE, tk), 0)
    group_rows = []
    for g in range(N_GROUPS):
        xg = biased[g * GROUP_SIZE:(g + 1) * GROUP_SIZE, :]
        m1, _, pick = _first_max(xg, sub, GROUP_SIZE)
        m2 = jnp.max(jnp.where(pick, NEG_INF, xg), axis=0, keepdims=True)
        group_rows.append(m1 + m2)
    gs = jnp.concatenate(group_rows, axis=0)
    gsub = lax.broadcasted_iota(I32, (N_GROUPS, tk), 0)
    sel = jnp.zeros((N_GROUPS, tk), F32)
    for _ in range(TOPK_GROUPS):
        _, _, pick = _first_max(gs, gsub, N_GROUPS)
        sel = jnp.where(pick, 1.0, sel)
        gs = jnp.where(pick, NEG_INF, gs)
    masked_rows = []
    for g in range(N_GROUPS):
        xg = biased[g * GROUP_SIZE:(g + 1) * GROUP_SIZE, :]
        masked_rows.append(jnp.where(sel[g:g + 1, :] > 0.5, xg, NEG_INF))
    masked = jnp.concatenate(masked_rows, axis=0)

    eio = lax.broadcasted_iota(I32, (e, tk), 0)
    chosen = jnp.zeros((e, tk), F32)
    ids, top_s = [], []
    for _ in range(TOP_K):
        _, idx, pick = _first_max(masked, eio, e)
        ids.append(idx)
        top_s.append(jnp.sum(jnp.where(pick, s, 0.0), axis=0, keepdims=True))
        chosen = jnp.where(pick, 1.0, chosen)
        masked = jnp.where(pick, NEG_INF, masked)
    total = top_s[0]
    for ts in top_s[1:]:
        total = total + ts

    before = (lax.broadcasted_iota(I32, (tk, tk), 0) < lax.broadcasted_iota(I32, (tk, tk), 1))
    upper = jnp.where(before, 1.0, 0.0).astype(BF16)
    rank = jnp.dot(chosen.astype(BF16), upper, preferred_element_type=F32) + carry_ref[:, 0:1]
    for k in range(TOP_K):
        ids_ref[k:k + 1, :] = ids[k]
        w_ref[k:k + 1, :] = top_s[k] / total * ROUTED_SCALE
        rk = jnp.sum(jnp.where(eio == ids[k], rank, 0.0), axis=0, keepdims=True)
        rank_ref[k:k + 1, :] = rk.astype(I32)
    carry_ref[...] = carry_ref[...] + jnp.sum(chosen, axis=1, keepdims=True)
    cnt_ref[...] = carry_ref[...].astype(I32)


def _route(logits_t, bias_col):
    e, t = logits_t.shape
    tk = min(512, t)
    row8 = lambda dt: jax.ShapeDtypeStruct((TOP_K, t), dt)
    blk8 = pl.BlockSpec((TOP_K, tk), lambda i: (0, i))
    return pl.pallas_call(
        _route_kernel, grid=(t // tk,),
        in_specs=[pl.BlockSpec((e, tk), lambda i: (0, i)),
                  pl.BlockSpec((e, 1), lambda i: (0, 0))],
        out_specs=[blk8, blk8, blk8, pl.BlockSpec((e, 128), lambda i: (0, 0))],
        out_shape=[row8(I32), row8(F32), row8(I32), jax.ShapeDtypeStruct((e, 128), I32)],
        scratch_shapes=[pltpu.VMEM((e, 128), F32)],
        compiler_params=_params(1), name="route",
    )(logits_t, bias_col)


def _dispatch_kernel(pad_ref, cnt_ref, hp_ref, slot_ref, wsg_ref, wsu_ref, wsd_ref, xs_ref, sh_ref,
                     zero_ref, words_ref, sem, zsem, *, bm):
    i = pl.program_id(0)
    td = hp_ref.shape[0] // TOKEN_ROWS
    n_rows = xs_ref.shape[0] // TOKEN_ROWS
    n_blocks = n_rows // bm
    block = bm * TOKEN_ROWS

    @pl.when(i == 0)
    def _():
        zero_ref[...] = jnp.zeros_like(zero_ref)
        last_e = N_EXPERTS - 1
        used_blocks = pad_ref[last_e] // bm + (cnt_ref[last_e] + bm - 1) // bm

        def tail_copy(j):
            return pltpu.make_async_copy(zero_ref, xs_ref.at[pl.ds(pl.multiple_of(j * block, block), block)], zsem)

        def tail(j, c, wait):
            cp = tail_copy(j)
            cp.wait() if wait else cp.start()
            return c

        def pad(e, c, wait):
            row = pad_ref[e] + cnt_ref[e]
            n = (cnt_ref[e] + bm - 1) // bm * bm - cnt_ref[e]
            bit = bm // 2
            while bit:
                take = (n & bit) != 0
                size = bit * TOKEN_ROWS
                cp = pltpu.make_async_copy(
                    zero_ref.at[pl.ds(0, size)],
                    xs_ref.at[pl.ds(pl.multiple_of(row * TOKEN_ROWS, TOKEN_ROWS), size)], zsem)

                @pl.when(take)
                def _():
                    cp.wait() if wait else cp.start()

                row = row + jnp.where(take, bit, 0)
                bit //= 2
            return c

        for wait in (False, True):
            lax.fori_loop(used_blocks, n_blocks, functools.partial(tail, wait=wait), 0)
            lax.fori_loop(0, N_EXPERTS, functools.partial(pad, wait=wait), 0)

    def body(t, c):
        for k in range(TOP_K):
            slot = slot_ref[t * TOP_K + k]
            pltpu.make_async_copy(hp_ref.at[_token(t)], xs_ref.at[_token(slot)], sem).start(priority=k % 2)
        return c

    lax.fori_loop(0, td, body, 0)

    h = _load_tokens(hp_ref[...], words_ref).astype(BF16)
    sg = jnp.dot(h, wsg_ref[...], preferred_element_type=F32)
    su = jnp.dot(h, wsu_ref[...], preferred_element_type=F32)
    shared = jnp.dot((_silu(sg) * su).astype(BF16), wsd_ref[...], preferred_element_type=F32)
    sh_ref[...] = shared.astype(sh_ref.dtype)

    for k in range(TOP_K):
        pltpu.make_async_copy(hp_ref, xs_ref.at[pl.ds(0, td * TOKEN_ROWS)], sem).wait()


def _dispatch(pad_start, counts, hp, slots, wsg_b, wsu_b, wsd_b, n_rows, bm):
    t = hp.shape[0] // TOKEN_ROWS
    d, ds = wsg_b.shape
    td = min(512, t)
    whole = lambda shape: pl.BlockSpec(shape, lambda i, *_: (0, 0))
    grid_spec = pltpu.PrefetchScalarGridSpec(
        num_scalar_prefetch=2, grid=(t // td,),
        in_specs=[pl.BlockSpec((td * TOKEN_ROWS, LANES), lambda i, *_: (i, 0)),
                  pl.BlockSpec((td * TOP_K,), lambda i, *_: (i,), memory_space=pltpu.SMEM),
                  whole((d, ds)), whole((d, ds)), whole((ds, d))],
        out_specs=[pl.BlockSpec(memory_space=pl.ANY),
                   pl.BlockSpec((td, d), lambda i, *_: (i, 0))],
        scratch_shapes=[pltpu.VMEM((bm * TOKEN_ROWS, LANES), BF16), _words_scratch(td),
                        pltpu.SemaphoreType.DMA(()), pltpu.SemaphoreType.DMA(())])
    return pl.pallas_call(
        functools.partial(_dispatch_kernel, bm=bm), grid_spec=grid_spec,
        out_shape=[jax.ShapeDtypeStruct((n_rows * TOKEN_ROWS, LANES), BF16),
                   jax.ShapeDtypeStruct((t, d), BF16)],
        compiler_params=_params(1), name="dispatch",
    )(pad_start, counts, hp, slots, wsg_b, wsu_b, wsd_b)


def _expert_kernel(be_ref, first_ref, slot_ref, nxt_ref, rows_ref, nu_ref, x_ref, wg_hbm, wu_hbm, wd_hbm, o_ref,
                   words_ref, wg32, wu32, wd32, wg16, wu16, wd16, wsem):
    i = pl.program_id(0)

    def weight_copies(e, s):
        return (pltpu.make_async_copy(wg_hbm.at[e], wg32.at[s], wsem.at[s, 0]),
                pltpu.make_async_copy(wu_hbm.at[e], wu32.at[s], wsem.at[s, 1]),
                pltpu.make_async_copy(wd_hbm.at[e], wd32.at[s], wsem.at[s, 2]))

    @pl.when(i == 0)
    def _():
        for cp in weight_copies(be_ref[0], 0):
            cp.start()

    @pl.when(jnp.logical_and(first_ref[i] == 1, i < nu_ref[0]))
    def _():
        s = slot_ref[i]
        for cp in weight_copies(be_ref[i], s):
            cp.wait()

        @pl.when(nxt_ref[i] >= 0)
        def _():
            for cp in weight_copies(nxt_ref[i], 1 - s):
                cp.start()

        wg16[...] = wg32[s].astype(BF16)
        wu16[...] = wu32[s].astype(BF16)
        wd16[...] = wd32[s].astype(BF16)

    bm = x_ref.shape[0] // TOKEN_ROWS
    quarter = bm // EXPERT_BLOCK_PARTS
    used = jnp.where(i < nu_ref[0], (rows_ref[i] + quarter - 1) // quarter, 0)

    def swiglu_rows(n):
        x = _load_tokens(x_ref[pl.ds(0, n * TOKEN_ROWS), :], words_ref).astype(BF16)
        g = jnp.dot(x, wg16[...], preferred_element_type=F32)
        u = jnp.dot(x, wu16[...], preferred_element_type=F32)
        mid = (_silu(g) * u).astype(BF16)
        _store_tokens(o_ref, jnp.dot(mid, wd16[...], preferred_element_type=F32), words_ref)
        if n < bm:
            rest = (bm - n) * TOKEN_ROWS
            o_ref[pl.ds(n * TOKEN_ROWS, rest), :] = jnp.zeros((rest, LANES), o_ref.dtype)

    for parts in range(1, EXPERT_BLOCK_PARTS + 1):
        pl.when(used == parts)(functools.partial(swiglu_rows, parts * quarter))

    @pl.when(used == 0)
    def _():
        o_ref[...] = jnp.zeros_like(o_ref)


def _experts(blk_e, blk_first, blk_slot, blk_next, blk_rows, n_used, xs, w_gate, w_up, w_down, n_blocks, bm):
    _, d, de = w_gate.shape
    x_map = lambda i, be, fi, sl, nx, ro, nu: (jnp.minimum(i, nu[0] - 1), 0)
    hbm = pl.BlockSpec(memory_space=pl.ANY)
    grid_spec = pltpu.PrefetchScalarGridSpec(
        num_scalar_prefetch=6, grid=(n_blocks,),
        in_specs=[pl.BlockSpec((bm * TOKEN_ROWS, LANES), x_map), hbm, hbm, hbm],
        out_specs=pl.BlockSpec((bm * TOKEN_ROWS, LANES), lambda i, *_: (i, 0)),
        scratch_shapes=[_words_scratch(bm),
                        pltpu.VMEM((2, d, de), F32), pltpu.VMEM((2, d, de), F32), pltpu.VMEM((2, de, d), F32),
                        pltpu.VMEM((d, de), BF16), pltpu.VMEM((d, de), BF16), pltpu.VMEM((de, d), BF16),
                        pltpu.SemaphoreType.DMA((2, 3))])
    return pl.pallas_call(
        _expert_kernel, grid_spec=grid_spec,
        out_shape=jax.ShapeDtypeStruct((n_blocks * bm * TOKEN_ROWS, LANES), BF16),
        compiler_params=_params(1), name="experts",
    )(blk_e, blk_first, blk_slot, blk_next, blk_rows, n_used, xs, w_gate, w_up, w_down)


def _combine_kernel(x1_ref, sh_ref, w_ref, mod_ref, fw_ref, cur_ref, nxt_ref, ys_ref, o_ref,
                    ybuf, acc_ref, sem, *, n_tiles):
    i = pl.program_id(0)
    tc = x1_ref.shape[0]

    def issue(slot_ref, buf):
        def body(t, c):
            for k in range(TOP_K):
                slot = slot_ref[t * TOP_K + k]
                pltpu.make_async_copy(ys_ref.at[_token(slot)], ybuf.at[buf, k, _token(t)],
                                      sem.at[buf]).start(priority=k % 2)
            return c
        lax.fori_loop(0, tc, body, 0)

    @pl.when(i == 0)
    def _():
        issue(cur_ref, 0)

    @pl.when(i + 1 < n_tiles)
    def _():
        issue(nxt_ref, (i + 1) % 2)

    buf = i % 2
    for k in range(TOP_K):
        pltpu.make_async_copy(ys_ref.at[pl.ds(0, tc * TOKEN_ROWS)], ybuf.at[buf, k], sem.at[buf]).wait()

    w = w_ref[...]
    acc = [None, None]
    for k in range(TOP_K):
        words = pltpu.bitcast(ybuf[buf, k], jnp.uint32)
        for half in range(2):
            part = pltpu.unpack_elementwise(words, index=half, packed_dtype=BF16, unpacked_dtype=F32)
            term = w[:, k:k + 1] * part
            acc[half] = term if k == 0 else acc[half] + term
    halves = []
    for half in range(2):
        acc_ref[half] = acc[half]
        halves.append(jnp.concatenate(
            [acc_ref[half, pl.ds(s, tc, stride=TOKEN_WORD_ROWS), :] for s in range(TOKEN_WORD_ROWS)], axis=1))
    routed = jnp.concatenate(halves, axis=1)
    x = x1_ref[...] + mod_ref[0][5:6, :] * (routed + sh_ref[...].astype(F32))
    o_ref[...] = x * lax.rsqrt(jnp.mean(x * x, axis=-1, keepdims=True) + EPS) * fw_ref[...]


def _combine(x1, shared, w_rows, mod3, final_w, slots, ys, seq):
    t, d = x1.shape
    tc = min(256, seq)
    n_tiles = t // tc
    tiles_per_seq = seq // tc
    cur = pl.BlockSpec((tc * TOP_K,), lambda i: (i,), memory_space=pltpu.SMEM)
    nxt = pl.BlockSpec((tc * TOP_K,), lambda i: (jnp.minimum(i + 1, n_tiles - 1),), memory_space=pltpu.SMEM)
    return pl.pallas_call(
        functools.partial(_combine_kernel, n_tiles=n_tiles), grid=(n_tiles,),
        in_specs=[pl.BlockSpec((tc, d), lambda i: (i, 0)),
                  pl.BlockSpec((tc, d), lambda i: (i, 0)),
                  pl.BlockSpec((tc * TOKEN_WORD_ROWS, TOP_K), lambda i: (i, 0)),
                  pl.BlockSpec((1, 6, d), lambda i: (i // tiles_per_seq, 0, 0)),
                  pl.BlockSpec((1, d), lambda i: (0, 0)),
                  cur, nxt,
                  pl.BlockSpec(memory_space=pl.ANY)],
        out_specs=pl.BlockSpec((tc, d), lambda i: (i, 0)),
        out_shape=jax.ShapeDtypeStruct((t, d), F32),
        scratch_shapes=[pltpu.VMEM((2, TOP_K, tc * TOKEN_ROWS, LANES), BF16),
                        pltpu.VMEM((2, tc * TOKEN_WORD_ROWS, LANES), F32),
                        pltpu.SemaphoreType.DMA((2,))],
        compiler_params=_params(1), name="combine",
    )(x1, shared, w_rows, mod3, final_w, slots, slots, ys)


def _rope_tables(seq, d):
    half = d // 2
    inv = ROPE_BASE ** (-jnp.arange(half, dtype=F32) / half)
    ang = jnp.arange(seq, dtype=F32)[:, None] * inv[None, :]
    cos, sin = jnp.cos(ang), jnp.sin(ang)
    return jnp.concatenate([cos, cos], axis=1), jnp.concatenate([-sin, sin], axis=1)


def kernel(x, c, w_ada, b_ada, norm1_w, w_in, ret_decay_fwd, ret_decay_bwd, ret_norm_w, w_ret_o, conv_w,
           w_conv_o, w_out, norm2_w, router_w, router_bias, w_gate, w_up, w_down, ws_gate, ws_up, ws_down,
           final_norm_w):
    batch, seq, d = x.shape
    depth = w_ada.shape[0]
    t = batch * seq
    bm = EXPERT_BLOCK
    n_blocks = (t * TOP_K + N_EXPERTS * (bm - 1)) // bm
    n_rows = n_blocks * bm
    cos, sin = _rope_tables(seq, HEAD_DIM)
    c_pad = jnp.pad(c, ((0, BF16_SUBLANES - batch % BF16_SUBLANES), (0, 0)))
    x2 = x.reshape(t, d)

    assert depth == 1, "the final norm is fused into the last stage of a single layer"
    for l in range(depth):
        mod = _ada(c_pad, w_ada[l], b_ada[l][None, :])[:batch]
        mod3 = mod.reshape(batch, 6, d)
        proj = _inproj(x2, mod3, norm1_w[l][None, :], w_in[l].astype(BF16), seq)
        lane_bcast = lambda v: jnp.broadcast_to(v[:, None, None], (RET_HEADS, 1, HEAD_DIM))
        og = _retention(proj, cos, sin, lane_bcast(ret_decay_fwd[l]), lane_bcast(ret_decay_bwd[l]),
                        ret_norm_w[l].reshape(RET_HEADS, 1, HEAD_DIM), batch, seq)
        merged = _mix(og, proj, conv_w[l], w_ret_o[l].astype(BF16), w_conv_o[l].astype(BF16), seq)
        x1, hp, logits_t = _outproj(merged, x2, mod3, norm2_w[l][None, :], w_out[l].astype(BF16),
                                    router_w[l].T, seq)
        ids_t, w_t, rank_t, cnt = _route(logits_t, router_bias[l][:, None])

        counts = cnt[:, 0]
        nblk = (counts + bm - 1) // bm
        blk_end = jnp.cumsum(nblk)
        pad_start = ((blk_end - nblk) * bm).astype(I32)
        n_used = blk_end[-1:].astype(I32)
        blk_ids = jnp.arange(n_blocks, dtype=I32)
        blk_e = jnp.minimum(jnp.sum((blk_ids[:, None] >= blk_end[None, :]).astype(I32), axis=1),
                            N_EXPERTS - 1)

        blk_first = jnp.concatenate([jnp.ones((1,), I32), (blk_e[1:] != blk_e[:-1]).astype(I32)])
        blk_slot = (jnp.cumsum(blk_first) - 1) % 2
        after = blk_end[blk_e]
        blk_next = jnp.where(after < n_used[0], blk_e[jnp.minimum(after, n_blocks - 1)], -1).astype(I32)

        onehot = ids_t[:, :, None] == jnp.arange(N_EXPERTS, dtype=I32)
        slots_t = rank_t + jnp.sum(jnp.where(onehot, pad_start, 0), axis=-1)
        slots = slots_t.T.reshape(t * TOP_K)
        xs, shared = _dispatch(pad_start, counts, hp, slots, ws_gate[l].astype(BF16), ws_up[l].astype(BF16),
                               ws_down[l].astype(BF16), n_rows, bm)
        blk_rows = jnp.clip(counts[blk_e] - (blk_ids * bm - pad_start[blk_e]), 0, bm).astype(I32)
        ys = _experts(blk_e, blk_first, blk_slot.astype(I32), blk_next, blk_rows, n_used, xs,
                      w_gate[l], w_up[l], w_down[l], n_blocks, bm)
        w_rows = jnp.repeat(w_t.T, TOKEN_WORD_ROWS, axis=0)
        x2 = _combine(x1, shared, w_rows, mod3, final_norm_w[None, :], slots, ys, seq)
    return x2.reshape(batch, seq, d)
```

```python
import functools

import jax
import jax.numpy as jnp
from jax import lax
from jax.experimental import pallas as pl
from jax.experimental.pallas import tpu as pltpu

F32 = jnp.float32
BF16 = jnp.bfloat16
I32 = jnp.int32

EPS = 1e-6
RET_HEADS = 8
HEAD_DIM = 128
ROPE_BASE = 10000.0
N_EXPERTS = 64
TOP_K = 8
N_GROUPS = 8
TOPK_GROUPS = 4
GROUP_SIZE = N_EXPERTS // N_GROUPS
ROUTED_SCALE = 2.5

V7X_VMEM_BYTES = 64 * 1024 * 1024
VMEM_LIMIT = V7X_VMEM_BYTES - 8 * 1024 * 1024
BF16_SUBLANES = 16

RET_CHUNK = 256
EXPERT_BLOCK = 512
EXPERT_BLOCK_PARTS = 4
NEG_INF = float("-inf")


def _params(n_axes):
    return pltpu.CompilerParams(dimension_semantics=("arbitrary",) * n_axes,
                                vmem_limit_bytes=VMEM_LIMIT)


def _sigmoid(x):
    return 1.0 / (1.0 + jnp.exp(-x))


def _silu(x):
    return x * _sigmoid(x)


LANES = 128
TOKEN_WORD_ROWS = 8
TOKEN_ROWS = 2 * TOKEN_WORD_ROWS


def _store_tokens(ref, x, words_ref):
    n, m = x.shape[0], x.shape[1] // 2
    assert m == TOKEN_WORD_ROWS * LANES
    packed = pltpu.pack_elementwise([x[:, :m], x[:, m:]], packed_dtype=BF16)
    for s in range(TOKEN_WORD_ROWS):
        words_ref[pl.ds(s, n, stride=TOKEN_WORD_ROWS), :] = packed[:, s * LANES:(s + 1) * LANES]
    ref[pl.ds(0, n * TOKEN_ROWS), :] = pltpu.bitcast(words_ref[pl.ds(0, n * TOKEN_WORD_ROWS), :], BF16)


def _load_tokens(tiles, words_ref):
    n = tiles.shape[0] // TOKEN_ROWS
    words_ref[pl.ds(0, n * TOKEN_WORD_ROWS), :] = pltpu.bitcast(tiles, jnp.uint32)
    p = jnp.concatenate([words_ref[pl.ds(s, n, stride=TOKEN_WORD_ROWS), :]
                         for s in range(TOKEN_WORD_ROWS)], axis=1)
    a = pltpu.unpack_elementwise(p, index=0, packed_dtype=BF16, unpacked_dtype=F32)
    b = pltpu.unpack_elementwise(p, index=1, packed_dtype=BF16, unpacked_dtype=F32)
    return jnp.concatenate([a, b], axis=1)


def _token(row):
    return pl.ds(pl.multiple_of(row * TOKEN_ROWS, TOKEN_ROWS), TOKEN_ROWS)


def _words_scratch(n_tokens):
    return pltpu.VMEM((n_tokens * TOKEN_WORD_ROWS, LANES), jnp.uint32)


def _ada_kernel(c_ref, w_ref, b_ref, o_ref):
    s = _silu(c_ref[...]).astype(BF16)
    o_ref[...] = jnp.dot(s, w_ref[...].astype(BF16), preferred_element_type=F32) + b_ref[...]


def _ada(c_pad, w_ada, b_ada):
    m, d = c_pad.shape
    n = w_ada.shape[1]
    tn = 1024
    return pl.pallas_call(
        _ada_kernel, grid=(n // tn,),
        in_specs=[pl.BlockSpec((m, d), lambda j: (0, 0)),
                  pl.BlockSpec((d, tn), lambda j: (0, j)),
                  pl.BlockSpec((1, tn), lambda j: (0, j))],
        out_specs=pl.BlockSpec((m, tn), lambda j: (0, j)),
        out_shape=jax.ShapeDtypeStruct((m, n), F32),
        compiler_params=_params(1), name="ada")(c_pad, w_ada, b_ada)


def _inproj_kernel(x_ref, mod_ref, nw_ref, w_ref, o_ref, h_ref):
    @pl.when(pl.program_id(1) == 0)
    def _():
        x = x_ref[...]
        y = x * lax.rsqrt(jnp.mean(x * x, axis=-1, keepdims=True) + EPS) * nw_ref[...]
        m = mod_ref[0]
        h_ref[...] = (y * (1.0 + m[1:2, :]) + m[0:1, :]).astype(BF16)

    o_ref[...] = jnp.dot(h_ref[...], w_ref[...], preferred_element_type=F32).astype(o_ref.dtype)


def _inproj(x2, mod3, norm_w, w_in_b, seq):
    t, d = x2.shape
    n = w_in_b.shape[1]
    tm, tn = min(1024, seq), 1024
    tiles_per_seq = seq // tm
    return pl.pallas_call(
        _inproj_kernel, grid=(t // tm, n // tn),
        in_specs=[pl.BlockSpec((tm, d), lambda i, j: (i, 0)),
                  pl.BlockSpec((1, 6, d), lambda i, j: (i // tiles_per_seq, 0, 0)),
                  pl.BlockSpec((1, d), lambda i, j: (0, 0)),
                  pl.BlockSpec((d, tn), lambda i, j: (0, j))],
        out_specs=pl.BlockSpec((tm, tn), lambda i, j: (i, j)),
        out_shape=jax.ShapeDtypeStruct((t, n), BF16),
        scratch_shapes=[pltpu.VMEM((tm, d), BF16)],
        compiler_params=_params(2), name="inproj")(x2, mod3, norm_w, w_in_b)


def _log_sigmoid(x):
    return jnp.minimum(x, 0.0) - jnp.log1p(jnp.exp(-jnp.abs(x)))


def _ret_kernel(q_ref, k_ref, v_ref, g_ref, cos_ref, sin_ref, df_ref, db_ref, nw_ref, o_ref,
                mask_ref, qf_ref, qb_ref, kf_ref, kb_ref, dec_ref, qr_ref, kr_ref, acc_ref, *, chunk):
    seq, d = q_ref.shape
    n = seq // chunk
    c = chunk

    @pl.when(pl.program_id(1) == 0)
    def _():
        lgf = _log_sigmoid(df_ref[0])
        lgb = _log_sigmoid(db_ref[0])
        ii = lax.broadcasted_iota(I32, (c, c), 0)
        jj = lax.broadcasted_iota(I32, (c, c), 1)
        diff = (ii - jj).astype(F32)
        lgf_c = jnp.concatenate([lgf] * (c // d), axis=1)
        lgb_c = jnp.concatenate([lgb] * (c // d), axis=1)
        mask_ref[...] = jnp.where(diff >= 0.0,
                                  jnp.exp(lgf_c * jnp.maximum(diff, 0.0)),
                                  jnp.exp(lgb_c * jnp.maximum(-diff, 0.0)))
        pos = lax.broadcasted_iota(I32, (c, d), 0).astype(F32)
        qf_ref[...] = jnp.exp(lgf * (pos + 1.0))
        qb_ref[...] = jnp.exp(lgb * (c - pos))
        kf_ref[...] = jnp.exp(lgf * (c - 1.0 - pos))
        kb_ref[...] = jnp.exp(lgb * pos)
        dec_ref[0:1, :] = jnp.exp(lgf * c)
        dec_ref[1:2, :] = jnp.exp(lgb * c)

    scale = d ** -0.5
    nt = (((1,), (1,)), ((), ()))
    sls = [pl.ds(i * c, c) for i in range(n)]

    for sl in sls:
        cs = cos_ref[sl, :]
        sn = sin_ref[sl, :]
        q = q_ref[sl, :].astype(F32)
        k = k_ref[sl, :].astype(F32)
        qr_ref[sl, :] = q * cs + pltpu.roll(q, d // 2, 1) * sn
        kr_ref[sl, :] = (k * cs + pltpu.roll(k, d // 2, 1) * sn) * scale

    for sl in sls:
        s = lax.dot_general(qr_ref[sl, :].astype(BF16), kr_ref[sl, :].astype(BF16), nt,
                            preferred_element_type=F32)
        p = (s * mask_ref[...]).astype(BF16)
        acc_ref[sl, :] = jnp.dot(p, v_ref[sl, :], preferred_element_type=F32)

    def kv_state(sl, kw_ref):
        kw = (kr_ref[sl, :] * kw_ref[...]).T.astype(BF16)
        return jnp.dot(kw, v_ref[sl, :], preferred_element_type=F32)

    def scan(order, qw_ref, kw_ref, dec):
        st = jnp.zeros((d, d), F32)
        for idx, ci in enumerate(order):
            sl = sls[ci]
            if idx > 0:
                qw = (qr_ref[sl, :] * qw_ref[...]).astype(BF16)
                acc_ref[sl, :] += jnp.dot(qw, st.astype(BF16), preferred_element_type=F32)
            if idx < n - 1:
                st = dec * st + kv_state(sl, kw_ref)

    scan(list(range(n)), qf_ref, kf_ref, dec_ref[0:1, :])
    scan(list(range(n - 1, -1, -1)), qb_ref, kb_ref, dec_ref[1:2, :])

    nw = nw_ref[0]
    for sl in sls:
        o = acc_ref[sl, :]
        on = o * lax.rsqrt(jnp.mean(o * o, axis=-1, keepdims=True) + EPS) * nw
        g = g_ref[sl, :].astype(F32)
        o_ref[sl, :] = (_silu(g) * on).astype(o_ref.dtype)


def _retention(proj, cos, sin, dec_f, dec_b, ret_norm_w, batch, seq):
    h, d = RET_HEADS, HEAD_DIM
    c = min(RET_CHUNK, seq)
    col = lambda off: pl.BlockSpec((seq, d), lambda hh, b: (b, off + hh))
    per_head = pl.BlockSpec((1, 1, d), lambda hh, b: (hh, 0, 0))
    table = pl.BlockSpec((seq, d), lambda hh, b: (0, 0))
    vm = lambda shape, dt=F32: pltpu.VMEM(shape, dt)
    return pl.pallas_call(
        functools.partial(_ret_kernel, chunk=c), grid=(h, batch),
        in_specs=[col(0), col(h), col(2 * h), col(3 * h), table, table, per_head, per_head, per_head],
        out_specs=pl.BlockSpec((seq, d), lambda hh, b: (b, hh)),
        out_shape=jax.ShapeDtypeStruct((batch * seq, h * d), BF16),
        scratch_shapes=[vm((c, c)), vm((c, d)), vm((c, d)), vm((c, d)), vm((c, d)), vm((8, d)),
                        vm((seq, d)), vm((seq, d)), vm((seq, d))],
        compiler_params=_params(2), name="retention",
    )(proj, proj, proj, proj, cos, sin, dec_f, dec_b, ret_norm_w)


def _mix_kernel(og_ref, cb_ref, cc_ref, cu_ref, ccp_ref, cup_ref, ccn_ref, cun_ref,
                gr0_ref, gr1_ref, gc0_ref, gc1_ref, cw_ref, wr_ref, wc_ref, o_ref, *, tiles_per_seq):
    i = pl.program_id(0)
    tm, dr = cc_ref.shape
    u = cc_ref[...].astype(F32) * cu_ref[...].astype(F32)
    pos = i % tiles_per_seq
    last = BF16_SUBLANES - 1
    u_before = ccp_ref[last:last + 1, :].astype(F32) * cup_ref[last:last + 1, :].astype(F32)
    u_before = jnp.where(pos == 0, 0.0, u_before)
    u_after = ccn_ref[0:1, :].astype(F32) * cun_ref[0:1, :].astype(F32)
    u_after = jnp.where(pos == tiles_per_seq - 1, 0.0, u_after)
    row = lax.broadcasted_iota(I32, u.shape, 0)
    u_prev = jnp.where(row == 0, u_before, pltpu.roll(u, 1, 0))
    u_next = jnp.where(row == tm - 1, u_after, pltpu.roll(u, tm - 1, 0))
    cw = cw_ref[...]
    y = cw[0:1, :] * u_prev + cw[1:2, :] * u + cw[2:3, :] * u_next
    z = (cb_ref[...].astype(F32) * y).astype(BF16)

    yr = jnp.dot(og_ref[...], wr_ref[...], preferred_element_type=F32)
    yc = jnp.dot(z, wc_ref[...], preferred_element_type=F32)
    for lo, gr_ref, gc_ref in ((0, gr0_ref, gc0_ref), (dr, gr1_ref, gc1_ref)):
        merged = (_sigmoid(gr_ref[...].astype(F32)) * yr[:, lo:lo + dr]
                  + _sigmoid(gc_ref[...].astype(F32)) * yc[:, lo:lo + dr])
        o_ref[:, lo:lo + dr] = merged.astype(o_ref.dtype)


def _mix(og, proj, conv_w, w_ret_o_b, w_conv_o_b, seq):
    t, dr = og.shape
    d = w_ret_o_b.shape[1]
    assert d == 2 * dr
    tm = min(512, seq)
    tiles_per_seq = seq // tm
    hb = tm // BF16_SUBLANES
    n_hblk = t // BF16_SUBLANES
    wide = lambda off: pl.BlockSpec((tm, dr), lambda i: (i, off))
    before = lambda off: pl.BlockSpec((BF16_SUBLANES, dr), lambda i: (jnp.maximum(i * hb - 1, 0), off))
    after = lambda off: pl.BlockSpec((BF16_SUBLANES, dr), lambda i: (jnp.minimum((i + 1) * hb, n_hblk - 1), off))
    whole = lambda shape: pl.BlockSpec(shape, lambda i: (0, 0))
    return pl.pallas_call(
        functools.partial(_mix_kernel, tiles_per_seq=tiles_per_seq), grid=(t // tm,),
        in_specs=[wide(0), wide(4), wide(5), wide(6), before(5), before(6), after(5), after(6),
                  wide(7), wide(8), wide(9), wide(10),
                  whole((3, dr)), whole((dr, d)), whole((dr, d))],
        out_specs=pl.BlockSpec((tm, d), lambda i: (i, 0)),
        out_shape=jax.ShapeDtypeStruct((t, d), BF16),
        compiler_params=_params(1), name="mix",
    )(og, proj, proj, proj, proj, proj, proj, proj, proj, proj, proj, proj, conv_w, w_ret_o_b, w_conv_o_b)


def _outproj_kernel(m_ref, x_ref, mod_ref, nw_ref, w_ref, rwt_ref, x1_ref, hp_ref, lg_ref, words_ref):
    y = jnp.dot(m_ref[...], w_ref[...], preferred_element_type=F32)
    m = mod_ref[0]
    x1 = x_ref[...] + m[2:3, :] * y
    x1_ref[...] = x1
    hn = x1 * lax.rsqrt(jnp.mean(x1 * x1, axis=-1, keepdims=True) + EPS) * nw_ref[...]
    h = hn * (1.0 + m[4:5, :]) + m[3:4, :]
    _store_tokens(hp_ref, h, words_ref)
    h_hi = h.astype(BF16)
    h_lo = (h - h_hi.astype(F32)).astype(BF16)
    rw = rwt_ref[...]
    r_hi = rw.astype(BF16)
    r_lo = (rw - r_hi.astype(F32)).astype(BF16)
    nt = (((1,), (1,)), ((), ()))
    dot = functools.partial(lax.dot_general, dimension_numbers=nt, preferred_element_type=F32)
    lg_ref[...] = dot(r_hi, h_hi) + dot(r_hi, h_lo) + dot(r_lo, h_hi)


def _outproj(merged, x2, mod3, norm_w, w_out_b, router_wt, seq):
    t, d = x2.shape
    e = router_wt.shape[0]
    tm = min(512, seq)
    tiles_per_seq = seq // tm
    return pl.pallas_call(
        _outproj_kernel, grid=(t // tm,),
        in_specs=[pl.BlockSpec((tm, d), lambda i: (i, 0)),
                  pl.BlockSpec((tm, d), lambda i: (i, 0)),
                  pl.BlockSpec((1, 6, d), lambda i: (i // tiles_per_seq, 0, 0)),
                  pl.BlockSpec((1, d), lambda i: (0, 0)),
                  pl.BlockSpec((d, d), lambda i: (0, 0)),
                  pl.BlockSpec((e, d), lambda i: (0, 0))],
        out_specs=[pl.BlockSpec((tm, d), lambda i: (i, 0)),
                   pl.BlockSpec((TOKEN_ROWS * tm, LANES), lambda i: (i, 0)),
                   pl.BlockSpec((e, tm), lambda i: (0, i))],
        out_shape=[jax.ShapeDtypeStruct((t, d), F32),
                   jax.ShapeDtypeStruct((TOKEN_ROWS * t, LANES), BF16),
                   jax.ShapeDtypeStruct((e, t), F32)],
        scratch_shapes=[_words_scratch(tm)],
        compiler_params=_params(1), name="outproj",
    )(merged, x2, mod3, norm_w, w_out_b, router_wt)


def _first_max(x, iota, sentinel):
    m = jnp.max(x, axis=0, keepdims=True)
    idx = jnp.min(jnp.where(x == m, iota, sentinel), axis=0, keepdims=True)
    return m, idx, iota == idx


def _route_kernel(lg_ref, bias_ref, ids_ref, w_ref, rank_ref, cnt_ref, carry_ref):
    i = pl.program_id(0)
    e, tk = lg_ref.shape

    @pl.when(i == 0)
    def _():
        carry_ref[...] = jnp.zeros_like(carry_ref)

    s = _sigmoid(lg_ref[...])
    biased = s + bias_ref[...]
    sub = lax.broadcasted_iota(I32, (GROUP_SIZE, tk), 0)
    group_rows = []
    for g in range(N_GROUPS):
        xg = biased[g * GROUP_SIZE:(g + 1) * GROUP_SIZE, :]
        m1, _, pick = _first_max(xg, sub, GROUP_SIZE)
        m2 = jnp.max(jnp.where(pick, NEG_INF, xg), axis=0, keepdims=True)
        group_rows.append(m1 + m2)
    gs = jnp.concatenate(group_rows, axis=0)
    gsub = lax.broadcasted_iota(I32, (N_GROUPS, tk), 0)
    sel = jnp.zeros((N_GROUPS, tk), F32)
    for _ in range(TOPK_GROUPS):
        _, _, pick = _first_max(gs, gsub, N_GROUPS)
        sel = jnp.where(pick, 1.0, sel)
        gs = jnp.where(pick, NEG_INF, gs)
    masked_rows = []
    for g in range(N_GROUPS):
        xg = biased[g * GROUP_SIZE:(g + 1) * GROUP_SIZE, :]
        masked_rows.append(jnp.where(sel[g:g + 1, :] > 0.5, xg, NEG_INF))
    masked = jnp.concatenate(masked_rows, axis=0)

    eio = lax.broadcasted_iota(I32, (e, tk), 0)
    chosen = jnp.zeros((e, tk), F32)
    ids, top_s = [], []
    for _ in range(TOP_K):
        _, idx, pick = _first_max(masked, eio, e)
        ids.append(idx)
        top_s.append(jnp.sum(jnp.where(pick, s, 0.0), axis=0, keepdims=True))
        chosen = jnp.where(pick, 1.0, chosen)
        masked = jnp.where(pick, NEG_INF, masked)
    total = top_s[0]
    for ts in top_s[1:]:
        total = total + ts

    before = (lax.broadcasted_iota(I32, (tk, tk), 0) < lax.broadcasted_iota(I32, (tk, tk), 1))
    upper = jnp.where(before, 1.0, 0.0).astype(BF16)
    rank = jnp.dot(chosen.astype(BF16), upper, preferred_element_type=F32) + carry_ref[:, 0:1]
    for k in range(TOP_K):
        ids_ref[k:k + 1, :] = ids[k]
        w_ref[k:k + 1, :] = top_s[k] / total * ROUTED_SCALE
        rk = jnp.sum(jnp.where(eio == ids[k], rank, 0.0), axis=0, keepdims=True)
        rank_ref[k:k + 1, :] = rk.astype(I32)
    carry_ref[...] = carry_ref[...] + jnp.sum(chosen, axis=1, keepdims=True)
    cnt_ref[...] = carry_ref[...].astype(I32)


def _route(logits_t, bias_col):
    e, t = logits_t.shape
    tk = min(512, t)
    row8 = lambda dt: jax.ShapeDtypeStruct((TOP_K, t), dt)
    blk8 = pl.BlockSpec((TOP_K, tk), lambda i: (0, i))
    return pl.pallas_call(
        _route_kernel, grid=(t // tk,),
        in_specs=[pl.BlockSpec((e, tk), lambda i: (0, i)),
                  pl.BlockSpec((e, 1), lambda i: (0, 0))],
        out_specs=[blk8, blk8, blk8, pl.BlockSpec((e, 128), lambda i: (0, 0))],
        out_shape=[row8(I32), row8(F32), row8(I32), jax.ShapeDtypeStruct((e, 128), I32)],
        scratch_shapes=[pltpu.VMEM((e, 128), F32)],
        compiler_params=_params(1), name="route",
    )(logits_t, bias_col)


def _dispatch_kernel(pad_ref, cnt_ref, hp_ref, slot_ref, wsg_ref, wsu_ref, wsd_ref, xs_ref, sh_ref,
                     zero_ref, words_ref, sem, zsem, *, bm):
    i = pl.program_id(0)
    td = hp_ref.shape[0] // TOKEN_ROWS
    n_rows = xs_ref.shape[0] // TOKEN_ROWS
    n_blocks = n_rows // bm
    block = bm * TOKEN_ROWS

    @pl.when(i == 0)
    def _():
        zero_ref[...] = jnp.zeros_like(zero_ref)
        last_e = N_EXPERTS - 1
        used_blocks = pad_ref[last_e] // bm + (cnt_ref[last_e] + bm - 1) // bm

        def tail_copy(j):
            return pltpu.make_async_copy(zero_ref, xs_ref.at[pl.ds(pl.multiple_of(j * block, block), block)], zsem)

        def tail(j, c, wait):
            cp = tail_copy(j)
            cp.wait() if wait else cp.start()
            return c

        def pad(e, c, wait):
            row = pad_ref[e] + cnt_ref[e]
            n = (cnt_ref[e] + bm - 1) // bm * bm - cnt_ref[e]
            bit = bm // 2
            while bit:
                take = (n & bit) != 0
                size = bit * TOKEN_ROWS
                cp = pltpu.make_async_copy(
                    zero_ref.at[pl.ds(0, size)],
                    xs_ref.at[pl.ds(pl.multiple_of(row * TOKEN_ROWS, TOKEN_ROWS), size)], zsem)

                @pl.when(take)
                def _():
                    cp.wait() if wait else cp.start()

                row = row + jnp.where(take, bit, 0)
                bit //= 2
            return c

        for wait in (False, True):
            lax.fori_loop(used_blocks, n_blocks, functools.partial(tail, wait=wait), 0)
            lax.fori_loop(0, N_EXPERTS, functools.partial(pad, wait=wait), 0)

    def body(t, c):
        for k in range(TOP_K):
            slot = slot_ref[t * TOP_K + k]
            pltpu.make_async_copy(hp_ref.at[_token(t)], xs_ref.at[_token(slot)], sem).start(priority=k % 2)
        return c

    lax.fori_loop(0, td, body, 0)

    h = _load_tokens(hp_ref[...], words_ref).astype(BF16)
    sg = jnp.dot(h, wsg_ref[...], preferred_element_type=F32)
    su = jnp.dot(h, wsu_ref[...], preferred_element_type=F32)
    shared = jnp.dot((_silu(sg) * su).astype(BF16), wsd_ref[...], preferred_element_type=F32)
    sh_ref[...] = shared.astype(sh_ref.dtype)

    for k in range(TOP_K):
        pltpu.make_async_copy(hp_ref, xs_ref.at[pl.ds(0, td * TOKEN_ROWS)], sem).wait()


def _dispatch(pad_start, counts, hp, slots, wsg_b, wsu_b, wsd_b, n_rows, bm):
    t = hp.shape[0] // TOKEN_ROWS
    d, ds = wsg_b.shape
    td = min(512, t)
    whole = lambda shape: pl.BlockSpec(shape, lambda i, *_: (0, 0))
    grid_spec = pltpu.PrefetchScalarGridSpec(
        num_scalar_prefetch=2, grid=(t // td,),
        in_specs=[pl.BlockSpec((td * TOKEN_ROWS, LANES), lambda i, *_: (i, 0)),
                  pl.BlockSpec((td * TOP_K,), lambda i, *_: (i,), memory_space=pltpu.SMEM),
                  whole((d, ds)), whole((d, ds)), whole((ds, d))],
        out_specs=[pl.BlockSpec(memory_space=pl.ANY),
                   pl.BlockSpec((td, d), lambda i, *_: (i, 0))],
        scratch_shapes=[pltpu.VMEM((bm * TOKEN_ROWS, LANES), BF16), _words_scratch(td),
                        pltpu.SemaphoreType.DMA(()), pltpu.SemaphoreType.DMA(())])
    return pl.pallas_call(
        functools.partial(_dispatch_kernel, bm=bm), grid_spec=grid_spec,
        out_shape=[jax.ShapeDtypeStruct((n_rows * TOKEN_ROWS, LANES), BF16),
                   jax.ShapeDtypeStruct((t, d), BF16)],
        compiler_params=_params(1), name="dispatch",
    )(pad_start, counts, hp, slots, wsg_b, wsu_b, wsd_b)


def _expert_kernel(be_ref, first_ref, slot_ref, nxt_ref, rows_ref, nu_ref, x_ref, wg_hbm, wu_hbm, wd_hbm, o_ref,
                   words_ref, wg32, wu32, wd32, wg16, wu16, wd16, wsem):
    i = pl.program_id(0)

    def weight_copies(e, s):
        return (pltpu.make_async_copy(wg_hbm.at[e], wg32.at[s], wsem.at[s, 0]),
                pltpu.make_async_copy(wu_hbm.at[e], wu32.at[s], wsem.at[s, 1]),
                pltpu.make_async_copy(wd_hbm.at[e], wd32.at[s], wsem.at[s, 2]))

    @pl.when(i == 0)
    def _():
        for cp in weight_copies(be_ref[0], 0):
            cp.start()

    @pl.when(jnp.logical_and(first_ref[i] == 1, i < nu_ref[0]))
    def _():
        s = slot_ref[i]
        for cp in weight_copies(be_ref[i], s):
            cp.wait()

        @pl.when(nxt_ref[i] >= 0)
        def _():
            for cp in weight_copies(nxt_ref[i], 1 - s):
                cp.start()

        wg16[...] = wg32[s].astype(BF16)
        wu16[...] = wu32[s].astype(BF16)
        wd16[...] = wd32[s].astype(BF16)

    bm = x_ref.shape[0] // TOKEN_ROWS
    quarter = bm // EXPERT_BLOCK_PARTS
    used = jnp.where(i < nu_ref[0], (rows_ref[i] + quarter - 1) // quarter, 0)

    def swiglu_rows(n):
        x = _load_tokens(x_ref[pl.ds(0, n * TOKEN_ROWS), :], words_ref).astype(BF16)
        g = jnp.dot(x, wg16[...], preferred_element_type=F32)
        u = jnp.dot(x, wu16[...], preferred_element_type=F32)
        mid = (_silu(g) * u).astype(BF16)
        _store_tokens(o_ref, jnp.dot(mid, wd16[...], preferred_element_type=F32), words_ref)
        if n < bm:
            rest = (bm - n) * TOKEN_ROWS
            o_ref[pl.ds(n * TOKEN_ROWS, rest), :] = jnp.zeros((rest, LANES), o_ref.dtype)

    for parts in range(1, EXPERT_BLOCK_PARTS + 1):
        pl.when(used == parts)(functools.partial(swiglu_rows, parts * quarter))

    @pl.when(used == 0)
    def _():
        o_ref[...] = jnp.zeros_like(o_ref)


def _experts(blk_e, blk_first, blk_slot, blk_next, blk_rows, n_used, xs, w_gate, w_up, w_down, n_blocks, bm):
    _, d, de = w_gate.shape
    x_map = lambda i, be, fi, sl, nx, ro, nu: (jnp.minimum(i, nu[0] - 1), 0)
    hbm = pl.BlockSpec(memory_space=pl.ANY)
    grid_spec = pltpu.PrefetchScalarGridSpec(
        num_scalar_prefetch=6, grid=(n_blocks,),
        in_specs=[pl.BlockSpec((bm * TOKEN_ROWS, LANES), x_map), hbm, hbm, hbm],
        out_specs=pl.BlockSpec((bm * TOKEN_ROWS, LANES), lambda i, *_: (i, 0)),
        scratch_shapes=[_words_scratch(bm),
                        pltpu.VMEM((2, d, de), F32), pltpu.VMEM((2, d, de), F32), pltpu.VMEM((2, de, d), F32),
                        pltpu.VMEM((d, de), BF16), pltpu.VMEM((d, de), BF16), pltpu.VMEM((de, d), BF16),
                        pltpu.SemaphoreType.DMA((2, 3))])
    return pl.pallas_call(
        _expert_kernel, grid_spec=grid_spec,
        out_shape=jax.ShapeDtypeStruct((n_blocks * bm * TOKEN_ROWS, LANES), BF16),
        compiler_params=_params(1), name="experts",
    )(blk_e, blk_first, blk_slot, blk_next, blk_rows, n_used, xs, w_gate, w_up, w_down)


def _combine_kernel(x1_ref, sh_ref, w_ref, mod_ref, fw_ref, cur_ref, nxt_ref, ys_ref, o_ref,
                    ybuf, acc_ref, sem, *, n_tiles):
    i = pl.program_id(0)
    tc = x1_ref.shape[0]

    def issue(slot_ref, buf):
        def body(t, c):
            for k in range(TOP_K):
                slot = slot_ref[t * TOP_K + k]
                pltpu.make_async_copy(ys_ref.at[_token(slot)], ybuf.at[buf, k, _token(t)],
                                      sem.at[buf]).start(priority=k % 2)
            return c
        lax.fori_loop(0, tc, body, 0)

    @pl.when(i == 0)
    def _():
        issue(cur_ref, 0)

    @pl.when(i + 1 < n_tiles)
    def _():
        issue(nxt_ref, (i + 1) % 2)

    buf = i % 2
    for k in range(TOP_K):
        pltpu.make_async_copy(ys_ref.at[pl.ds(0, tc * TOKEN_ROWS)], ybuf.at[buf, k], sem.at[buf]).wait()

    w = w_ref[...]
    acc = [None, None]
    for k in range(TOP_K):
        words = pltpu.bitcast(ybuf[buf, k], jnp.uint32)
        for half in range(2):
            part = pltpu.unpack_elementwise(words, index=half, packed_dtype=BF16, unpacked_dtype=F32)
            term = w[:, k:k + 1] * part
            acc[half] = term if k == 0 else acc[half] + term
    halves = []
    for half in range(2):
        acc_ref[half] = acc[half]
        halves.append(jnp.concatenate(
            [acc_ref[half, pl.ds(s, tc, stride=TOKEN_WORD_ROWS), :] for s in range(TOKEN_WORD_ROWS)], axis=1))
    routed = jnp.concatenate(halves, axis=1)
    x = x1_ref[...] + mod_ref[0][5:6, :] * (routed + sh_ref[...].astype(F32))
    o_ref[...] = x * lax.rsqrt(jnp.mean(x * x, axis=-1, keepdims=True) + EPS) * fw_ref[...]


def _combine(x1, shared, w_rows, mod3, final_w, slots, ys, seq):
    t, d = x1.shape
    tc = min(256, seq)
    n_tiles = t // tc
    tiles_per_seq = seq // tc
    cur = pl.BlockSpec((tc * TOP_K,), lambda i: (i,), memory_space=pltpu.SMEM)
    nxt = pl.BlockSpec((tc * TOP_K,), lambda i: (jnp.minimum(i + 1, n_tiles - 1),), memory_space=pltpu.SMEM)
    return pl.pallas_call(
        functools.partial(_combine_kernel, n_tiles=n_tiles), grid=(n_tiles,),
        in_specs=[pl.BlockSpec((tc, d), lambda i: (i, 0)),
                  pl.BlockSpec((tc, d), lambda i: (i, 0)),
                  pl.BlockSpec((tc * TOKEN_WORD_ROWS, TOP_K), lambda i: (i, 0)),
                  pl.BlockSpec((1, 6, d), lambda i: (i // tiles_per_seq, 0, 0)),
                  pl.BlockSpec((1, d), lambda i: (0, 0)),
                  cur, nxt,
                  pl.BlockSpec(memory_space=pl.ANY)],
        out_specs=pl.BlockSpec((tc, d), lambda i: (i, 0)),
        out_shape=jax.ShapeDtypeStruct((t, d), F32),
        scratch_shapes=[pltpu.VMEM((2, TOP_K, tc * TOKEN_ROWS, LANES), BF16),
                        pltpu.VMEM((2, tc * TOKEN_WORD_ROWS, LANES), F32),
                        pltpu.SemaphoreType.DMA((2,))],
        compiler_params=_params(1), name="combine",
    )(x1, shared, w_rows, mod3, final_w, slots, slots, ys)


def _rope_tables(seq, d):
    half = d // 2
    inv = ROPE_BASE ** (-jnp.arange(half, dtype=F32) / half)
    ang = jnp.arange(seq, dtype=F32)[:, None] * inv[None, :]
    cos, sin = jnp.cos(ang), jnp.sin(ang)
    return jnp.concatenate([cos, cos], axis=1), jnp.concatenate([-sin, sin], axis=1)


def kernel(x, c, w_ada, b_ada, norm1_w, w_in, ret_decay_fwd, ret_decay_bwd, ret_norm_w, w_ret_o, conv_w,
           w_conv_o, w_out, norm2_w, router_w, router_bias, w_gate, w_up, w_down, ws_gate, ws_up, ws_down,
           final_norm_w):
    batch, seq, d = x.shape
    depth = w_ada.shape[0]
    t = batch * seq
    bm = EXPERT_BLOCK
    n_blocks = (t * TOP_K + N_EXPERTS * (bm - 1)) // bm
    n_rows = n_blocks * bm
    cos, sin = _rope_tables(seq, HEAD_DIM)
    c_pad = jnp.pad(c, ((0, BF16_SUBLANES - batch % BF16_SUBLANES), (0, 0)))
    x2 = x.reshape(t, d)

    assert depth == 1, "the final norm is fused into the last stage of a single layer"
    for l in range(depth):
        mod = _ada(c_pad, w_ada[l], b_ada[l][None, :])[:batch]
        mod3 = mod.reshape(batch, 6, d)
        proj = _inproj(x2, mod3, norm1_w[l][None, :], w_in[l].astype(BF16), seq)
        lane_bcast = lambda v: jnp.broadcast_to(v[:, None, None], (RET_HEADS, 1, HEAD_DIM))
        og = _retention(proj, cos, sin, lane_bcast(ret_decay_fwd[l]), lane_bcast(ret_decay_bwd[l]),
                        ret_norm_w[l].reshape(RET_HEADS, 1, HEAD_DIM), batch, seq)
        merged = _mix(og, proj, conv_w[l], w_ret_o[l].astype(BF16), w_conv_o[l].astype(BF16), seq)
        x1, hp, logits_t = _outproj(merged, x2, mod3, norm2_w[l][None, :], w_out[l].astype(BF16),
                                    router_w[l].T, seq)
        ids_t, w_t, rank_t, cnt = _route(logits_t, router_bias[l][:, None])

        counts = cnt[:, 0]
        nblk = (counts + bm - 1) // bm
        blk_end = jnp.cumsum(nblk)
        pad_start = ((blk_end - nblk) * bm).astype(I32)
        n_used = blk_end[-1:].astype(I32)
        blk_ids = jnp.arange(n_blocks, dtype=I32)
        blk_e = jnp.minimum(jnp.sum((blk_ids[:, None] >= blk_end[None, :]).astype(I32), axis=1),
                            N_EXPERTS - 1)

        blk_first = jnp.concatenate([jnp.ones((1,), I32), (blk_e[1:] != blk_e[:-1]).astype(I32)])
        blk_slot = (jnp.cumsum(blk_first) - 1) % 2
        after = blk_end[blk_e]
        blk_next = jnp.where(after < n_used[0], blk_e[jnp.minimum(after, n_blocks - 1)], -1).astype(I32)

        onehot = ids_t[:, :, None] == jnp.arange(N_EXPERTS, dtype=I32)
        slots_t = rank_t + jnp.sum(jnp.where(onehot, pad_start, 0), axis=-1)
        slots = slots_t.T.reshape(t * TOP_K)
        xs, shared = _dispatch(pad_start, counts, hp, slots, ws_gate[l].astype(BF16), ws_up[l].astype(BF16),
                               ws_down[l].astype(BF16), n_rows, bm)
        blk_rows = jnp.clip(counts[blk_e] - (blk_ids * bm - pad_start[blk_e]), 0, bm).astype(I32)
        ys = _experts(blk_e, blk_first, blk_slot.astype(I32), blk_next, blk_rows, n_used, xs,
                      w_gate[l], w_up[l], w_down[l], n_blocks, bm)
        w_rows = jnp.repeat(w_t.T, TOKEN_WORD_ROWS, axis=0)
        x2 = _combine(x1, shared, w_rows, mod3, final_norm_w[None, :], slots, ys, seq)
    return x2.reshape(batch, seq, d)
```

```python
import functools

import jax
import jax.numpy as jnp
from jax import lax
from jax.experimental import pallas as pl
from jax.experimental.pallas import tpu as pltpu

F32 = jnp.float32
BF16 = jnp.bfloat16
I32 = jnp.int32

EPS = 1e-6
RET_HEADS = 8
HEAD_DIM = 128
ROPE_BASE = 10000.0
N_EXPERTS = 64
TOP_K = 8
N_GROUPS = 8
TOPK_GROUPS = 4
GROUP_SIZE = N_EXPERTS // N_GROUPS
ROUTED_SCALE = 2.5

V7X_VMEM_BYTES = 64 * 1024 * 1024
VMEM_LIMIT = V7X_VMEM_BYTES - 8 * 1024 * 1024
BF16_SUBLANES = 16

RET_CHUNK = 256
EXPERT_BLOCK = 512
EXPERT_BLOCK_PARTS = 4
COMBINE_TILE = 256
NEG_INF = float("-inf")


def _params(n_axes):
    return pltpu.CompilerParams(dimension_semantics=("arbitrary",) * n_axes,
                                vmem_limit_bytes=VMEM_LIMIT)


def _sigmoid(x):
    return 1.0 / (1.0 + jnp.exp(-x))


def _silu(x):
    return x * _sigmoid(x)


LANES = 128
TOKEN_WORD_ROWS = 8
TOKEN_ROWS = 2 * TOKEN_WORD_ROWS


def _store_tokens(ref, x, words_ref):
    n, m = x.shape[0], x.shape[1] // 2
    assert m == TOKEN_WORD_ROWS * LANES
    packed = pltpu.pack_elementwise([x[:, :m], x[:, m:]], packed_dtype=BF16)
    for s in range(TOKEN_WORD_ROWS):
        words_ref[pl.ds(s, n, stride=TOKEN_WORD_ROWS), :] = packed[:, s * LANES:(s + 1) * LANES]
    ref[pl.ds(0, n * TOKEN_ROWS), :] = pltpu.bitcast(words_ref[pl.ds(0, n * TOKEN_WORD_ROWS), :], BF16)


def _load_tokens(tiles, words_ref):
    n = tiles.shape[0] // TOKEN_ROWS
    words_ref[pl.ds(0, n * TOKEN_WORD_ROWS), :] = pltpu.bitcast(tiles, jnp.uint32)
    p = jnp.concatenate([words_ref[pl.ds(s, n, stride=TOKEN_WORD_ROWS), :]
                         for s in range(TOKEN_WORD_ROWS)], axis=1)
    a = pltpu.unpack_elementwise(p, index=0, packed_dtype=BF16, unpacked_dtype=F32)
    b = pltpu.unpack_elementwise(p, index=1, packed_dtype=BF16, unpacked_dtype=F32)
    return jnp.concatenate([a, b], axis=1)


def _token(row):
    return pl.ds(pl.multiple_of(row * TOKEN_ROWS, TOKEN_ROWS), TOKEN_ROWS)


def _words_scratch(n_tokens):
    return pltpu.VMEM((n_tokens * TOKEN_WORD_ROWS, LANES), jnp.uint32)


def _ada_kernel(c_ref, w_ref, b_ref, o_ref):
    s = _silu(c_ref[...]).astype(BF16)
    o_ref[...] = jnp.dot(s, w_ref[...].astype(BF16), preferred_element_type=F32) + b_ref[...]


def _ada(c_pad, w_ada, b_ada):
    m, d = c_pad.shape
    n = w_ada.shape[1]
    tn = 1024
    return pl.pallas_call(
        _ada_kernel, grid=(n // tn,),
        in_specs=[pl.BlockSpec((m, d), lambda j: (0, 0)),
                  pl.BlockSpec((d, tn), lambda j: (0, j)),
                  pl.BlockSpec((1, tn), lambda j: (0, j))],
        out_specs=pl.BlockSpec((m, tn), lambda j: (0, j)),
        out_shape=jax.ShapeDtypeStruct((m, n), F32),
        compiler_params=_params(1), name="ada")(c_pad, w_ada, b_ada)


def _inproj_kernel(x_ref, mod_ref, nw_ref, w_ref, o_ref, h_ref):
    @pl.when(pl.program_id(1) == 0)
    def _():
        x = x_ref[...]
        y = x * lax.rsqrt(jnp.mean(x * x, axis=-1, keepdims=True) + EPS) * nw_ref[...]
        m = mod_ref[0]
        h_ref[...] = (y * (1.0 + m[1:2, :]) + m[0:1, :]).astype(BF16)

    o_ref[...] = jnp.dot(h_ref[...], w_ref[...], preferred_element_type=F32).astype(o_ref.dtype)


def _inproj(x2, mod3, norm_w, w_in_b, seq):
    t, d = x2.shape
    n = w_in_b.shape[1]
    tm, tn = min(1024, seq), 1024
    tiles_per_seq = seq // tm
    return pl.pallas_call(
        _inproj_kernel, grid=(t // tm, n // tn),
        in_specs=[pl.BlockSpec((tm, d), lambda i, j: (i, 0)),
                  pl.BlockSpec((1, 6, d), lambda i, j: (i // tiles_per_seq, 0, 0)),
                  pl.BlockSpec((1, d), lambda i, j: (0, 0)),
                  pl.BlockSpec((d, tn), lambda i, j: (0, j))],
        out_specs=pl.BlockSpec((tm, tn), lambda i, j: (i, j)),
        out_shape=jax.ShapeDtypeStruct((t, n), BF16),
        scratch_shapes=[pltpu.VMEM((tm, d), BF16)],
        compiler_params=_params(2), name="inproj")(x2, mod3, norm_w, w_in_b)


def _log_sigmoid(x):
    return jnp.minimum(x, 0.0) - jnp.log1p(jnp.exp(-jnp.abs(x)))


def _ret_kernel(q_ref, k_ref, v_ref, g_ref, cos_ref, sin_ref, df_ref, db_ref, nw_ref, o_ref,
                mask_ref, qf_ref, qb_ref, kf_ref, kb_ref, dec_ref, qr_ref, kr_ref, acc_ref, *, chunk):
    seq, d = q_ref.shape
    n = seq // chunk
    c = chunk

    @pl.when(pl.program_id(1) == 0)
    def _():
        lgf = _log_sigmoid(df_ref[0])
        lgb = _log_sigmoid(db_ref[0])
        ii = lax.broadcasted_iota(I32, (c, c), 0)
        jj = lax.broadcasted_iota(I32, (c, c), 1)
        diff = (ii - jj).astype(F32)
        lgf_c = jnp.concatenate([lgf] * (c // d), axis=1)
        lgb_c = jnp.concatenate([lgb] * (c // d), axis=1)
        mask_ref[...] = jnp.where(diff >= 0.0,
                                  jnp.exp(lgf_c * jnp.maximum(diff, 0.0)),
                                  jnp.exp(lgb_c * jnp.maximum(-diff, 0.0)))
        pos = lax.broadcasted_iota(I32, (c, d), 0).astype(F32)
        qf_ref[...] = jnp.exp(lgf * (pos + 1.0))
        qb_ref[...] = jnp.exp(lgb * (c - pos))
        kf_ref[...] = jnp.exp(lgf * (c - 1.0 - pos))
        kb_ref[...] = jnp.exp(lgb * pos)
        dec_ref[0:1, :] = jnp.exp(lgf * c)
        dec_ref[1:2, :] = jnp.exp(lgb * c)

    scale = d ** -0.5
    nt = (((1,), (1,)), ((), ()))
    sls = [pl.ds(i * c, c) for i in range(n)]

    for sl in sls:
        cs = cos_ref[sl, :]
        sn = sin_ref[sl, :]
        q = q_ref[sl, :].astype(F32)
        k = k_ref[sl, :].astype(F32)
        qr_ref[sl, :] = q * cs + pltpu.roll(q, d // 2, 1) * sn
        kr_ref[sl, :] = (k * cs + pltpu.roll(k, d // 2, 1) * sn) * scale

    for sl in sls:
        s = lax.dot_general(qr_ref[sl, :].astype(BF16), kr_ref[sl, :].astype(BF16), nt,
                            preferred_element_type=F32)
        p = (s * mask_ref[...]).astype(BF16)
        acc_ref[sl, :] = jnp.dot(p, v_ref[sl, :], preferred_element_type=F32)

    def kv_state(sl, kw_ref):
        kw = (kr_ref[sl, :] * kw_ref[...]).T.astype(BF16)
        return jnp.dot(kw, v_ref[sl, :], preferred_element_type=F32)

    def scan(order, qw_ref, kw_ref, dec):
        st = jnp.zeros((d, d), F32)
        for idx, ci in enumerate(order):
            sl = sls[ci]
            if idx > 0:
                qw = (qr_ref[sl, :] * qw_ref[...]).astype(BF16)
                acc_ref[sl, :] += jnp.dot(qw, st.astype(BF16), preferred_element_type=F32)
            if idx < n - 1:
                st = dec * st + kv_state(sl, kw_ref)

    scan(list(range(n)), qf_ref, kf_ref, dec_ref[0:1, :])
    scan(list(range(n - 1, -1, -1)), qb_ref, kb_ref, dec_ref[1:2, :])

    nw = nw_ref[0]
    for sl in sls:
        o = acc_ref[sl, :]
        on = o * lax.rsqrt(jnp.mean(o * o, axis=-1, keepdims=True) + EPS) * nw
        g = g_ref[sl, :].astype(F32)
        o_ref[sl, :] = (_silu(g) * on).astype(o_ref.dtype)


def _retention(proj, cos, sin, dec_f, dec_b, ret_norm_w, batch, seq):
    h, d = RET_HEADS, HEAD_DIM
    c = min(RET_CHUNK, seq)
    col = lambda off: pl.BlockSpec((seq, d), lambda hh, b: (b, off + hh))
    per_head = pl.BlockSpec((1, 1, d), lambda hh, b: (hh, 0, 0))
    table = pl.BlockSpec((seq, d), lambda hh, b: (0, 0))
    vm = lambda shape, dt=F32: pltpu.VMEM(shape, dt)
    return pl.pallas_call(
        functools.partial(_ret_kernel, chunk=c), grid=(h, batch),
        in_specs=[col(0), col(h), col(2 * h), col(3 * h), table, table, per_head, per_head, per_head],
        out_specs=pl.BlockSpec((seq, d), lambda hh, b: (b, hh)),
        out_shape=jax.ShapeDtypeStruct((batch * seq, h * d), BF16),
        scratch_shapes=[vm((c, c)), vm((c, d)), vm((c, d)), vm((c, d)), vm((c, d)), vm((8, d)),
                        vm((seq, d)), vm((seq, d)), vm((seq, d))],
        compiler_params=_params(2), name="retention",
    )(proj, proj, proj, proj, cos, sin, dec_f, dec_b, ret_norm_w)


def _mix_kernel(og_ref, cb_ref, cc_ref, cu_ref, ccp_ref, cup_ref, ccn_ref, cun_ref,
                gr0_ref, gr1_ref, gc0_ref, gc1_ref, cw_ref, wr_ref, wc_ref, o_ref, *, tiles_per_seq):
    i = pl.program_id(0)
    tm, dr = cc_ref.shape
    u = cc_ref[...].astype(F32) * cu_ref[...].astype(F32)
    pos = i % tiles_per_seq
    last = BF16_SUBLANES - 1
    u_before = ccp_ref[last:last + 1, :].astype(F32) * cup_ref[last:last + 1, :].astype(F32)
    u_before = jnp.where(pos == 0, 0.0, u_before)
    u_after = ccn_ref[0:1, :].astype(F32) * cun_ref[0:1, :].astype(F32)
    u_after = jnp.where(pos == tiles_per_seq - 1, 0.0, u_after)
    row = lax.broadcasted_iota(I32, u.shape, 0)
    u_prev = jnp.where(row == 0, u_before, pltpu.roll(u, 1, 0))
    u_next = jnp.where(row == tm - 1, u_after, pltpu.roll(u, tm - 1, 0))
    cw = cw_ref[...]
    y = cw[0:1, :] * u_prev + cw[1:2, :] * u + cw[2:3, :] * u_next
    z = (cb_ref[...].astype(F32) * y).astype(BF16)

    yr = jnp.dot(og_ref[...], wr_ref[...], preferred_element_type=F32)
    yc = jnp.dot(z, wc_ref[...], preferred_element_type=F32)
    for lo, gr_ref, gc_ref in ((0, gr0_ref, gc0_ref), (dr, gr1_ref, gc1_ref)):
        merged = (_sigmoid(gr_ref[...].astype(F32)) * yr[:, lo:lo + dr]
                  + _sigmoid(gc_ref[...].astype(F32)) * yc[:, lo:lo + dr])
        o_ref[:, lo:lo + dr] = merged.astype(o_ref.dtype)


def _mix(og, proj, conv_w, w_ret_o_b, w_conv_o_b, seq):
    t, dr = og.shape
    d = w_ret_o_b.shape[1]
    assert d == 2 * dr
    tm = min(512, seq)
    tiles_per_seq = seq // tm
    hb = tm // BF16_SUBLANES
    n_hblk = t // BF16_SUBLANES
    wide = lambda off: pl.BlockSpec((tm, dr), lambda i: (i, off))
    before = lambda off: pl.BlockSpec((BF16_SUBLANES, dr), lambda i: (jnp.maximum(i * hb - 1, 0), off))
    after = lambda off: pl.BlockSpec((BF16_SUBLANES, dr), lambda i: (jnp.minimum((i + 1) * hb, n_hblk - 1), off))
    whole = lambda shape: pl.BlockSpec(shape, lambda i: (0, 0))
    return pl.pallas_call(
        functools.partial(_mix_kernel, tiles_per_seq=tiles_per_seq), grid=(t // tm,),
        in_specs=[wide(0), wide(4), wide(5), wide(6), before(5), before(6), after(5), after(6),
                  wide(7), wide(8), wide(9), wide(10),
                  whole((3, dr)), whole((dr, d)), whole((dr, d))],
        out_specs=pl.BlockSpec((tm, d), lambda i: (i, 0)),
        out_shape=jax.ShapeDtypeStruct((t, d), BF16),
        compiler_params=_params(1), name="mix",
    )(og, proj, proj, proj, proj, proj, proj, proj, proj, proj, proj, proj, conv_w, w_ret_o_b, w_conv_o_b)


def _outproj_kernel(m_ref, x_ref, mod_ref, nw_ref, w_ref, rwt_ref, x1_ref, hp_ref, lg_ref, words_ref):
    y = jnp.dot(m_ref[...], w_ref[...], preferred_element_type=F32)
    m = mod_ref[0]
    x1 = x_ref[...] + m[2:3, :] * y
    x1_ref[...] = x1
    hn = x1 * lax.rsqrt(jnp.mean(x1 * x1, axis=-1, keepdims=True) + EPS) * nw_ref[...]
    h = hn * (1.0 + m[4:5, :]) + m[3:4, :]
    _store_tokens(hp_ref, h, words_ref)
    h_hi = h.astype(BF16)
    h_lo = (h - h_hi.astype(F32)).astype(BF16)
    rw = rwt_ref[...]
    r_hi = rw.astype(BF16)
    r_lo = (rw - r_hi.astype(F32)).astype(BF16)
    nt = (((1,), (1,)), ((), ()))
    dot = functools.partial(lax.dot_general, dimension_numbers=nt, preferred_element_type=F32)
    lg_ref[...] = dot(r_hi, h_hi) + dot(r_hi, h_lo) + dot(r_lo, h_hi)


def _outproj(merged, x2, mod3, norm_w, w_out_b, router_wt, seq):
    t, d = x2.shape
    e = router_wt.shape[0]
    tm = min(512, seq)
    tiles_per_seq = seq // tm
    return pl.pallas_call(
        _outproj_kernel, grid=(t // tm,),
        in_specs=[pl.BlockSpec((tm, d), lambda i: (i, 0)),
                  pl.BlockSpec((tm, d), lambda i: (i, 0)),
                  pl.BlockSpec((1, 6, d), lambda i: (i // tiles_per_seq, 0, 0)),
                  pl.BlockSpec((1, d), lambda i: (0, 0)),
                  pl.BlockSpec((d, d), lambda i: (0, 0)),
                  pl.BlockSpec((e, d), lambda i: (0, 0))],
        out_specs=[pl.BlockSpec((tm, d), lambda i: (i, 0)),
                   pl.BlockSpec((TOKEN_ROWS * tm, LANES), lambda i: (i, 0)),
                   pl.BlockSpec((e, tm), lambda i: (0, i))],
        out_shape=[jax.ShapeDtypeStruct((t, d), F32),
                   jax.ShapeDtypeStruct((TOKEN_ROWS * t, LANES), BF16),
                   jax.ShapeDtypeStruct((e, t), F32)],
        scratch_shapes=[_words_scratch(tm)],
        compiler_params=_params(1), name="outproj",
    )(merged, x2, mod3, norm_w, w_out_b, router_wt)


def _first_max(x, iota, sentinel):
    m = jnp.max(x, axis=0, keepdims=True)
    idx = jnp.min(jnp.where(x == m, iota, sentinel), axis=0, keepdims=True)
    return m, idx, iota == idx


def _route_kernel(lg_ref, bias_ref, ids_ref, w_ref, rank_ref, lidx_ref, cnt_ref, tcarry_ref, tcnt_ref, carry_ref):
    i = pl.program_id(0)
    e, tk = lg_ref.shape

    @pl.when(i == 0)
    def _():
        carry_ref[...] = jnp.zeros_like(carry_ref)

    s = _sigmoid(lg_ref[...])
    biased = s + bias_ref[...]
    sub = lax.broadcasted_iota(I32, (GROUP_SIZE, tk), 0)
    group_rows = []
    for g in range(N_GROUPS):
        xg = biased[g * GROUP_SIZE:(g + 1) * GROUP_SIZE, :]
        m1, _, pick = _first_max(xg, sub, GROUP_SIZE)
        m2 = jnp.max(jnp.where(pick, NEG_INF, xg), axis=0, keepdims=True)
        group_rows.append(m1 + m2)
    gs = jnp.concatenate(group_rows, axis=0)
    gsub = lax.broadcasted_iota(I32, (N_GROUPS, tk), 0)
    sel = jnp.zeros((N_GROUPS, tk), F32)
    for _ in range(TOPK_GROUPS):
        _, _, pick = _first_max(gs, gsub, N_GROUPS)
        sel = jnp.where(pick, 1.0, sel)
        gs = jnp.where(pick, NEG_INF, gs)
    masked_rows = []
    for g in range(N_GROUPS):
        xg = biased[g * GROUP_SIZE:(g + 1) * GROUP_SIZE, :]
        masked_rows.append(jnp.where(sel[g:g + 1, :] > 0.5, xg, NEG_INF))
    masked = jnp.concatenate(masked_rows, axis=0)

    eio = lax.broadcasted_iota(I32, (e, tk), 0)
    chosen = jnp.zeros((e, tk), F32)
    ids, top_s = [], []
    for _ in range(TOP_K):
        _, idx, pick = _first_max(masked, eio, e)
        ids.append(idx)
        top_s.append(jnp.sum(jnp.where(pick, s, 0.0), axis=0, keepdims=True))
        chosen = jnp.where(pick, 1.0, chosen)
        masked = jnp.where(pick, NEG_INF, masked)
    total = top_s[0]
    for ts in top_s[1:]:
        total = total + ts

    before = (lax.broadcasted_iota(I32, (tk, tk), 0) < lax.broadcasted_iota(I32, (tk, tk), 1))
    upper = jnp.where(before, 1.0, 0.0).astype(BF16)
    local = jnp.dot(chosen.astype(BF16), upper, preferred_element_type=F32)
    carry = carry_ref[...]
    tile_cnt = jnp.broadcast_to(jnp.sum(chosen, axis=1, keepdims=True), carry.shape)
    below = (lax.broadcasted_iota(I32, (e, e), 1) < lax.broadcasted_iota(I32, (e, e), 0))
    lower = jnp.where(below, 1.0, 0.0).astype(BF16)
    first = jnp.dot(lower, tile_cnt.astype(BF16), preferred_element_type=F32)
    rank = local + carry[:, 0:1]
    place = local + first[:, 0:1]
    for k in range(TOP_K):
        ids_ref[k:k + 1, :] = ids[k]
        w_ref[k:k + 1, :] = top_s[k] / total * ROUTED_SCALE
        mine = eio == ids[k]
        rank_ref[k:k + 1, :] = jnp.sum(jnp.where(mine, rank, 0.0), axis=0, keepdims=True).astype(I32)
        lidx_ref[k:k + 1, :] = jnp.sum(jnp.where(mine, place, 0.0), axis=0, keepdims=True).astype(I32)
    tcarry_ref[0] = carry.astype(I32)
    tcnt_ref[0] = tile_cnt.astype(I32)
    carry_ref[...] = carry + tile_cnt
    cnt_ref[...] = carry_ref[...].astype(I32)


def _route(logits_t, bias_col, tk):
    e, t = logits_t.shape
    n_tiles = t // tk
    row8 = lambda dt: jax.ShapeDtypeStruct((TOP_K, t), dt)
    blk8 = pl.BlockSpec((TOP_K, tk), lambda i: (0, i))
    per_tile = pl.BlockSpec((1, e, LANES), lambda i: (i, 0, 0))
    return pl.pallas_call(
        _route_kernel, grid=(n_tiles,),
        in_specs=[pl.BlockSpec((e, tk), lambda i: (0, i)),
                  pl.BlockSpec((e, 1), lambda i: (0, 0))],
        out_specs=[blk8, blk8, blk8, blk8, pl.BlockSpec((e, LANES), lambda i: (0, 0)), per_tile, per_tile],
        out_shape=[row8(I32), row8(F32), row8(I32), row8(I32), jax.ShapeDtypeStruct((e, LANES), I32),
                   jax.ShapeDtypeStruct((n_tiles, e, LANES), I32), jax.ShapeDtypeStruct((n_tiles, e, LANES), I32)],
        scratch_shapes=[pltpu.VMEM((e, LANES), F32)],
        compiler_params=_params(1), name="route",
    )(logits_t, bias_col)


def _dispatch_kernel(pad_ref, cnt_ref, hp_ref, slot_ref, wsg_ref, wsu_ref, wsd_ref, xs_ref, sh_ref,
                     zero_ref, words_ref, sem, zsem, *, bm):
    i = pl.program_id(0)
    td = hp_ref.shape[0] // TOKEN_ROWS
    n_rows = xs_ref.shape[0] // TOKEN_ROWS
    n_blocks = n_rows // bm
    block = bm * TOKEN_ROWS

    @pl.when(i == 0)
    def _():
        zero_ref[...] = jnp.zeros_like(zero_ref)
        last_e = N_EXPERTS - 1
        used_blocks = pad_ref[last_e] // bm + (cnt_ref[last_e] + bm - 1) // bm

        def tail_copy(j):
            return pltpu.make_async_copy(zero_ref, xs_ref.at[pl.ds(pl.multiple_of(j * block, block), block)], zsem)

        def tail(j, c, wait):
            cp = tail_copy(j)
            cp.wait() if wait else cp.start()
            return c

        def pad(e, c, wait):
            row = pad_ref[e] + cnt_ref[e]
            n = (cnt_ref[e] + bm - 1) // bm * bm - cnt_ref[e]
            bit = bm // 2
            while bit:
                take = (n & bit) != 0
                size = bit * TOKEN_ROWS
                cp = pltpu.make_async_copy(
                    zero_ref.at[pl.ds(0, size)],
                    xs_ref.at[pl.ds(pl.multiple_of(row * TOKEN_ROWS, TOKEN_ROWS), size)], zsem)

                @pl.when(take)
                def _():
                    cp.wait() if wait else cp.start()

                row = row + jnp.where(take, bit, 0)
                bit //= 2
            return c

        for wait in (False, True):
            lax.fori_loop(used_blocks, n_blocks, functools.partial(tail, wait=wait), 0)
            lax.fori_loop(0, N_EXPERTS, functools.partial(pad, wait=wait), 0)

    def body(t, c):
        for k in range(TOP_K):
            slot = slot_ref[t * TOP_K + k]
            pltpu.make_async_copy(hp_ref.at[_token(t)], xs_ref.at[_token(slot)], sem).start(priority=k % 2)
        return c

    lax.fori_loop(0, td, body, 0)

    h = _load_tokens(hp_ref[...], words_ref).astype(BF16)
    sg = jnp.dot(h, wsg_ref[...], preferred_element_type=F32)
    su = jnp.dot(h, wsu_ref[...], preferred_element_type=F32)
    shared = jnp.dot((_silu(sg) * su).astype(BF16), wsd_ref[...], preferred_element_type=F32)
    sh_ref[...] = shared.astype(sh_ref.dtype)

    for k in range(TOP_K):
        pltpu.make_async_copy(hp_ref, xs_ref.at[pl.ds(0, td * TOKEN_ROWS)], sem).wait()


def _dispatch(pad_start, counts, hp, slots, wsg_b, wsu_b, wsd_b, n_rows, bm):
    t = hp.shape[0] // TOKEN_ROWS
    d, ds = wsg_b.shape
    td = min(512, t)
    whole = lambda shape: pl.BlockSpec(shape, lambda i, *_: (0, 0))
    grid_spec = pltpu.PrefetchScalarGridSpec(
        num_scalar_prefetch=2, grid=(t // td,),
        in_specs=[pl.BlockSpec((td * TOKEN_ROWS, LANES), lambda i, *_: (i, 0)),
                  pl.BlockSpec((td * TOP_K,), lambda i, *_: (i,), memory_space=pltpu.SMEM),
                  whole((d, ds)), whole((d, ds)), whole((ds, d))],
        out_specs=[pl.BlockSpec(memory_space=pl.ANY),
                   pl.BlockSpec((td, d), lambda i, *_: (i, 0))],
        scratch_shapes=[pltpu.VMEM((bm * TOKEN_ROWS, LANES), BF16), _words_scratch(td),
                        pltpu.SemaphoreType.DMA(()), pltpu.SemaphoreType.DMA(())])
    return pl.pallas_call(
        functools.partial(_dispatch_kernel, bm=bm), grid_spec=grid_spec,
        out_shape=[jax.ShapeDtypeStruct((n_rows * TOKEN_ROWS, LANES), BF16),
                   jax.ShapeDtypeStruct((t, d), BF16)],
        compiler_params=_params(1), name="dispatch",
    )(pad_start, counts, hp, slots, wsg_b, wsu_b, wsd_b)


def _expert_kernel(be_ref, first_ref, slot_ref, nxt_ref, rows_ref, nu_ref, x_ref, wg_hbm, wu_hbm, wd_hbm, o_ref,
                   words_ref, wg32, wu32, wd32, wg16, wu16, wd16, wsem):
    i = pl.program_id(0)

    def weight_copies(e, s):
        return (pltpu.make_async_copy(wg_hbm.at[e], wg32.at[s], wsem.at[s, 0]),
                pltpu.make_async_copy(wu_hbm.at[e], wu32.at[s], wsem.at[s, 1]),
                pltpu.make_async_copy(wd_hbm.at[e], wd32.at[s], wsem.at[s, 2]))

    @pl.when(i == 0)
    def _():
        for cp in weight_copies(be_ref[0], 0):
            cp.start()

    @pl.when(jnp.logical_and(first_ref[i] == 1, i < nu_ref[0]))
    def _():
        s = slot_ref[i]
        for cp in weight_copies(be_ref[i], s):
            cp.wait()

        @pl.when(nxt_ref[i] >= 0)
        def _():
            for cp in weight_copies(nxt_ref[i], 1 - s):
                cp.start()

        wg16[...] = wg32[s].astype(BF16)
        wu16[...] = wu32[s].astype(BF16)
        wd16[...] = wd32[s].astype(BF16)

    bm = x_ref.shape[0] // TOKEN_ROWS
    quarter = bm // EXPERT_BLOCK_PARTS
    used = jnp.where(i < nu_ref[0], (rows_ref[i] + quarter - 1) // quarter, 0)

    def swiglu_rows(n):
        x = _load_tokens(x_ref[pl.ds(0, n * TOKEN_ROWS), :], words_ref).astype(BF16)
        g = jnp.dot(x, wg16[...], preferred_element_type=F32)
        u = jnp.dot(x, wu16[...], preferred_element_type=F32)
        mid = (_silu(g) * u).astype(BF16)
        _store_tokens(o_ref, jnp.dot(mid, wd16[...], preferred_element_type=F32), words_ref)
        if n < bm:
            rest = (bm - n) * TOKEN_ROWS
            o_ref[pl.ds(n * TOKEN_ROWS, rest), :] = jnp.zeros((rest, LANES), o_ref.dtype)

    for parts in range(1, EXPERT_BLOCK_PARTS + 1):
        pl.when(used == parts)(functools.partial(swiglu_rows, parts * quarter))

    @pl.when(used == 0)
    def _():
        o_ref[...] = jnp.zeros_like(o_ref)


def _experts(blk_e, blk_first, blk_slot, blk_next, blk_rows, n_used, xs, w_gate, w_up, w_down, n_blocks, bm):
    _, d, de = w_gate.shape
    x_map = lambda i, be, fi, sl, nx, ro, nu: (jnp.minimum(i, nu[0] - 1), 0)
    hbm = pl.BlockSpec(memory_space=pl.ANY)
    grid_spec = pltpu.PrefetchScalarGridSpec(
        num_scalar_prefetch=6, grid=(n_blocks,),
        in_specs=[pl.BlockSpec((bm * TOKEN_ROWS, LANES), x_map), hbm, hbm, hbm],
        out_specs=pl.BlockSpec((bm * TOKEN_ROWS, LANES), lambda i, *_: (i, 0)),
        scratch_shapes=[_words_scratch(bm),
                        pltpu.VMEM((2, d, de), F32), pltpu.VMEM((2, d, de), F32), pltpu.VMEM((2, de, d), F32),
                        pltpu.VMEM((d, de), BF16), pltpu.VMEM((d, de), BF16), pltpu.VMEM((de, d), BF16),
                        pltpu.SemaphoreType.DMA((2, 3))])
    return pl.pallas_call(
        _expert_kernel, grid_spec=grid_spec,
        out_shape=jax.ShapeDtypeStruct((n_blocks * bm * TOKEN_ROWS, LANES), BF16),
        compiler_params=_params(1), name="experts",
    )(blk_e, blk_first, blk_slot, blk_next, blk_rows, n_used, xs, w_gate, w_up, w_down)


def _combine_kernel(gstart_ref, tcnt_ref, x1_ref, sh_ref, lidx_ref, w_ref, mod_ref, fw_ref, ys_ref, o_ref,
                    stage, acc_ref, sem, *, n_tiles):
    i = pl.program_id(0)
    tc = x1_ref.shape[0]
    n_e = N_EXPERTS

    def fetch(tile, buf):
        def per_expert(e, off):
            count = tcnt_ref[tile * n_e + e]
            row = gstart_ref[tile * n_e + e]
            bit = tc
            while bit:
                take = (count & bit) != 0
                size = bit * TOKEN_ROWS
                cp = pltpu.make_async_copy(
                    ys_ref.at[pl.ds(pl.multiple_of(row * TOKEN_ROWS, TOKEN_ROWS), size)],
                    stage.at[buf, pl.ds(pl.multiple_of(off * TOKEN_ROWS, TOKEN_ROWS), size)], sem.at[buf])
                pl.when(take)(cp.start)
                step = jnp.where(take, bit, 0)
                row, off = row + step, off + step
                bit //= 2
            return off
        lax.fori_loop(0, n_e, per_expert, 0)

    @pl.when(i == 0)
    def _():
        fetch(0, 0)

    @pl.when(i + 1 < n_tiles)
    def _():
        fetch(i + 1, (i + 1) % 2)

    buf = i % 2
    pltpu.make_async_copy(ys_ref.at[pl.ds(0, tc * TOP_K * TOKEN_ROWS)], stage.at[buf], sem.at[buf]).wait()

    def per_token(t, c):
        lo = hi = None
        for k in range(TOP_K):
            place = lidx_ref[t * TOP_K + k]
            wk = w_ref[t * TOP_K + k]
            words = pltpu.bitcast(stage[buf, _token(place), :], jnp.uint32)
            a = wk * pltpu.unpack_elementwise(words, index=0, packed_dtype=BF16, unpacked_dtype=F32)
            b = wk * pltpu.unpack_elementwise(words, index=1, packed_dtype=BF16, unpacked_dtype=F32)
            lo, hi = (a, b) if k == 0 else (lo + a, hi + b)
        rows = pl.ds(pl.multiple_of(t * TOKEN_WORD_ROWS, TOKEN_WORD_ROWS), TOKEN_WORD_ROWS)
        acc_ref[0, rows, :] = lo
        acc_ref[1, rows, :] = hi
        return c

    lax.fori_loop(0, tc, per_token, 0)

    halves = [jnp.concatenate([acc_ref[half, pl.ds(s, tc, stride=TOKEN_WORD_ROWS), :]
                               for s in range(TOKEN_WORD_ROWS)], axis=1) for half in range(2)]
    routed = jnp.concatenate(halves, axis=1)
    x = x1_ref[...] + mod_ref[0][5:6, :] * (routed + sh_ref[...].astype(F32))
    o_ref[...] = x * lax.rsqrt(jnp.mean(x * x, axis=-1, keepdims=True) + EPS) * fw_ref[...]


def _combine(gstart, tcnt, x1, shared, lidx, w_flat, mod3, final_w, ys, seq, tc):
    t, d = x1.shape
    n_tiles = t // tc
    tiles_per_seq = seq // tc
    per_tile = lambda: pl.BlockSpec((tc * TOP_K,), lambda i, *_: (i,), memory_space=pltpu.SMEM)
    grid_spec = pltpu.PrefetchScalarGridSpec(
        num_scalar_prefetch=2, grid=(n_tiles,),
        in_specs=[pl.BlockSpec((tc, d), lambda i, *_: (i, 0)),
                  pl.BlockSpec((tc, d), lambda i, *_: (i, 0)),
                  per_tile(), per_tile(),
                  pl.BlockSpec((1, 6, d), lambda i, *_: (i // tiles_per_seq, 0, 0)),
                  pl.BlockSpec((1, d), lambda i, *_: (0, 0)),
                  pl.BlockSpec(memory_space=pl.ANY)],
        out_specs=pl.BlockSpec((tc, d), lambda i, *_: (i, 0)),
        scratch_shapes=[pltpu.VMEM((2, tc * TOP_K * TOKEN_ROWS, LANES), BF16),
                        pltpu.VMEM((2, tc * TOKEN_WORD_ROWS, LANES), F32),
                        pltpu.SemaphoreType.DMA((2,))])
    return pl.pallas_call(
        functools.partial(_combine_kernel, n_tiles=n_tiles), grid_spec=grid_spec,
        out_shape=jax.ShapeDtypeStruct((t, d), F32),
        compiler_params=_params(1), name="combine",
    )(gstart, tcnt, x1, shared, lidx, w_flat, mod3, final_w, ys)


def _rope_tables(seq, d):
    half = d // 2
    inv = ROPE_BASE ** (-jnp.arange(half, dtype=F32) / half)
    ang = jnp.arange(seq, dtype=F32)[:, None] * inv[None, :]
    cos, sin = jnp.cos(ang), jnp.sin(ang)
    return jnp.concatenate([cos, cos], axis=1), jnp.concatenate([-sin, sin], axis=1)


def kernel(x, c, w_ada, b_ada, norm1_w, w_in, ret_decay_fwd, ret_decay_bwd, ret_norm_w, w_ret_o, conv_w,
           w_conv_o, w_out, norm2_w, router_w, router_bias, w_gate, w_up, w_down, ws_gate, ws_up, ws_down,
           final_norm_w):
    batch, seq, d = x.shape
    depth = w_ada.shape[0]
    t = batch * seq
    bm = EXPERT_BLOCK
    n_blocks = (t * TOP_K + N_EXPERTS * (bm - 1)) // bm
    n_rows = n_blocks * bm
    cos, sin = _rope_tables(seq, HEAD_DIM)
    c_pad = jnp.pad(c, ((0, BF16_SUBLANES - batch % BF16_SUBLANES), (0, 0)))
    x2 = x.reshape(t, d)

    assert depth == 1, "the final norm is fused into the last stage of a single layer"
    for l in range(depth):
        mod = _ada(c_pad, w_ada[l], b_ada[l][None, :])[:batch]
        mod3 = mod.reshape(batch, 6, d)
        proj = _inproj(x2, mod3, norm1_w[l][None, :], w_in[l].astype(BF16), seq)
        lane_bcast = lambda v: jnp.broadcast_to(v[:, None, None], (RET_HEADS, 1, HEAD_DIM))
        og = _retention(proj, cos, sin, lane_bcast(ret_decay_fwd[l]), lane_bcast(ret_decay_bwd[l]),
                        ret_norm_w[l].reshape(RET_HEADS, 1, HEAD_DIM), batch, seq)
        merged = _mix(og, proj, conv_w[l], w_ret_o[l].astype(BF16), w_conv_o[l].astype(BF16), seq)
        x1, hp, logits_t = _outproj(merged, x2, mod3, norm2_w[l][None, :], w_out[l].astype(BF16),
                                    router_w[l].T, seq)
        ids_t, w_t, rank_t, lidx_t, cnt, tile_carry, tile_cnt = _route(logits_t, router_bias[l][:, None], COMBINE_TILE)

        counts = cnt[:, 0]
        nblk = (counts + bm - 1) // bm
        blk_end = jnp.cumsum(nblk)
        pad_start = ((blk_end - nblk) * bm).astype(I32)
        n_used = blk_end[-1:].astype(I32)
        blk_ids = jnp.arange(n_blocks, dtype=I32)
        blk_e = jnp.minimum(jnp.sum((blk_ids[:, None] >= blk_end[None, :]).astype(I32), axis=1),
                            N_EXPERTS - 1)

        blk_first = jnp.concatenate([jnp.ones((1,), I32), (blk_e[1:] != blk_e[:-1]).astype(I32)])
        blk_slot = (jnp.cumsum(blk_first) - 1) % 2
        after = blk_end[blk_e]
        blk_next = jnp.where(after < n_used[0], blk_e[jnp.minimum(after, n_blocks - 1)], -1).astype(I32)

        onehot = ids_t[:, :, None] == jnp.arange(N_EXPERTS, dtype=I32)
        slots_t = rank_t + jnp.sum(jnp.where(onehot, pad_start, 0), axis=-1)
        slots = slots_t.T.reshape(t * TOP_K)
        xs, shared = _dispatch(pad_start, counts, hp, slots, ws_gate[l].astype(BF16), ws_up[l].astype(BF16),
                               ws_down[l].astype(BF16), n_rows, bm)
        blk_rows = jnp.clip(counts[blk_e] - (blk_ids * bm - pad_start[blk_e]), 0, bm).astype(I32)
        ys = _experts(blk_e, blk_first, blk_slot.astype(I32), blk_next, blk_rows, n_used, xs,
                      w_gate[l], w_up[l], w_down[l], n_blocks, bm)
        gstart = (pad_start[None, :] + tile_carry[:, :, 0]).reshape(-1)
        x2 = _combine(gstart, tile_cnt[:, :, 0].reshape(-1), x1, shared, lidx_t.T.reshape(t * TOP_K),
                      w_t.T.reshape(t * TOP_K), mod3, final_norm_w[None, :], ys, seq, COMBINE_TILE)
    return x2.reshape(batch, seq, d)
```

```python
import functools

import jax
import jax.numpy as jnp
from jax import lax
from jax.experimental import pallas as pl
from jax.experimental.pallas import tpu as pltpu

F32 = jnp.float32
BF16 = jnp.bfloat16
I32 = jnp.int32

EPS = 1e-6
RET_HEADS = 8
HEAD_DIM = 128
ROPE_BASE = 10000.0
N_EXPERTS = 64
TOP_K = 8
N_GROUPS = 8
TOPK_GROUPS = 4
GROUP_SIZE = N_EXPERTS // N_GROUPS
ROUTED_SCALE = 2.5

V7X_VMEM_BYTES = 64 * 1024 * 1024
VMEM_LIMIT = V7X_VMEM_BYTES - 8 * 1024 * 1024
BF16_SUBLANES = 16

RET_CHUNK = 256
EXPERT_BLOCK = 512
EXPERT_BLOCK_PARTS = 4
COMBINE_TILE = 256
NEG_INF = float("-inf")


def _params(n_axes):
    return pltpu.CompilerParams(dimension_semantics=("arbitrary",) * n_axes,
                                vmem_limit_bytes=VMEM_LIMIT)


def _sigmoid(x):
    return 1.0 / (1.0 + jnp.exp(-x))


def _silu(x):
    return x * _sigmoid(x)


LANES = 128
TOKEN_WORD_ROWS = 8
TOKEN_ROWS = 2 * TOKEN_WORD_ROWS


def _store_tokens(ref, x, words_ref):
    n, m = x.shape[0], x.shape[1] // 2
    assert m == TOKEN_WORD_ROWS * LANES
    packed = pltpu.pack_elementwise([x[:, :m], x[:, m:]], packed_dtype=BF16)
    for s in range(TOKEN_WORD_ROWS):
        words_ref[pl.ds(s, n, stride=TOKEN_WORD_ROWS), :] = packed[:, s * LANES:(s + 1) * LANES]
    ref[pl.ds(0, n * TOKEN_ROWS), :] = pltpu.bitcast(words_ref[pl.ds(0, n * TOKEN_WORD_ROWS), :], BF16)


def _load_tokens(tiles, words_ref):
    n = tiles.shape[0] // TOKEN_ROWS
    words_ref[pl.ds(0, n * TOKEN_WORD_ROWS), :] = pltpu.bitcast(tiles, jnp.uint32)
    p = jnp.concatenate([words_ref[pl.ds(s, n, stride=TOKEN_WORD_ROWS), :]
                         for s in range(TOKEN_WORD_ROWS)], axis=1)
    a = pltpu.unpack_elementwise(p, index=0, packed_dtype=BF16, unpacked_dtype=F32)
    b = pltpu.unpack_elementwise(p, index=1, packed_dtype=BF16, unpacked_dtype=F32)
    return jnp.concatenate([a, b], axis=1)


def _token(row):
    return pl.ds(pl.multiple_of(row * TOKEN_ROWS, TOKEN_ROWS), TOKEN_ROWS)


def _words_scratch(n_tokens):
    return pltpu.VMEM((n_tokens * TOKEN_WORD_ROWS, LANES), jnp.uint32)


def _ada_kernel(c_ref, w_ref, b_ref, o_ref):
    s = _silu(c_ref[...]).astype(BF16)
    o_ref[...] = jnp.dot(s, w_ref[...].astype(BF16), preferred_element_type=F32) + b_ref[...]


def _ada(c_pad, w_ada, b_ada):
    m, d = c_pad.shape
    n = w_ada.shape[1]
    tn = 1024
    return pl.pallas_call(
        _ada_kernel, grid=(n // tn,),
        in_specs=[pl.BlockSpec((m, d), lambda j: (0, 0)),
                  pl.BlockSpec((d, tn), lambda j: (0, j)),
                  pl.BlockSpec((1, tn), lambda j: (0, j))],
        out_specs=pl.BlockSpec((m, tn), lambda j: (0, j)),
        out_shape=jax.ShapeDtypeStruct((m, n), F32),
        compiler_params=_params(1), name="ada")(c_pad, w_ada, b_ada)


def _inproj_kernel(x_ref, mod_ref, nw_ref, w_ref, o_ref, h_ref):
    @pl.when(pl.program_id(1) == 0)
    def _():
        x = x_ref[...]
        y = x * lax.rsqrt(jnp.mean(x * x, axis=-1, keepdims=True) + EPS) * nw_ref[...]
        m = mod_ref[0]
        h_ref[...] = (y * (1.0 + m[1:2, :]) + m[0:1, :]).astype(BF16)

    o_ref[...] = jnp.dot(h_ref[...], w_ref[...], preferred_element_type=F32).astype(o_ref.dtype)


def _inproj(x2, mod3, norm_w, w_in_b, seq):
    t, d = x2.shape
    n = w_in_b.shape[1]
    tm, tn = min(1024, seq), 1024
    tiles_per_seq = seq // tm
    return pl.pallas_call(
        _inproj_kernel, grid=(t // tm, n // tn),
        in_specs=[pl.BlockSpec((tm, d), lambda i, j: (i, 0)),
                  pl.BlockSpec((1, 6, d), lambda i, j: (i // tiles_per_seq, 0, 0)),
                  pl.BlockSpec((1, d), lambda i, j: (0, 0)),
                  pl.BlockSpec((d, tn), lambda i, j: (0, j))],
        out_specs=pl.BlockSpec((tm, tn), lambda i, j: (i, j)),
        out_shape=jax.ShapeDtypeStruct((t, n), BF16),
        scratch_shapes=[pltpu.VMEM((tm, d), BF16)],
        compiler_params=_params(2), name="inproj")(x2, mod3, norm_w, w_in_b)


def _log_sigmoid(x):
    return jnp.minimum(x, 0.0) - jnp.log1p(jnp.exp(-jnp.abs(x)))


def _ret_kernel(q_ref, k_ref, v_ref, g_ref, cos_ref, sin_ref, df_ref, db_ref, nw_ref, o_ref,
                mask_ref, qf_ref, qb_ref, kf_ref, kb_ref, dec_ref, qr_ref, kr_ref, acc_ref, *, chunk):
    seq, d = q_ref.shape
    n = seq // chunk
    c = chunk

    @pl.when(pl.program_id(1) == 0)
    def _():
        lgf = _log_sigmoid(df_ref[0])
        lgb = _log_sigmoid(db_ref[0])
        ii = lax.broadcasted_iota(I32, (c, c), 0)
        jj = lax.broadcasted_iota(I32, (c, c), 1)
        diff = (ii - jj).astype(F32)
        lgf_c = jnp.concatenate([lgf] * (c // d), axis=1)
        lgb_c = jnp.concatenate([lgb] * (c // d), axis=1)
        mask_ref[...] = jnp.where(diff >= 0.0,
                                  jnp.exp(lgf_c * jnp.maximum(diff, 0.0)),
                                  jnp.exp(lgb_c * jnp.maximum(-diff, 0.0)))
        pos = lax.broadcasted_iota(I32, (c, d), 0).astype(F32)
        qf_ref[...] = jnp.exp(lgf * (pos + 1.0))
        qb_ref[...] = jnp.exp(lgb * (c - pos))
        kf_ref[...] = jnp.exp(lgf * (c - 1.0 - pos))
        kb_ref[...] = jnp.exp(lgb * pos)
        dec_ref[0:1, :] = jnp.exp(lgf * c)
        dec_ref[1:2, :] = jnp.exp(lgb * c)

    scale = d ** -0.5
    nt = (((1,), (1,)), ((), ()))
    sls = [pl.ds(i * c, c) for i in range(n)]

    for sl in sls:
        cs = cos_ref[sl, :]
        sn = sin_ref[sl, :]
        q = q_ref[sl, :].astype(F32)
        k = k_ref[sl, :].astype(F32)
        qr_ref[sl, :] = q * cs + pltpu.roll(q, d // 2, 1) * sn
        kr_ref[sl, :] = (k * cs + pltpu.roll(k, d // 2, 1) * sn) * scale

    for sl in sls:
        s = lax.dot_general(qr_ref[sl, :].astype(BF16), kr_ref[sl, :].astype(BF16), nt,
                            preferred_element_type=F32)
        p = (s * mask_ref[...]).astype(BF16)
        acc_ref[sl, :] = jnp.dot(p, v_ref[sl, :], preferred_element_type=F32)

    def kv_state(sl, kw_ref):
        kw = (kr_ref[sl, :] * kw_ref[...]).T.astype(BF16)
        return jnp.dot(kw, v_ref[sl, :], preferred_element_type=F32)

    def scan(order, qw_ref, kw_ref, dec):
        st = jnp.zeros((d, d), F32)
        for idx, ci in enumerate(order):
            sl = sls[ci]
            if idx > 0:
                qw = (qr_ref[sl, :] * qw_ref[...]).astype(BF16)
                acc_ref[sl, :] += jnp.dot(qw, st.astype(BF16), preferred_element_type=F32)
            if idx < n - 1:
                st = dec * st + kv_state(sl, kw_ref)

    scan(list(range(n)), qf_ref, kf_ref, dec_ref[0:1, :])
    scan(list(range(n - 1, -1, -1)), qb_ref, kb_ref, dec_ref[1:2, :])

    nw = nw_ref[0]
    for sl in sls:
        o = acc_ref[sl, :]
        on = o * lax.rsqrt(jnp.mean(o * o, axis=-1, keepdims=True) + EPS) * nw
        g = g_ref[sl, :].astype(F32)
        o_ref[sl, :] = (_silu(g) * on).astype(o_ref.dtype)


def _retention(proj, cos, sin, dec_f, dec_b, ret_norm_w, batch, seq):
    h, d = RET_HEADS, HEAD_DIM
    c = min(RET_CHUNK, seq)
    col = lambda off: pl.BlockSpec((seq, d), lambda hh, b: (b, off + hh))
    per_head = pl.BlockSpec((1, 1, d), lambda hh, b: (hh, 0, 0))
    table = pl.BlockSpec((seq, d), lambda hh, b: (0, 0))
    vm = lambda shape, dt=F32: pltpu.VMEM(shape, dt)
    return pl.pallas_call(
        functools.partial(_ret_kernel, chunk=c), grid=(h, batch),
        in_specs=[col(0), col(h), col(2 * h), col(3 * h), table, table, per_head, per_head, per_head],
        out_specs=pl.BlockSpec((seq, d), lambda hh, b: (b, hh)),
        out_shape=jax.ShapeDtypeStruct((batch * seq, h * d), BF16),
        scratch_shapes=[vm((c, c)), vm((c, d)), vm((c, d)), vm((c, d)), vm((c, d)), vm((8, d)),
                        vm((seq, d)), vm((seq, d)), vm((seq, d))],
        compiler_params=_params(2), name="retention",
    )(proj, proj, proj, proj, cos, sin, dec_f, dec_b, ret_norm_w)


def _mix_kernel(og_ref, cb_ref, cc_ref, cu_ref, ccp_ref, cup_ref, ccn_ref, cun_ref,
                gr0_ref, gr1_ref, gc0_ref, gc1_ref, cw_ref, wr_ref, wc_ref, o_ref, *, tiles_per_seq):
    i = pl.program_id(0)
    tm, dr = cc_ref.shape
    u = cc_ref[...].astype(F32) * cu_ref[...].astype(F32)
    pos = i % tiles_per_seq
    last = BF16_SUBLANES - 1
    u_before = ccp_ref[last:last + 1, :].astype(F32) * cup_ref[last:last + 1, :].astype(F32)
    u_before = jnp.where(pos == 0, 0.0, u_before)
    u_after = ccn_ref[0:1, :].astype(F32) * cun_ref[0:1, :].astype(F32)
    u_after = jnp.where(pos == tiles_per_seq - 1, 0.0, u_after)
    row = lax.broadcasted_iota(I32, u.shape, 0)
    u_prev = jnp.where(row == 0, u_before, pltpu.roll(u, 1, 0))
    u_next = jnp.where(row == tm - 1, u_after, pltpu.roll(u, tm - 1, 0))
    cw = cw_ref[...]
    y = cw[0:1, :] * u_prev + cw[1:2, :] * u + cw[2:3, :] * u_next
    z = (cb_ref[...].astype(F32) * y).astype(BF16)

    yr = jnp.dot(og_ref[...], wr_ref[...], preferred_element_type=F32)
    yc = jnp.dot(z, wc_ref[...], preferred_element_type=F32)
    for lo, gr_ref, gc_ref in ((0, gr0_ref, gc0_ref), (dr, gr1_ref, gc1_ref)):
        merged = (_sigmoid(gr_ref[...].astype(F32)) * yr[:, lo:lo + dr]
                  + _sigmoid(gc_ref[...].astype(F32)) * yc[:, lo:lo + dr])
        o_ref[:, lo:lo + dr] = merged.astype(o_ref.dtype)


def _mix(og, proj, conv_w, w_ret_o_b, w_conv_o_b, seq):
    t, dr = og.shape
    d = w_ret_o_b.shape[1]
    assert d == 2 * dr
    tm = min(512, seq)
    tiles_per_seq = seq // tm
    hb = tm // BF16_SUBLANES
    n_hblk = t // BF16_SUBLANES
    wide = lambda off: pl.BlockSpec((tm, dr), lambda i: (i, off))
    before = lambda off: pl.BlockSpec((BF16_SUBLANES, dr), lambda i: (jnp.maximum(i * hb - 1, 0), off))
    after = lambda off: pl.BlockSpec((BF16_SUBLANES, dr), lambda i: (jnp.minimum((i + 1) * hb, n_hblk - 1), off))
    whole = lambda shape: pl.BlockSpec(shape, lambda i: (0, 0))
    return pl.pallas_call(
        functools.partial(_mix_kernel, tiles_per_seq=tiles_per_seq), grid=(t // tm,),
        in_specs=[wide(0), wide(4), wide(5), wide(6), before(5), before(6), after(5), after(6),
                  wide(7), wide(8), wide(9), wide(10),
                  whole((3, dr)), whole((dr, d)), whole((dr, d))],
        out_specs=pl.BlockSpec((tm, d), lambda i: (i, 0)),
        out_shape=jax.ShapeDtypeStruct((t, d), BF16),
        compiler_params=_params(1), name="mix",
    )(og, proj, proj, proj, proj, proj, proj, proj, proj, proj, proj, proj, conv_w, w_ret_o_b, w_conv_o_b)


def _outproj_kernel(m_ref, x_ref, mod_ref, nw_ref, w_ref, rwt_ref, x1_ref, hp_ref, lg_ref, words_ref):
    y = jnp.dot(m_ref[...], w_ref[...], preferred_element_type=F32)
    m = mod_ref[0]
    x1 = x_ref[...] + m[2:3, :] * y
    x1_ref[...] = x1
    hn = x1 * lax.rsqrt(jnp.mean(x1 * x1, axis=-1, keepdims=True) + EPS) * nw_ref[...]
    h = hn * (1.0 + m[4:5, :]) + m[3:4, :]
    _store_tokens(hp_ref, h, words_ref)
    h_hi = h.astype(BF16)
    h_lo = (h - h_hi.astype(F32)).astype(BF16)
    rw = rwt_ref[...]
    r_hi = rw.astype(BF16)
    r_lo = (rw - r_hi.astype(F32)).astype(BF16)
    nt = (((1,), (1,)), ((), ()))
    dot = functools.partial(lax.dot_general, dimension_numbers=nt, preferred_element_type=F32)
    lg_ref[...] = dot(r_hi, h_hi) + dot(r_hi, h_lo) + dot(r_lo, h_hi)


def _outproj(merged, x2, mod3, norm_w, w_out_b, router_wt, seq):
    t, d = x2.shape
    e = router_wt.shape[0]
    tm = min(512, seq)
    tiles_per_seq = seq // tm
    return pl.pallas_call(
        _outproj_kernel, grid=(t // tm,),
        in_specs=[pl.BlockSpec((tm, d), lambda i: (i, 0)),
                  pl.BlockSpec((tm, d), lambda i: (i, 0)),
                  pl.BlockSpec((1, 6, d), lambda i: (i // tiles_per_seq, 0, 0)),
                  pl.BlockSpec((1, d), lambda i: (0, 0)),
                  pl.BlockSpec((d, d), lambda i: (0, 0)),
                  pl.BlockSpec((e, d), lambda i: (0, 0))],
        out_specs=[pl.BlockSpec((tm, d), lambda i: (i, 0)),
                   pl.BlockSpec((TOKEN_ROWS * tm, LANES), lambda i: (i, 0)),
                   pl.BlockSpec((e, tm), lambda i: (0, i))],
        out_shape=[jax.ShapeDtypeStruct((t, d), F32),
                   jax.ShapeDtypeStruct((TOKEN_ROWS * t, LANES), BF16),
                   jax.ShapeDtypeStruct((e, t), F32)],
        scratch_shapes=[_words_scratch(tm)],
        compiler_params=_params(1), name="outproj",
    )(merged, x2, mod3, norm_w, w_out_b, router_wt)


def _first_max(x, iota, sentinel):
    m = jnp.max(x, axis=0, keepdims=True)
    idx = jnp.min(jnp.where(x == m, iota, sentinel), axis=0, keepdims=True)
    return m, idx, iota == idx


def _route_tile(lg, bias, carry):
    e, tk = lg.shape
    s = _sigmoid(lg)
    biased = s + bias
    sub = lax.broadcasted_iota(I32, (GROUP_SIZE, tk), 0)
    group_rows = []
    for g in range(N_GROUPS):
        xg = biased[g * GROUP_SIZE:(g + 1) * GROUP_SIZE, :]
        m1, _, pick = _first_max(xg, sub, GROUP_SIZE)
        m2 = jnp.max(jnp.where(pick, NEG_INF, xg), axis=0, keepdims=True)
        group_rows.append(m1 + m2)
    gs = jnp.concatenate(group_rows, axis=0)
    gsub = lax.broadcasted_iota(I32, (N_GROUPS, tk), 0)
    sel = jnp.zeros((N_GROUPS, tk), F32)
    for _ in range(TOPK_GROUPS):
        _, _, pick = _first_max(gs, gsub, N_GROUPS)
        sel = jnp.where(pick, 1.0, sel)
        gs = jnp.where(pick, NEG_INF, gs)
    masked_rows = []
    for g in range(N_GROUPS):
        xg = biased[g * GROUP_SIZE:(g + 1) * GROUP_SIZE, :]
        masked_rows.append(jnp.where(sel[g:g + 1, :] > 0.5, xg, NEG_INF))
    masked = jnp.concatenate(masked_rows, axis=0)

    eio = lax.broadcasted_iota(I32, (e, tk), 0)
    chosen = jnp.zeros((e, tk), F32)
    ids, top_s = [], []
    for _ in range(TOP_K):
        _, idx, pick = _first_max(masked, eio, e)
        ids.append(idx)
        top_s.append(jnp.sum(jnp.where(pick, s, 0.0), axis=0, keepdims=True))
        chosen = jnp.where(pick, 1.0, chosen)
        masked = jnp.where(pick, NEG_INF, masked)
    total = top_s[0]
    for ts in top_s[1:]:
        total = total + ts

    before = (lax.broadcasted_iota(I32, (tk, tk), 0) < lax.broadcasted_iota(I32, (tk, tk), 1))
    upper = jnp.where(before, 1.0, 0.0).astype(BF16)
    local = jnp.dot(chosen.astype(BF16), upper, preferred_element_type=F32)
    tile_cnt = jnp.broadcast_to(jnp.sum(chosen, axis=1, keepdims=True), carry.shape)
    below = (lax.broadcasted_iota(I32, (e, e), 1) < lax.broadcasted_iota(I32, (e, e), 0))
    lower = jnp.where(below, 1.0, 0.0).astype(BF16)
    first = jnp.dot(lower, tile_cnt.astype(BF16), preferred_element_type=F32)
    rank = local + carry[:, 0:1]
    place = local + first[:, 0:1]
    weights = [ts / total * ROUTED_SCALE for ts in top_s]
    pick_row = lambda k, v: jnp.sum(jnp.where(eio == ids[k], v, 0.0), axis=0, keepdims=True).astype(I32)
    ranks = [pick_row(k, rank) for k in range(TOP_K)]
    places = [pick_row(k, place) for k in range(TOP_K)]
    return ids, weights, ranks, places, tile_cnt


def _route_kernel(lg_ref, bias_ref, ids_ref, w_ref, rank_ref, lidx_ref, cnt_ref, tcarry_ref, tcnt_ref, carry_ref,
                  *, tile):
    i = pl.program_id(0)

    @pl.when(i == 0)
    def _():
        carry_ref[...] = jnp.zeros_like(carry_ref)

    carry = carry_ref[...]
    for j in range(lg_ref.shape[1] // tile):
        cols = slice(j * tile, (j + 1) * tile)
        ids, weights, ranks, places, tile_cnt = _route_tile(lg_ref[:, cols], bias_ref[...], carry)
        for k in range(TOP_K):
            ids_ref[k:k + 1, cols] = ids[k]
            w_ref[k:k + 1, cols] = weights[k]
            rank_ref[k:k + 1, cols] = ranks[k]
            lidx_ref[k:k + 1, cols] = places[k]
        tcarry_ref[j] = carry.astype(I32)
        tcnt_ref[j] = tile_cnt.astype(I32)
        carry = carry + tile_cnt
    carry_ref[...] = carry
    cnt_ref[...] = carry.astype(I32)


def _route(logits_t, bias_col, tile):
    e, t = logits_t.shape
    tk = min(4 * tile, t)
    per_step = tk // tile
    row8 = lambda dt: jax.ShapeDtypeStruct((TOP_K, t), dt)
    blk8 = pl.BlockSpec((TOP_K, tk), lambda i: (0, i))
    per_tile = pl.BlockSpec((per_step, e, LANES), lambda i: (i, 0, 0))
    return pl.pallas_call(
        functools.partial(_route_kernel, tile=tile), grid=(t // tk,),
        in_specs=[pl.BlockSpec((e, tk), lambda i: (0, i)),
                  pl.BlockSpec((e, 1), lambda i: (0, 0))],
        out_specs=[blk8, blk8, blk8, blk8, pl.BlockSpec((e, LANES), lambda i: (0, 0)), per_tile, per_tile],
        out_shape=[row8(I32), row8(F32), row8(I32), row8(I32), jax.ShapeDtypeStruct((e, LANES), I32),
                   jax.ShapeDtypeStruct((t // tile, e, LANES), I32), jax.ShapeDtypeStruct((t // tile, e, LANES), I32)],
        scratch_shapes=[pltpu.VMEM((e, LANES), F32)],
        compiler_params=_params(1), name="route",
    )(logits_t, bias_col)


def _dispatch_kernel(pad_ref, cnt_ref, hp_ref, slot_ref, wsg_ref, wsu_ref, wsd_ref, xs_ref, sh_ref,
                     zero_ref, words_ref, sem, zsem, *, bm):
    i = pl.program_id(0)
    td = hp_ref.shape[0] // TOKEN_ROWS
    n_rows = xs_ref.shape[0] // TOKEN_ROWS
    n_blocks = n_rows // bm
    block = bm * TOKEN_ROWS

    @pl.when(i == 0)
    def _():
        zero_ref[...] = jnp.zeros_like(zero_ref)
        last_e = N_EXPERTS - 1
        used_blocks = pad_ref[last_e] // bm + (cnt_ref[last_e] + bm - 1) // bm

        def tail_copy(j):
            return pltpu.make_async_copy(zero_ref, xs_ref.at[pl.ds(pl.multiple_of(j * block, block), block)], zsem)

        def tail(j, c, wait):
            cp = tail_copy(j)
            cp.wait() if wait else cp.start()
            return c

        def pad(e, c, wait):
            row = pad_ref[e] + cnt_ref[e]
            n = (cnt_ref[e] + bm - 1) // bm * bm - cnt_ref[e]
            bit = bm // 2
            while bit:
                take = (n & bit) != 0
                size = bit * TOKEN_ROWS
                cp = pltpu.make_async_copy(
                    zero_ref.at[pl.ds(0, size)],
                    xs_ref.at[pl.ds(pl.multiple_of(row * TOKEN_ROWS, TOKEN_ROWS), size)], zsem)

                @pl.when(take)
                def _():
                    cp.wait() if wait else cp.start()

                row = row + jnp.where(take, bit, 0)
                bit //= 2
            return c

        for wait in (False, True):
            lax.fori_loop(used_blocks, n_blocks, functools.partial(tail, wait=wait), 0)
            lax.fori_loop(0, N_EXPERTS, functools.partial(pad, wait=wait), 0)

    def body(t, c):
        for k in range(TOP_K):
            slot = slot_ref[t * TOP_K + k]
            pltpu.make_async_copy(hp_ref.at[_token(t)], xs_ref.at[_token(slot)], sem).start(priority=k % 2)
        return c

    lax.fori_loop(0, td, body, 0)

    h = _load_tokens(hp_ref[...], words_ref).astype(BF16)
    sg = jnp.dot(h, wsg_ref[...], preferred_element_type=F32)
    su = jnp.dot(h, wsu_ref[...], preferred_element_type=F32)
    shared = jnp.dot((_silu(sg) * su).astype(BF16), wsd_ref[...], preferred_element_type=F32)
    sh_ref[...] = shared.astype(sh_ref.dtype)

    for k in range(TOP_K):
        pltpu.make_async_copy(hp_ref, xs_ref.at[pl.ds(0, td * TOKEN_ROWS)], sem).wait()


def _dispatch(pad_start, counts, hp, slots, wsg_b, wsu_b, wsd_b, n_rows, bm):
    t = hp.shape[0] // TOKEN_ROWS
    d, ds = wsg_b.shape
    td = min(512, t)
    whole = lambda shape: pl.BlockSpec(shape, lambda i, *_: (0, 0))
    grid_spec = pltpu.PrefetchScalarGridSpec(
        num_scalar_prefetch=2, grid=(t // td,),
        in_specs=[pl.BlockSpec((td * TOKEN_ROWS, LANES), lambda i, *_: (i, 0)),
                  pl.BlockSpec((td * TOP_K,), lambda i, *_: (i,), memory_space=pltpu.SMEM),
                  whole((d, ds)), whole((d, ds)), whole((ds, d))],
        out_specs=[pl.BlockSpec(memory_space=pl.ANY),
                   pl.BlockSpec((td, d), lambda i, *_: (i, 0))],
        scratch_shapes=[pltpu.VMEM((bm * TOKEN_ROWS, LANES), BF16), _words_scratch(td),
                        pltpu.SemaphoreType.DMA(()), pltpu.SemaphoreType.DMA(())])
    return pl.pallas_call(
        functools.partial(_dispatch_kernel, bm=bm), grid_spec=grid_spec,
        out_shape=[jax.ShapeDtypeStruct((n_rows * TOKEN_ROWS, LANES), BF16),
                   jax.ShapeDtypeStruct((t, d), BF16)],
        compiler_params=_params(1), name="dispatch",
    )(pad_start, counts, hp, slots, wsg_b, wsu_b, wsd_b)


def _expert_kernel(be_ref, first_ref, slot_ref, nxt_ref, rows_ref, nu_ref, x_ref, wg_hbm, wu_hbm, wd_hbm, o_ref,
                   words_ref, wg32, wu32, wd32, wg16, wu16, wd16, wsem):
    i = pl.program_id(0)

    def weight_copies(e, s):
        return (pltpu.make_async_copy(wg_hbm.at[e], wg32.at[s], wsem.at[s, 0]),
                pltpu.make_async_copy(wu_hbm.at[e], wu32.at[s], wsem.at[s, 1]),
                pltpu.make_async_copy(wd_hbm.at[e], wd32.at[s], wsem.at[s, 2]))

    @pl.when(i == 0)
    def _():
        for cp in weight_copies(be_ref[0], 0):
            cp.start()

    @pl.when(jnp.logical_and(first_ref[i] == 1, i < nu_ref[0]))
    def _():
        s = slot_ref[i]
        for cp in weight_copies(be_ref[i], s):
            cp.wait()

        @pl.when(nxt_ref[i] >= 0)
        def _():
            for cp in weight_copies(nxt_ref[i], 1 - s):
                cp.start()

        wg16[...] = wg32[s].astype(BF16)
        wu16[...] = wu32[s].astype(BF16)
        wd16[...] = wd32[s].astype(BF16)

    bm = x_ref.shape[0] // TOKEN_ROWS
    quarter = bm // EXPERT_BLOCK_PARTS
    used = jnp.where(i < nu_ref[0], (rows_ref[i] + quarter - 1) // quarter, 0)

    def swiglu_rows(n):
        x = _load_tokens(x_ref[pl.ds(0, n * TOKEN_ROWS), :], words_ref).astype(BF16)
        g = jnp.dot(x, wg16[...], preferred_element_type=F32)
        u = jnp.dot(x, wu16[...], preferred_element_type=F32)
        mid = (_silu(g) * u).astype(BF16)
        _store_tokens(o_ref, jnp.dot(mid, wd16[...], preferred_element_type=F32), words_ref)
        if n < bm:
            rest = (bm - n) * TOKEN_ROWS
            o_ref[pl.ds(n * TOKEN_ROWS, rest), :] = jnp.zeros((rest, LANES), o_ref.dtype)

    for parts in range(1, EXPERT_BLOCK_PARTS + 1):
        pl.when(used == parts)(functools.partial(swiglu_rows, parts * quarter))

    @pl.when(used == 0)
    def _():
        o_ref[...] = jnp.zeros_like(o_ref)


def _experts(blk_e, blk_first, blk_slot, blk_next, blk_rows, n_used, xs, w_gate, w_up, w_down, n_blocks, bm):
    _, d, de = w_gate.shape
    x_map = lambda i, be, fi, sl, nx, ro, nu: (jnp.minimum(i, nu[0] - 1), 0)
    hbm = pl.BlockSpec(memory_space=pl.ANY)
    grid_spec = pltpu.PrefetchScalarGridSpec(
        num_scalar_prefetch=6, grid=(n_blocks,),
        in_specs=[pl.BlockSpec((bm * TOKEN_ROWS, LANES), x_map), hbm, hbm, hbm],
        out_specs=pl.BlockSpec((bm * TOKEN_ROWS, LANES), lambda i, *_: (i, 0)),
        scratch_shapes=[_words_scratch(bm),
                        pltpu.VMEM((2, d, de), F32), pltpu.VMEM((2, d, de), F32), pltpu.VMEM((2, de, d), F32),
                        pltpu.VMEM((d, de), BF16), pltpu.VMEM((d, de), BF16), pltpu.VMEM((de, d), BF16),
                        pltpu.SemaphoreType.DMA((2, 3))])
    return pl.pallas_call(
        _expert_kernel, grid_spec=grid_spec,
        out_shape=jax.ShapeDtypeStruct((n_blocks * bm * TOKEN_ROWS, LANES), BF16),
        compiler_params=_params(1), name="experts",
    )(blk_e, blk_first, blk_slot, blk_next, blk_rows, n_used, xs, w_gate, w_up, w_down)


def _combine_kernel(gstart_ref, tcnt_ref, x1_ref, sh_ref, lidx_ref, w_ref, mod_ref, fw_ref, ys_ref, o_ref,
                    stage, acc_ref, sem, *, n_tiles):
    i = pl.program_id(0)
    tc = x1_ref.shape[0]
    n_e = N_EXPERTS

    def fetch(tile, buf):
        def per_expert(e, off):
            count = tcnt_ref[tile * n_e + e]
            row = gstart_ref[tile * n_e + e]
            bit = tc
            while bit:
                take = (count & bit) != 0
                size = bit * TOKEN_ROWS
                cp = pltpu.make_async_copy(
                    ys_ref.at[pl.ds(pl.multiple_of(row * TOKEN_ROWS, TOKEN_ROWS), size)],
                    stage.at[buf, pl.ds(pl.multiple_of(off * TOKEN_ROWS, TOKEN_ROWS), size)], sem.at[buf])
                pl.when(take)(cp.start)
                step = jnp.where(take, bit, 0)
                row, off = row + step, off + step
                bit //= 2
            return off
        lax.fori_loop(0, n_e, per_expert, 0)

    @pl.when(i == 0)
    def _():
        fetch(0, 0)

    @pl.when(i + 1 < n_tiles)
    def _():
        fetch(i + 1, (i + 1) % 2)

    buf = i % 2
    pltpu.make_async_copy(ys_ref.at[pl.ds(0, tc * TOP_K * TOKEN_ROWS)], stage.at[buf], sem.at[buf]).wait()

    def per_token(t, c):
        lo = hi = None
        for k in range(TOP_K):
            place = lidx_ref[t * TOP_K + k]
            wk = w_ref[t * TOP_K + k]
            words = pltpu.bitcast(stage[buf, _token(place), :], jnp.uint32)
            a = wk * pltpu.unpack_elementwise(words, index=0, packed_dtype=BF16, unpacked_dtype=F32)
            b = wk * pltpu.unpack_elementwise(words, index=1, packed_dtype=BF16, unpacked_dtype=F32)
            lo, hi = (a, b) if k == 0 else (lo + a, hi + b)
        rows = pl.ds(pl.multiple_of(t * TOKEN_WORD_ROWS, TOKEN_WORD_ROWS), TOKEN_WORD_ROWS)
        acc_ref[0, rows, :] = lo
        acc_ref[1, rows, :] = hi
        return c

    lax.fori_loop(0, tc, per_token, 0, unroll=2)

    halves = [jnp.concatenate([acc_ref[half, pl.ds(s, tc, stride=TOKEN_WORD_ROWS), :]
                               for s in range(TOKEN_WORD_ROWS)], axis=1) for half in range(2)]
    routed = jnp.concatenate(halves, axis=1)
    x = x1_ref[...] + mod_ref[0][5:6, :] * (routed + sh_ref[...].astype(F32))
    o_ref[...] = x * lax.rsqrt(jnp.mean(x * x, axis=-1, keepdims=True) + EPS) * fw_ref[...]


def _combine(gstart, tcnt, x1, shared, lidx, w_flat, mod3, final_w, ys, seq, tc):
    t, d = x1.shape
    n_tiles = t // tc
    tiles_per_seq = seq // tc
    per_tile = lambda: pl.BlockSpec((tc * TOP_K,), lambda i, *_: (i,), memory_space=pltpu.SMEM)
    grid_spec = pltpu.PrefetchScalarGridSpec(
        num_scalar_prefetch=2, grid=(n_tiles,),
        in_specs=[pl.BlockSpec((tc, d), lambda i, *_: (i, 0)),
                  pl.BlockSpec((tc, d), lambda i, *_: (i, 0)),
                  per_tile(), per_tile(),
                  pl.BlockSpec((1, 6, d), lambda i, *_: (i // tiles_per_seq, 0, 0)),
                  pl.BlockSpec((1, d), lambda i, *_: (0, 0)),
                  pl.BlockSpec(memory_space=pl.ANY)],
        out_specs=pl.BlockSpec((tc, d), lambda i, *_: (i, 0)),
        scratch_shapes=[pltpu.VMEM((2, tc * TOP_K * TOKEN_ROWS, LANES), BF16),
                        pltpu.VMEM((2, tc * TOKEN_WORD_ROWS, LANES), F32),
                        pltpu.SemaphoreType.DMA((2,))])
    return pl.pallas_call(
        functools.partial(_combine_kernel, n_tiles=n_tiles), grid_spec=grid_spec,
        out_shape=jax.ShapeDtypeStruct((t, d), F32),
        compiler_params=_params(1), name="combine",
    )(gstart, tcnt, x1, shared, lidx, w_flat, mod3, final_w, ys)


def _rope_tables(seq, d):
    half = d // 2
    inv = ROPE_BASE ** (-jnp.arange(half, dtype=F32) / half)
    ang = jnp.arange(seq, dtype=F32)[:, None] * inv[None, :]
    cos, sin = jnp.cos(ang), jnp.sin(ang)
    return jnp.concatenate([cos, cos], axis=1), jnp.concatenate([-sin, sin], axis=1)


def kernel(x, c, w_ada, b_ada, norm1_w, w_in, ret_decay_fwd, ret_decay_bwd, ret_norm_w, w_ret_o, conv_w,
           w_conv_o, w_out, norm2_w, router_w, router_bias, w_gate, w_up, w_down, ws_gate, ws_up, ws_down,
           final_norm_w):
    batch, seq, d = x.shape
    depth = w_ada.shape[0]
    t = batch * seq
    bm = EXPERT_BLOCK
    n_blocks = (t * TOP_K + N_EXPERTS * (bm - 1)) // bm
    n_rows = n_blocks * bm
    cos, sin = _rope_tables(seq, HEAD_DIM)
    c_pad = jnp.pad(c, ((0, BF16_SUBLANES - batch % BF16_SUBLANES), (0, 0)))
    x2 = x.reshape(t, d)

    assert depth == 1, "the final norm is fused into the last stage of a single layer"
    for l in range(depth):
        mod = _ada(c_pad, w_ada[l], b_ada[l][None, :])[:batch]
        mod3 = mod.reshape(batch, 6, d)
        proj = _inproj(x2, mod3, norm1_w[l][None, :], w_in[l].astype(BF16), seq)
        lane_bcast = lambda v: jnp.broadcast_to(v[:, None, None], (RET_HEADS, 1, HEAD_DIM))
        og = _retention(proj, cos, sin, lane_bcast(ret_decay_fwd[l]), lane_bcast(ret_decay_bwd[l]),
                        ret_norm_w[l].reshape(RET_HEADS, 1, HEAD_DIM), batch, seq)
        merged = _mix(og, proj, conv_w[l], w_ret_o[l].astype(BF16), w_conv_o[l].astype(BF16), seq)
        x1, hp, logits_t = _outproj(merged, x2, mod3, norm2_w[l][None, :], w_out[l].astype(BF16),
                                    router_w[l].T, seq)
        ids_t, w_t, rank_t, lidx_t, cnt, tile_carry, tile_cnt = _route(logits_t, router_bias[l][:, None], COMBINE_TILE)

        counts = cnt[:, 0]
        nblk = (counts + bm - 1) // bm
        blk_end = jnp.cumsum(nblk)
        pad_start = ((blk_end - nblk) * bm).astype(I32)
        n_used = blk_end[-1:].astype(I32)
        blk_ids = jnp.arange(n_blocks, dtype=I32)
        blk_e = jnp.minimum(jnp.sum((blk_ids[:, None] >= blk_end[None, :]).astype(I32), axis=1),
                            N_EXPERTS - 1)

        blk_first = jnp.concatenate([jnp.ones((1,), I32), (blk_e[1:] != blk_e[:-1]).astype(I32)])
        blk_slot = (jnp.cumsum(blk_first) - 1) % 2
        after = blk_end[blk_e]
        blk_next = jnp.where(after < n_used[0], blk_e[jnp.minimum(after, n_blocks - 1)], -1).astype(I32)

        onehot = ids_t[:, :, None] == jnp.arange(N_EXPERTS, dtype=I32)
        slots_t = rank_t + jnp.sum(jnp.where(onehot, pad_start, 0), axis=-1)
        slots = slots_t.T.reshape(t * TOP_K)
        xs, shared = _dispatch(pad_start, counts, hp, slots, ws_gate[l].astype(BF16), ws_up[l].astype(BF16),
                               ws_down[l].astype(BF16), n_rows, bm)
        blk_rows = jnp.clip(counts[blk_e] - (blk_ids * bm - pad_start[blk_e]), 0, bm).astype(I32)
        ys = _experts(blk_e, blk_first, blk_slot.astype(I32), blk_next, blk_rows, n_used, xs,
                      w_gate[l], w_up[l], w_down[l], n_blocks, bm)
        gstart = (pad_start[None, :] + tile_carry[:, :, 0]).reshape(-1)
        x2 = _combine(gstart, tile_cnt[:, :, 0].reshape(-1), x1, shared, lidx_t.T.reshape(t * TOP_K),
                      w_t.T.reshape(t * TOP_K), mod3, final_norm_w[None, :], ys, seq, COMBINE_TILE)
    return x2.reshape(batch, seq, d)
```

```python
import functools

import jax
import jax.numpy as jnp
from jax import lax
from jax.experimental import pallas as pl
from jax.experimental.pallas import tpu as pltpu

F32 = jnp.float32
BF16 = jnp.bfloat16
I32 = jnp.int32

EPS = 1e-6
RET_HEADS = 8
HEAD_DIM = 128
ROPE_BASE = 10000.0
N_EXPERTS = 64
TOP_K = 8
N_GROUPS = 8
TOPK_GROUPS = 4
GROUP_SIZE = N_EXPERTS // N_GROUPS
ROUTED_SCALE = 2.5

V7X_VMEM_BYTES = 64 * 1024 * 1024
VMEM_LIMIT = V7X_VMEM_BYTES - 8 * 1024 * 1024
BF16_SUBLANES = 16

RET_CHUNK = 256
EXPERT_BLOCK = 512
EXPERT_BLOCK_PARTS = 4
COMBINE_TILE = 256
NEG_INF = float("-inf")


def _params(n_axes):
    return pltpu.CompilerParams(dimension_semantics=("arbitrary",) * n_axes,
                                vmem_limit_bytes=VMEM_LIMIT)


def _sigmoid(x):
    return 1.0 / (1.0 + jnp.exp(-x))


def _silu(x):
    return x * _sigmoid(x)


LANES = 128
TOKEN_WORD_ROWS = 8
TOKEN_ROWS = 2 * TOKEN_WORD_ROWS


def _store_tokens(ref, x, words_ref):
    n, m = x.shape[0], x.shape[1] // 2
    assert m == TOKEN_WORD_ROWS * LANES
    packed = pltpu.pack_elementwise([x[:, :m], x[:, m:]], packed_dtype=BF16)
    for s in range(TOKEN_WORD_ROWS):
        words_ref[pl.ds(s, n, stride=TOKEN_WORD_ROWS), :] = packed[:, s * LANES:(s + 1) * LANES]
    ref[pl.ds(0, n * TOKEN_ROWS), :] = pltpu.bitcast(words_ref[pl.ds(0, n * TOKEN_WORD_ROWS), :], BF16)


def _load_tokens(tiles, words_ref):
    n = tiles.shape[0] // TOKEN_ROWS
    words_ref[pl.ds(0, n * TOKEN_WORD_ROWS), :] = pltpu.bitcast(tiles, jnp.uint32)
    p = jnp.concatenate([words_ref[pl.ds(s, n, stride=TOKEN_WORD_ROWS), :]
                         for s in range(TOKEN_WORD_ROWS)], axis=1)
    a = pltpu.unpack_elementwise(p, index=0, packed_dtype=BF16, unpacked_dtype=F32)
    b = pltpu.unpack_elementwise(p, index=1, packed_dtype=BF16, unpacked_dtype=F32)
    return jnp.concatenate([a, b], axis=1)


def _token(row):
    return pl.ds(pl.multiple_of(row * TOKEN_ROWS, TOKEN_ROWS), TOKEN_ROWS)


def _words_scratch(n_tokens):
    return pltpu.VMEM((n_tokens * TOKEN_WORD_ROWS, LANES), jnp.uint32)


def _ada_kernel(c_ref, w_ref, b_ref, o_ref):
    s = _silu(c_ref[...]).astype(BF16)
    o_ref[...] = jnp.dot(s, w_ref[...].astype(BF16), preferred_element_type=F32) + b_ref[...]


def _ada(c_pad, w_ada, b_ada):
    m, d = c_pad.shape
    n = w_ada.shape[1]
    tn = 1024
    return pl.pallas_call(
        _ada_kernel, grid=(n // tn,),
        in_specs=[pl.BlockSpec((m, d), lambda j: (0, 0)),
                  pl.BlockSpec((d, tn), lambda j: (0, j)),
                  pl.BlockSpec((1, tn), lambda j: (0, j))],
        out_specs=pl.BlockSpec((m, tn), lambda j: (0, j)),
        out_shape=jax.ShapeDtypeStruct((m, n), F32),
        compiler_params=_params(1), name="ada")(c_pad, w_ada, b_ada)


def _inproj_kernel(x_ref, mod_ref, nw_ref, w_ref, o_ref, h_ref):
    @pl.when(pl.program_id(1) == 0)
    def _():
        x = x_ref[...]
        y = x * lax.rsqrt(jnp.mean(x * x, axis=-1, keepdims=True) + EPS) * nw_ref[...]
        m = mod_ref[0]
        h_ref[...] = (y * (1.0 + m[1:2, :]) + m[0:1, :]).astype(BF16)

    o_ref[...] = jnp.dot(h_ref[...], w_ref[...], preferred_element_type=F32).astype(o_ref.dtype)


def _inproj(x2, mod3, norm_w, w_in_b, seq):
    t, d = x2.shape
    n = w_in_b.shape[1]
    tm, tn = min(1024, seq), 1024
    tiles_per_seq = seq // tm
    return pl.pallas_call(
        _inproj_kernel, grid=(t // tm, n // tn),
        in_specs=[pl.BlockSpec((tm, d), lambda i, j: (i, 0)),
                  pl.BlockSpec((1, 6, d), lambda i, j: (i // tiles_per_seq, 0, 0)),
                  pl.BlockSpec((1, d), lambda i, j: (0, 0)),
                  pl.BlockSpec((d, tn), lambda i, j: (0, j))],
        out_specs=pl.BlockSpec((tm, tn), lambda i, j: (i, j)),
        out_shape=jax.ShapeDtypeStruct((t, n), BF16),
        scratch_shapes=[pltpu.VMEM((tm, d), BF16)],
        compiler_params=_params(2), name="inproj")(x2, mod3, norm_w, w_in_b)


def _log_sigmoid(x):
    return jnp.minimum(x, 0.0) - jnp.log1p(jnp.exp(-jnp.abs(x)))


def _ret_kernel(q_ref, k_ref, v_ref, g_ref, cos_ref, sin_ref, df_ref, db_ref, nw_ref, o_ref,
                mask_ref, qf_ref, qb_ref, kf_ref, kb_ref, dec_ref, qr_ref, kr_ref, acc_ref, *, chunk):
    seq, d = q_ref.shape
    n = seq // chunk
    c = chunk

    @pl.when(pl.program_id(1) == 0)
    def _():
        lgf = _log_sigmoid(df_ref[0])
        lgb = _log_sigmoid(db_ref[0])
        ii = lax.broadcasted_iota(I32, (c, c), 0)
        jj = lax.broadcasted_iota(I32, (c, c), 1)
        diff = (ii - jj).astype(F32)
        lgf_c = jnp.concatenate([lgf] * (c // d), axis=1)
        lgb_c = jnp.concatenate([lgb] * (c // d), axis=1)
        mask_ref[...] = jnp.where(diff >= 0.0,
                                  jnp.exp(lgf_c * jnp.maximum(diff, 0.0)),
                                  jnp.exp(lgb_c * jnp.maximum(-diff, 0.0)))
        pos = lax.broadcasted_iota(I32, (c, d), 0).astype(F32)
        qf_ref[...] = jnp.exp(lgf * (pos + 1.0))
        qb_ref[...] = jnp.exp(lgb * (c - pos))
        kf_ref[...] = jnp.exp(lgf * (c - 1.0 - pos))
        kb_ref[...] = jnp.exp(lgb * pos)
        dec_ref[0:1, :] = jnp.exp(lgf * c)
        dec_ref[1:2, :] = jnp.exp(lgb * c)

    scale = d ** -0.5
    nt = (((1,), (1,)), ((), ()))
    sls = [pl.ds(i * c, c) for i in range(n)]

    for sl in sls:
        cs = cos_ref[sl, :]
        sn = sin_ref[sl, :]
        q = q_ref[sl, :].astype(F32)
        k = k_ref[sl, :].astype(F32)
        qr_ref[sl, :] = q * cs + pltpu.roll(q, d // 2, 1) * sn
        kr_ref[sl, :] = (k * cs + pltpu.roll(k, d // 2, 1) * sn) * scale

    for sl in sls:
        s = lax.dot_general(qr_ref[sl, :].astype(BF16), kr_ref[sl, :].astype(BF16), nt,
                            preferred_element_type=F32)
        p = (s * mask_ref[...]).astype(BF16)
        acc_ref[sl, :] = jnp.dot(p, v_ref[sl, :], preferred_element_type=F32)

    def kv_state(sl, kw_ref):
        kw = (kr_ref[sl, :] * kw_ref[...]).T.astype(BF16)
        return jnp.dot(kw, v_ref[sl, :], preferred_element_type=F32)

    def scan(order, qw_ref, kw_ref, dec):
        st = jnp.zeros((d, d), F32)
        for idx, ci in enumerate(order):
            sl = sls[ci]
            if idx > 0:
                qw = (qr_ref[sl, :] * qw_ref[...]).astype(BF16)
                acc_ref[sl, :] += jnp.dot(qw, st.astype(BF16), preferred_element_type=F32)
            if idx < n - 1:
                st = dec * st + kv_state(sl, kw_ref)

    scan(list(range(n)), qf_ref, kf_ref, dec_ref[0:1, :])
    scan(list(range(n - 1, -1, -1)), qb_ref, kb_ref, dec_ref[1:2, :])

    nw = nw_ref[0]
    for sl in sls:
        o = acc_ref[sl, :]
        on = o * lax.rsqrt(jnp.mean(o * o, axis=-1, keepdims=True) + EPS) * nw
        g = g_ref[sl, :].astype(F32)
        o_ref[sl, :] = (_silu(g) * on).astype(o_ref.dtype)


def _retention(proj, cos, sin, dec_f, dec_b, ret_norm_w, batch, seq):
    h, d = RET_HEADS, HEAD_DIM
    c = min(RET_CHUNK, seq)
    col = lambda off: pl.BlockSpec((seq, d), lambda hh, b: (b, off + hh))
    per_head = pl.BlockSpec((1, 1, d), lambda hh, b: (hh, 0, 0))
    table = pl.BlockSpec((seq, d), lambda hh, b: (0, 0))
    vm = lambda shape, dt=F32: pltpu.VMEM(shape, dt)
    return pl.pallas_call(
        functools.partial(_ret_kernel, chunk=c), grid=(h, batch),
        in_specs=[col(0), col(h), col(2 * h), col(3 * h), table, table, per_head, per_head, per_head],
        out_specs=pl.BlockSpec((seq, d), lambda hh, b: (b, hh)),
        out_shape=jax.ShapeDtypeStruct((batch * seq, h * d), BF16),
        scratch_shapes=[vm((c, c)), vm((c, d)), vm((c, d)), vm((c, d)), vm((c, d)), vm((8, d)),
                        vm((seq, d)), vm((seq, d)), vm((seq, d))],
        compiler_params=_params(2), name="retention",
    )(proj, proj, proj, proj, cos, sin, dec_f, dec_b, ret_norm_w)


def _mix_kernel(og_ref, cb_ref, cc_ref, cu_ref, ccp_ref, cup_ref, ccn_ref, cun_ref,
                gr0_ref, gr1_ref, gc0_ref, gc1_ref, cw_ref, wr_ref, wc_ref, o_ref, *, tiles_per_seq):
    i = pl.program_id(0)
    tm, dr = cc_ref.shape
    u = cc_ref[...].astype(F32) * cu_ref[...].astype(F32)
    pos = i % tiles_per_seq
    last = BF16_SUBLANES - 1
    u_before = ccp_ref[last:last + 1, :].astype(F32) * cup_ref[last:last + 1, :].astype(F32)
    u_before = jnp.where(pos == 0, 0.0, u_before)
    u_after = ccn_ref[0:1, :].astype(F32) * cun_ref[0:1, :].astype(F32)
    u_after = jnp.where(pos == tiles_per_seq - 1, 0.0, u_after)
    row = lax.broadcasted_iota(I32, u.shape, 0)
    u_prev = jnp.where(row == 0, u_before, pltpu.roll(u, 1, 0))
    u_next = jnp.where(row == tm - 1, u_after, pltpu.roll(u, tm - 1, 0))
    cw = cw_ref[...]
    y = cw[0:1, :] * u_prev + cw[1:2, :] * u + cw[2:3, :] * u_next
    z = (cb_ref[...].astype(F32) * y).astype(BF16)

    yr = jnp.dot(og_ref[...], wr_ref[...], preferred_element_type=F32)
    yc = jnp.dot(z, wc_ref[...], preferred_element_type=F32)
    for lo, gr_ref, gc_ref in ((0, gr0_ref, gc0_ref), (dr, gr1_ref, gc1_ref)):
        merged = (_sigmoid(gr_ref[...].astype(F32)) * yr[:, lo:lo + dr]
                  + _sigmoid(gc_ref[...].astype(F32)) * yc[:, lo:lo + dr])
        o_ref[:, lo:lo + dr] = merged.astype(o_ref.dtype)


def _mix(og, proj, conv_w, w_ret_o_b, w_conv_o_b, seq):
    t, dr = og.shape
    d = w_ret_o_b.shape[1]
    assert d == 2 * dr
    tm = min(512, seq)
    tiles_per_seq = seq // tm
    hb = tm // BF16_SUBLANES
    n_hblk = t // BF16_SUBLANES
    wide = lambda off: pl.BlockSpec((tm, dr), lambda i: (i, off))
    before = lambda off: pl.BlockSpec((BF16_SUBLANES, dr), lambda i: (jnp.maximum(i * hb - 1, 0), off))
    after = lambda off: pl.BlockSpec((BF16_SUBLANES, dr), lambda i: (jnp.minimum((i + 1) * hb, n_hblk - 1), off))
    whole = lambda shape: pl.BlockSpec(shape, lambda i: (0, 0))
    return pl.pallas_call(
        functools.partial(_mix_kernel, tiles_per_seq=tiles_per_seq), grid=(t // tm,),
        in_specs=[wide(0), wide(4), wide(5), wide(6), before(5), before(6), after(5), after(6),
                  wide(7), wide(8), wide(9), wide(10),
                  whole((3, dr)), whole((dr, d)), whole((dr, d))],
        out_specs=pl.BlockSpec((tm, d), lambda i: (i, 0)),
        out_shape=jax.ShapeDtypeStruct((t, d), BF16),
        compiler_params=_params(1), name="mix",
    )(og, proj, proj, proj, proj, proj, proj, proj, proj, proj, proj, proj, conv_w, w_ret_o_b, w_conv_o_b)


def _outproj_kernel(m_ref, x_ref, mod_ref, nw_ref, w_ref, rwt_ref, x1_ref, hp_ref, lg_ref, words_ref):
    y = jnp.dot(m_ref[...], w_ref[...], preferred_element_type=F32)
    m = mod_ref[0]
    x1 = x_ref[...] + m[2:3, :] * y
    x1_ref[...] = x1
    hn = x1 * lax.rsqrt(jnp.mean(x1 * x1, axis=-1, keepdims=True) + EPS) * nw_ref[...]
    h = hn * (1.0 + m[4:5, :]) + m[3:4, :]
    _store_tokens(hp_ref, h, words_ref)
    h_hi = h.astype(BF16)
    h_lo = (h - h_hi.astype(F32)).astype(BF16)
    rw = rwt_ref[...]
    r_hi = rw.astype(BF16)
    r_lo = (rw - r_hi.astype(F32)).astype(BF16)
    nt = (((1,), (1,)), ((), ()))
    dot = functools.partial(lax.dot_general, dimension_numbers=nt, preferred_element_type=F32)
    lg_ref[...] = dot(r_hi, h_hi) + dot(r_hi, h_lo) + dot(r_lo, h_hi)


def _outproj(merged, x2, mod3, norm_w, w_out_b, router_wt, seq):
    t, d = x2.shape
    e = router_wt.shape[0]
    tm = min(512, seq)
    tiles_per_seq = seq // tm
    return pl.pallas_call(
        _outproj_kernel, grid=(t // tm,),
        in_specs=[pl.BlockSpec((tm, d), lambda i: (i, 0)),
                  pl.BlockSpec((tm, d), lambda i: (i, 0)),
                  pl.BlockSpec((1, 6, d), lambda i: (i // tiles_per_seq, 0, 0)),
                  pl.BlockSpec((1, d), lambda i: (0, 0)),
                  pl.BlockSpec((d, d), lambda i: (0, 0)),
                  pl.BlockSpec((e, d), lambda i: (0, 0))],
        out_specs=[pl.BlockSpec((tm, d), lambda i: (i, 0)),
                   pl.BlockSpec((TOKEN_ROWS * tm, LANES), lambda i: (i, 0)),
                   pl.BlockSpec((e, tm), lambda i: (0, i))],
        out_shape=[jax.ShapeDtypeStruct((t, d), F32),
                   jax.ShapeDtypeStruct((TOKEN_ROWS * t, LANES), BF16),
                   jax.ShapeDtypeStruct((e, t), F32)],
        scratch_shapes=[_words_scratch(tm)],
        compiler_params=_params(1), name="outproj",
    )(merged, x2, mod3, norm_w, w_out_b, router_wt)


def _first_max(x, iota, sentinel):
    m = jnp.max(x, axis=0, keepdims=True)
    idx = jnp.min(jnp.where(x == m, iota, sentinel), axis=0, keepdims=True)
    return m, idx, iota == idx


def _route_tile(lg, bias, carry, place_base):
    e, tk = lg.shape
    s = _sigmoid(lg)
    biased = s + bias
    sub = lax.broadcasted_iota(I32, (GROUP_SIZE, tk), 0)
    group_rows = []
    for g in range(N_GROUPS):
        xg = biased[g * GROUP_SIZE:(g + 1) * GROUP_SIZE, :]
        m1, _, pick = _first_max(xg, sub, GROUP_SIZE)
        m2 = jnp.max(jnp.where(pick, NEG_INF, xg), axis=0, keepdims=True)
        group_rows.append(m1 + m2)
    gs = jnp.concatenate(group_rows, axis=0)
    gsub = lax.broadcasted_iota(I32, (N_GROUPS, tk), 0)
    sel = jnp.zeros((N_GROUPS, tk), F32)
    for _ in range(TOPK_GROUPS):
        _, _, pick = _first_max(gs, gsub, N_GROUPS)
        sel = jnp.where(pick, 1.0, sel)
        gs = jnp.where(pick, NEG_INF, gs)
    masked_rows = []
    for g in range(N_GROUPS):
        xg = biased[g * GROUP_SIZE:(g + 1) * GROUP_SIZE, :]
        masked_rows.append(jnp.where(sel[g:g + 1, :] > 0.5, xg, NEG_INF))
    masked = jnp.concatenate(masked_rows, axis=0)

    eio = lax.broadcasted_iota(I32, (e, tk), 0)
    chosen = jnp.zeros((e, tk), F32)
    ids, top_s = [], []
    for _ in range(TOP_K):
        _, idx, pick = _first_max(masked, eio, e)
        ids.append(idx)
        top_s.append(jnp.sum(jnp.where(pick, s, 0.0), axis=0, keepdims=True))
        chosen = jnp.where(pick, 1.0, chosen)
        masked = jnp.where(pick, NEG_INF, masked)
    total = top_s[0]
    for ts in top_s[1:]:
        total = total + ts

    before = (lax.broadcasted_iota(I32, (tk, tk), 0) < lax.broadcasted_iota(I32, (tk, tk), 1))
    upper = jnp.where(before, 1.0, 0.0).astype(BF16)
    local = jnp.dot(chosen.astype(BF16), upper, preferred_element_type=F32)
    tile_cnt = jnp.broadcast_to(jnp.sum(chosen, axis=1, keepdims=True), carry.shape)
    below = (lax.broadcasted_iota(I32, (e, e), 1) < lax.broadcasted_iota(I32, (e, e), 0))
    lower = jnp.where(below, 1.0, 0.0).astype(BF16)
    first = jnp.dot(lower, tile_cnt.astype(BF16), preferred_element_type=F32)
    rank = local + carry[:, 0:1]
    place = (local + first[:, 0:1] + place_base) * TOKEN_ROWS
    weights = [ts / total * ROUTED_SCALE for ts in top_s]
    pick_row = lambda k, v: jnp.sum(jnp.where(eio == ids[k], v, 0.0), axis=0, keepdims=True).astype(I32)
    ranks = [pick_row(k, rank) for k in range(TOP_K)]
    places = [pick_row(k, place) for k in range(TOP_K)]
    return ids, weights, ranks, places, tile_cnt


def _route_kernel(lg_ref, bias_ref, ids_ref, w_ref, rank_ref, lidx_ref, cnt_ref, tcarry_ref, tcnt_ref, carry_ref,
                  *, tile):
    i = pl.program_id(0)

    @pl.when(i == 0)
    def _():
        carry_ref[...] = jnp.zeros_like(carry_ref)

    carry = carry_ref[...]
    for j in range(lg_ref.shape[1] // tile):
        cols = slice(j * tile, (j + 1) * tile)
        place_base = float((j % 2) * tile * TOP_K)
        ids, weights, ranks, places, tile_cnt = _route_tile(lg_ref[:, cols], bias_ref[...], carry, place_base)
        for k in range(TOP_K):
            ids_ref[k:k + 1, cols] = ids[k]
            w_ref[k:k + 1, cols] = weights[k]
            rank_ref[k:k + 1, cols] = ranks[k]
            lidx_ref[k:k + 1, cols] = places[k]
        tcarry_ref[j] = carry.astype(I32)
        tcnt_ref[j] = tile_cnt.astype(I32)
        carry = carry + tile_cnt
    carry_ref[...] = carry
    cnt_ref[...] = carry.astype(I32)


def _route(logits_t, bias_col, tile):
    e, t = logits_t.shape
    tk = min(4 * tile, t)
    per_step = tk // tile
    assert per_step % 2 == 0 or t == tile
    row8 =lambda dt: jax.ShapeDtypeStruct((TOP_K, t), dt)
    blk8 = pl.BlockSpec((TOP_K, tk), lambda i: (0, i))
    per_tile = pl.BlockSpec((per_step, e, LANES), lambda i: (i, 0, 0))
    return pl.pallas_call(
        functools.partial(_route_kernel, tile=tile), grid=(t // tk,),
        in_specs=[pl.BlockSpec((e, tk), lambda i: (0, i)),
                  pl.BlockSpec((e, 1), lambda i: (0, 0))],
        out_specs=[blk8, blk8, blk8, blk8, pl.BlockSpec((e, LANES), lambda i: (0, 0)), per_tile, per_tile],
        out_shape=[row8(I32), row8(F32), row8(I32), row8(I32), jax.ShapeDtypeStruct((e, LANES), I32),
                   jax.ShapeDtypeStruct((t // tile, e, LANES), I32), jax.ShapeDtypeStruct((t // tile, e, LANES), I32)],
        scratch_shapes=[pltpu.VMEM((e, LANES), F32)],
        compiler_params=_params(1), name="route",
    )(logits_t, bias_col)


def _dispatch_kernel(pad_ref, cnt_ref, hp_ref, slot_ref, wsg_ref, wsu_ref, wsd_ref, xs_ref, sh_ref,
                     zero_ref, words_ref, sem, zsem, *, bm):
    i = pl.program_id(0)
    td = hp_ref.shape[0] // TOKEN_ROWS
    n_rows = xs_ref.shape[0] // TOKEN_ROWS
    n_blocks = n_rows // bm
    block = bm * TOKEN_ROWS

    @pl.when(i == 0)
    def _():
        zero_ref[...] = jnp.zeros_like(zero_ref)
        last_e = N_EXPERTS - 1
        used_blocks = pad_ref[last_e] // bm + (cnt_ref[last_e] + bm - 1) // bm

        def tail_copy(j):
            return pltpu.make_async_copy(zero_ref, xs_ref.at[pl.ds(pl.multiple_of(j * block, block), block)], zsem)

        def tail(j, c, wait):
            cp = tail_copy(j)
            cp.wait() if wait else cp.start()
            return c

        def pad(e, c, wait):
            row = pad_ref[e] + cnt_ref[e]
            n = (cnt_ref[e] + bm - 1) // bm * bm - cnt_ref[e]
            bit = bm // 2
            while bit:
                take = (n & bit) != 0
                size = bit * TOKEN_ROWS
                cp = pltpu.make_async_copy(
                    zero_ref.at[pl.ds(0, size)],
                    xs_ref.at[pl.ds(pl.multiple_of(row * TOKEN_ROWS, TOKEN_ROWS), size)], zsem)

                @pl.when(take)
                def _():
                    cp.wait() if wait else cp.start()

                row = row + jnp.where(take, bit, 0)
                bit //= 2
            return c

        for wait in (False, True):
            lax.fori_loop(used_blocks, n_blocks, functools.partial(tail, wait=wait), 0)
            lax.fori_loop(0, N_EXPERTS, functools.partial(pad, wait=wait), 0)

    def body(t, c):
        for k in range(TOP_K):
            slot = slot_ref[t * TOP_K + k]
            pltpu.make_async_copy(hp_ref.at[_token(t)], xs_ref.at[_token(slot)], sem).start(priority=k % 2)
        return c

    lax.fori_loop(0, td, body, 0)

    h = _load_tokens(hp_ref[...], words_ref).astype(BF16)
    sg = jnp.dot(h, wsg_ref[...], preferred_element_type=F32)
    su = jnp.dot(h, wsu_ref[...], preferred_element_type=F32)
    shared = jnp.dot((_silu(sg) * su).astype(BF16), wsd_ref[...], preferred_element_type=F32)
    sh_ref[...] = shared.astype(sh_ref.dtype)

    for k in range(TOP_K):
        pltpu.make_async_copy(hp_ref, xs_ref.at[pl.ds(0, td * TOKEN_ROWS)], sem).wait()


def _dispatch(pad_start, counts, hp, slots, wsg_b, wsu_b, wsd_b, n_rows, bm):
    t = hp.shape[0] // TOKEN_ROWS
    d, ds = wsg_b.shape
    td = min(512, t)
    whole = lambda shape: pl.BlockSpec(shape, lambda i, *_: (0, 0))
    grid_spec = pltpu.PrefetchScalarGridSpec(
        num_scalar_prefetch=2, grid=(t // td,),
        in_specs=[pl.BlockSpec((td * TOKEN_ROWS, LANES), lambda i, *_: (i, 0)),
                  pl.BlockSpec((td * TOP_K,), lambda i, *_: (i,), memory_space=pltpu.SMEM),
                  whole((d, ds)), whole((d, ds)), whole((ds, d))],
        out_specs=[pl.BlockSpec(memory_space=pl.ANY),
                   pl.BlockSpec((td, d), lambda i, *_: (i, 0))],
        scratch_shapes=[pltpu.VMEM((bm * TOKEN_ROWS, LANES), BF16), _words_scratch(td),
                        pltpu.SemaphoreType.DMA(()), pltpu.SemaphoreType.DMA(())])
    return pl.pallas_call(
        functools.partial(_dispatch_kernel, bm=bm), grid_spec=grid_spec,
        out_shape=[jax.ShapeDtypeStruct((n_rows * TOKEN_ROWS, LANES), BF16),
                   jax.ShapeDtypeStruct((t, d), BF16)],
        compiler_params=_params(1), name="dispatch",
    )(pad_start, counts, hp, slots, wsg_b, wsu_b, wsd_b)


def _expert_kernel(be_ref, first_ref, slot_ref, nxt_ref, rows_ref, nu_ref, x_ref, wg_hbm, wu_hbm, wd_hbm, o_ref,
                   words_ref, wg32, wu32, wd32, wg16, wu16, wd16, wsem):
    i = pl.program_id(0)

    def weight_copies(e, s):
        return (pltpu.make_async_copy(wg_hbm.at[e], wg32.at[s], wsem.at[s, 0]),
                pltpu.make_async_copy(wu_hbm.at[e], wu32.at[s], wsem.at[s, 1]),
                pltpu.make_async_copy(wd_hbm.at[e], wd32.at[s], wsem.at[s, 2]))

    @pl.when(i == 0)
    def _():
        for cp in weight_copies(be_ref[0], 0):
            cp.start()

    @pl.when(jnp.logical_and(first_ref[i] == 1, i < nu_ref[0]))
    def _():
        s = slot_ref[i]
        for cp in weight_copies(be_ref[i], s):
            cp.wait()

        @pl.when(nxt_ref[i] >= 0)
        def _():
            for cp in weight_copies(nxt_ref[i], 1 - s):
                cp.start()

        wg16[...] = wg32[s].astype(BF16)
        wu16[...] = wu32[s].astype(BF16)
        wd16[...] = wd32[s].astype(BF16)

    bm = x_ref.shape[0] // TOKEN_ROWS
    quarter = bm // EXPERT_BLOCK_PARTS
    used = jnp.where(i < nu_ref[0], (rows_ref[i] + quarter - 1) // quarter, 0)

    def swiglu_rows(n):
        x = _load_tokens(x_ref[pl.ds(0, n * TOKEN_ROWS), :], words_ref).astype(BF16)
        g = jnp.dot(x, wg16[...], preferred_element_type=F32)
        u = jnp.dot(x, wu16[...], preferred_element_type=F32)
        mid = (_silu(g) * u).astype(BF16)
        _store_tokens(o_ref, jnp.dot(mid, wd16[...], preferred_element_type=F32), words_ref)
        if n < bm:
            rest = (bm - n) * TOKEN_ROWS
            o_ref[pl.ds(n * TOKEN_ROWS, rest), :] = jnp.zeros((rest, LANES), o_ref.dtype)

    for parts in range(1, EXPERT_BLOCK_PARTS + 1):
        pl.when(used == parts)(functools.partial(swiglu_rows, parts * quarter))

    @pl.when(used == 0)
    def _():
        o_ref[...] = jnp.zeros_like(o_ref)


def _experts(blk_e, blk_first, blk_slot, blk_next, blk_rows, n_used, xs, w_gate, w_up, w_down, n_blocks, bm):
    _, d, de = w_gate.shape
    x_map = lambda i, be, fi, sl, nx, ro, nu: (jnp.minimum(i, nu[0] - 1), 0)
    hbm = pl.BlockSpec(memory_space=pl.ANY)
    grid_spec = pltpu.PrefetchScalarGridSpec(
        num_scalar_prefetch=6, grid=(n_blocks,),
        in_specs=[pl.BlockSpec((bm * TOKEN_ROWS, LANES), x_map), hbm, hbm, hbm],
        out_specs=pl.BlockSpec((bm * TOKEN_ROWS, LANES), lambda i, *_: (i, 0)),
        scratch_shapes=[_words_scratch(bm),
                        pltpu.VMEM((2, d, de), F32), pltpu.VMEM((2, d, de), F32), pltpu.VMEM((2, de, d), F32),
                        pltpu.VMEM((d, de), BF16), pltpu.VMEM((d, de), BF16), pltpu.VMEM((de, d), BF16),
                        pltpu.SemaphoreType.DMA((2, 3))])
    return pl.pallas_call(
        _expert_kernel, grid_spec=grid_spec,
        out_shape=jax.ShapeDtypeStruct((n_blocks * bm * TOKEN_ROWS, LANES), BF16),
        compiler_params=_params(1), name="experts",
    )(blk_e, blk_first, blk_slot, blk_next, blk_rows, n_used, xs, w_gate, w_up, w_down)


def _combine_kernel(gstart_ref, tcnt_ref, x1_ref, sh_ref, lidx_ref, w_ref, mod_ref, fw_ref, ys_ref, o_ref,
                    stage, acc_ref, sem, *, n_tiles):
    i = pl.program_id(0)
    tc = x1_ref.shape[0]
    n_e = N_EXPERTS
    tile_rows = tc * TOP_K

    def fetch(tile, buf):
        def per_expert(e, off):
            count = tcnt_ref[tile * n_e + e]
            row = gstart_ref[tile * n_e + e]
            bit = tc
            while bit:
                take = (count & bit) != 0
                size = bit * TOKEN_ROWS
                cp = pltpu.make_async_copy(
                    ys_ref.at[pl.ds(pl.multiple_of(row * TOKEN_ROWS, TOKEN_ROWS), size)],
                    stage.at[pl.ds(pl.multiple_of(off * TOKEN_ROWS, TOKEN_ROWS), size)], sem.at[buf])
                pl.when(take)(cp.start)
                step = jnp.where(take, bit, 0)
                row, off = row + step, off + step
                bit //= 2
            return off
        lax.fori_loop(0, n_e, per_expert, buf * tile_rows)

    @pl.when(i == 0)
    def _():
        fetch(0, 0)

    @pl.when(i + 1 < n_tiles)
    def _():
        fetch(i + 1, (i + 1) % 2)

    buf = i % 2
    half = pl.ds(pl.multiple_of(buf * tile_rows * TOKEN_ROWS, TOKEN_ROWS), tile_rows * TOKEN_ROWS)
    pltpu.make_async_copy(ys_ref.at[pl.ds(0, tile_rows * TOKEN_ROWS)], stage.at[half], sem.at[buf]).wait()

    def per_token(t, c):
        lo = hi = None
        for k in range(TOP_K):
            place = lidx_ref[t * TOP_K + k]
            wk = w_ref[t * TOP_K + k]
            words = pltpu.bitcast(stage[pl.ds(pl.multiple_of(place, TOKEN_ROWS), TOKEN_ROWS), :], jnp.uint32)
            a = wk * pltpu.unpack_elementwise(words, index=0, packed_dtype=BF16, unpacked_dtype=F32)
            b = wk * pltpu.unpack_elementwise(words, index=1, packed_dtype=BF16, unpacked_dtype=F32)
            lo, hi = (a, b) if k == 0 else (lo + a, hi + b)
        rows = pl.ds(pl.multiple_of(t * TOKEN_WORD_ROWS, TOKEN_WORD_ROWS), TOKEN_WORD_ROWS)
        acc_ref[0, rows, :] = lo
        acc_ref[1, rows, :] = hi
        return c

    lax.fori_loop(0, tc, per_token, 0, unroll=2)

    halves = [jnp.concatenate([acc_ref[half, pl.ds(s, tc, stride=TOKEN_WORD_ROWS), :]
                               for s in range(TOKEN_WORD_ROWS)], axis=1) for half in range(2)]
    routed = jnp.concatenate(halves, axis=1)
    x = x1_ref[...] + mod_ref[0][5:6, :] * (routed + sh_ref[...].astype(F32))
    o_ref[...] = x * lax.rsqrt(jnp.mean(x * x, axis=-1, keepdims=True) + EPS) * fw_ref[...]


def _combine(gstart, tcnt, x1, shared, lidx, w_flat, mod3, final_w, ys, seq, tc):
    t, d = x1.shape
    n_tiles = t // tc
    tiles_per_seq = seq // tc
    per_tile = lambda: pl.BlockSpec((tc * TOP_K,), lambda i, *_: (i,), memory_space=pltpu.SMEM)
    grid_spec = pltpu.PrefetchScalarGridSpec(
        num_scalar_prefetch=2, grid=(n_tiles,),
        in_specs=[pl.BlockSpec((tc, d), lambda i, *_: (i, 0)),
                  pl.BlockSpec((tc, d), lambda i, *_: (i, 0)),
                  per_tile(), per_tile(),
                  pl.BlockSpec((1, 6, d), lambda i, *_: (i // tiles_per_seq, 0, 0)),
                  pl.BlockSpec((1, d), lambda i, *_: (0, 0)),
                  pl.BlockSpec(memory_space=pl.ANY)],
        out_specs=pl.BlockSpec((tc, d), lambda i, *_: (i, 0)),
        scratch_shapes=[pltpu.VMEM((2 * tc * TOP_K * TOKEN_ROWS, LANES), BF16),
                        pltpu.VMEM((2, tc * TOKEN_WORD_ROWS, LANES), F32),
                        pltpu.SemaphoreType.DMA((2,))])
    return pl.pallas_call(
        functools.partial(_combine_kernel, n_tiles=n_tiles), grid_spec=grid_spec,
        out_shape=jax.ShapeDtypeStruct((t, d), F32),
        compiler_params=_params(1), name="combine",
    )(gstart, tcnt, x1, shared, lidx, w_flat, mod3, final_w, ys)


def _rope_tables(seq, d):
    half = d // 2
    inv = ROPE_BASE ** (-jnp.arange(half, dtype=F32) / half)
    ang = jnp.arange(seq, dtype=F32)[:, None] * inv[None, :]
    cos, sin = jnp.cos(ang), jnp.sin(ang)
    return jnp.concatenate([cos, cos], axis=1), jnp.concatenate([-sin, sin], axis=1)


def kernel(x, c, w_ada, b_ada, norm1_w, w_in, ret_decay_fwd, ret_decay_bwd, ret_norm_w, w_ret_o, conv_w,
           w_conv_o, w_out, norm2_w, router_w, router_bias, w_gate, w_up, w_down, ws_gate, ws_up, ws_down,
           final_norm_w):
    batch, seq, d = x.shape
    depth = w_ada.shape[0]
    t = batch * seq
    bm = EXPERT_BLOCK
    n_blocks = (t * TOP_K + N_EXPERTS * (bm - 1)) // bm
    n_rows = n_blocks * bm
    cos, sin = _rope_tables(seq, HEAD_DIM)
    c_pad = jnp.pad(c, ((0, BF16_SUBLANES - batch % BF16_SUBLANES), (0, 0)))
    x2 = x.reshape(t, d)

    assert depth == 1, "the final norm is fused into the last stage of a single layer"
    for l in range(depth):
        mod = _ada(c_pad, w_ada[l], b_ada[l][None, :])[:batch]
        mod3 = mod.reshape(batch, 6, d)
        proj = _inproj(x2, mod3, norm1_w[l][None, :], w_in[l].astype(BF16), seq)
        lane_bcast = lambda v: jnp.broadcast_to(v[:, None, None], (RET_HEADS, 1, HEAD_DIM))
        og = _retention(proj, cos, sin, lane_bcast(ret_decay_fwd[l]), lane_bcast(ret_decay_bwd[l]),
                        ret_norm_w[l].reshape(RET_HEADS, 1, HEAD_DIM), batch, seq)
        merged = _mix(og, proj, conv_w[l], w_ret_o[l].astype(BF16), w_conv_o[l].astype(BF16), seq)
        x1, hp, logits_t = _outproj(merged, x2, mod3, norm2_w[l][None, :], w_out[l].astype(BF16),
                                    router_w[l].T, seq)
        ids_t, w_t, rank_t, lidx_t, cnt, tile_carry, tile_cnt = _route(logits_t, router_bias[l][:, None], COMBINE_TILE)

        counts = cnt[:, 0]
        nblk = (counts + bm - 1) // bm
        blk_end = jnp.cumsum(nblk)
        pad_start = ((blk_end - nblk) * bm).astype(I32)
        n_used = blk_end[-1:].astype(I32)
        blk_ids = jnp.arange(n_blocks, dtype=I32)
        blk_e = jnp.minimum(jnp.sum((blk_ids[:, None] >= blk_end[None, :]).astype(I32), axis=1),
                            N_EXPERTS - 1)

        blk_first = jnp.concatenate([jnp.ones((1,), I32), (blk_e[1:] != blk_e[:-1]).astype(I32)])
        blk_slot = (jnp.cumsum(blk_first) - 1) % 2
        after = blk_end[blk_e]
        blk_next = jnp.where(after < n_used[0], blk_e[jnp.minimum(after, n_blocks - 1)], -1).astype(I32)

        onehot = ids_t[:, :, None] == jnp.arange(N_EXPERTS, dtype=I32)
        slots_t = rank_t + jnp.sum(jnp.where(onehot, pad_start, 0), axis=-1)
        slots = slots_t.T.reshape(t * TOP_K)
        xs, shared = _dispatch(pad_start, counts, hp, slots, ws_gate[l].astype(BF16), ws_up[l].astype(BF16),
                               ws_down[l].astype(BF16), n_rows, bm)
        blk_rows = jnp.clip(counts[blk_e] - (blk_ids * bm - pad_start[blk_e]), 0, bm).astype(I32)
        ys = _experts(blk_e, blk_first, blk_slot.astype(I32), blk_next, blk_rows, n_used, xs,
                      w_gate[l], w_up[l], w_down[l], n_blocks, bm)
        gstart = (pad_start[None, :] + tile_carry[:, :, 0]).reshape(-1)
        x2 = _combine(gstart, tile_cnt[:, :, 0].reshape(-1), x1, shared, lidx_t.T.reshape(t * TOP_K),
                      w_t.T.reshape(t * TOP_K), mod3, final_norm_w[None, :], ys, seq, COMBINE_TILE)
    return x2.reshape(batch, seq, d)
```

```python
import functools

import jax
import jax.numpy as jnp
from jax import lax
from jax.experimental import pallas as pl
from jax.experimental.pallas import tpu as pltpu

F32 = jnp.float32
BF16 = jnp.bfloat16
I32 = jnp.int32

EPS = 1e-6
RET_HEADS = 8
HEAD_DIM = 128
ROPE_BASE = 10000.0
N_EXPERTS = 64
TOP_K = 8
N_GROUPS = 8
TOPK_GROUPS = 4
GROUP_SIZE = N_EXPERTS // N_GROUPS
ROUTED_SCALE = 2.5

V7X_VMEM_BYTES = 64 * 1024 * 1024
VMEM_LIMIT = V7X_VMEM_BYTES - 8 * 1024 * 1024
BF16_SUBLANES = 16

RET_CHUNK = 256
EXPERT_BLOCK = 512
EXPERT_BLOCK_PARTS = 4
COMBINE_TILE = 256
NEG_INF = float("-inf")


def _params(n_axes):
    return pltpu.CompilerParams(dimension_semantics=("arbitrary",) * n_axes,
                                vmem_limit_bytes=VMEM_LIMIT)


def _sigmoid(x):
    return 1.0 / (1.0 + jnp.exp(-x))


def _silu(x):
    return x * _sigmoid(x)


LANES = 128
TOKEN_WORD_ROWS = 8
TOKEN_ROWS = 2 * TOKEN_WORD_ROWS


def _store_tokens(ref, x, words_ref):
    n, m = x.shape[0], x.shape[1] // 2
    assert m == TOKEN_WORD_ROWS * LANES
    packed = pltpu.pack_elementwise([x[:, :m], x[:, m:]], packed_dtype=BF16)
    for s in range(TOKEN_WORD_ROWS):
        words_ref[pl.ds(s, n, stride=TOKEN_WORD_ROWS), :] = packed[:, s * LANES:(s + 1) * LANES]
    ref[pl.ds(0, n * TOKEN_ROWS), :] = pltpu.bitcast(words_ref[pl.ds(0, n * TOKEN_WORD_ROWS), :], BF16)


def _load_tokens(tiles, words_ref):
    n = tiles.shape[0] // TOKEN_ROWS
    words_ref[pl.ds(0, n * TOKEN_WORD_ROWS), :] = pltpu.bitcast(tiles, jnp.uint32)
    p = jnp.concatenate([words_ref[pl.ds(s, n, stride=TOKEN_WORD_ROWS), :]
                         for s in range(TOKEN_WORD_ROWS)], axis=1)
    a = pltpu.unpack_elementwise(p, index=0, packed_dtype=BF16, unpacked_dtype=F32)
    b = pltpu.unpack_elementwise(p, index=1, packed_dtype=BF16, unpacked_dtype=F32)
    return jnp.concatenate([a, b], axis=1)


def _token(row):
    return pl.ds(pl.multiple_of(row * TOKEN_ROWS, TOKEN_ROWS), TOKEN_ROWS)


def _words_scratch(n_tokens):
    return pltpu.VMEM((n_tokens * TOKEN_WORD_ROWS, LANES), jnp.uint32)


def _ada_kernel(c_ref, w_ref, b_ref, o_ref):
    s = _silu(c_ref[...]).astype(BF16)
    o_ref[...] = jnp.dot(s, w_ref[...].astype(BF16), preferred_element_type=F32) + b_ref[...]


def _ada(c_pad, w_ada, b_ada):
    m, d = c_pad.shape
    n = w_ada.shape[1]
    tn = 1024
    return pl.pallas_call(
        _ada_kernel, grid=(n // tn,),
        in_specs=[pl.BlockSpec((m, d), lambda j: (0, 0)),
                  pl.BlockSpec((d, tn), lambda j: (0, j)),
                  pl.BlockSpec((1, tn), lambda j: (0, j))],
        out_specs=pl.BlockSpec((m, tn), lambda j: (0, j)),
        out_shape=jax.ShapeDtypeStruct((m, n), F32),
        compiler_params=_params(1), name="ada")(c_pad, w_ada, b_ada)


def _inproj_kernel(x_ref, mod_ref, nw_ref, w_ref, o_ref, h_ref):
    @pl.when(pl.program_id(1) == 0)
    def _():
        x = x_ref[...]
        y = x * lax.rsqrt(jnp.mean(x * x, axis=-1, keepdims=True) + EPS) * nw_ref[...]
        m = mod_ref[0]
        h_ref[...] = (y * (1.0 + m[1:2, :]) + m[0:1, :]).astype(BF16)

    o_ref[...] = jnp.dot(h_ref[...], w_ref[...].astype(BF16), preferred_element_type=F32).astype(o_ref.dtype)


def _inproj(x2, mod3, norm_w, w_in, seq):
    t, d = x2.shape
    n = w_in.shape[1]
    tm, tn = min(1024, seq), 1024
    tiles_per_seq = seq // tm
    return pl.pallas_call(
        _inproj_kernel, grid=(t // tm, n // tn),
        in_specs=[pl.BlockSpec((tm, d), lambda i, j: (i, 0)),
                  pl.BlockSpec((1, 6, d), lambda i, j: (i // tiles_per_seq, 0, 0)),
                  pl.BlockSpec((1, d), lambda i, j: (0, 0)),
                  pl.BlockSpec((d, tn), lambda i, j: (0, j))],
        out_specs=pl.BlockSpec((tm, tn), lambda i, j: (i, j)),
        out_shape=jax.ShapeDtypeStruct((t, n), BF16),
        scratch_shapes=[pltpu.VMEM((tm, d), BF16)],
        compiler_params=_params(2), name="inproj")(x2, mod3, norm_w, w_in)


def _log_sigmoid(x):
    return jnp.minimum(x, 0.0) - jnp.log1p(jnp.exp(-jnp.abs(x)))


def _ret_kernel(q_ref, k_ref, v_ref, g_ref, cos_ref, sin_ref, df_ref, db_ref, nw_ref, o_ref,
                mask_ref, qf_ref, qb_ref, kf_ref, kb_ref, dec_ref, qr_ref, kr_ref, acc_ref, *, chunk):
    seq, d = q_ref.shape
    n = seq // chunk
    c = chunk

    @pl.when(pl.program_id(1) == 0)
    def _():
        lgf = _log_sigmoid(df_ref[0])
        lgb = _log_sigmoid(db_ref[0])
        ii = lax.broadcasted_iota(I32, (c, c), 0)
        jj = lax.broadcasted_iota(I32, (c, c), 1)
        diff = (ii - jj).astype(F32)
        lgf_c = jnp.concatenate([lgf] * (c // d), axis=1)
        lgb_c = jnp.concatenate([lgb] * (c // d), axis=1)
        mask_ref[...] = jnp.where(diff >= 0.0,
                                  jnp.exp(lgf_c * jnp.maximum(diff, 0.0)),
                                  jnp.exp(lgb_c * jnp.maximum(-diff, 0.0)))
        pos = lax.broadcasted_iota(I32, (c, d), 0).astype(F32)
        qf_ref[...] = jnp.exp(lgf * (pos + 1.0))
        qb_ref[...] = jnp.exp(lgb * (c - pos))
        kf_ref[...] = jnp.exp(lgf * (c - 1.0 - pos))
        kb_ref[...] = jnp.exp(lgb * pos)
        dec_ref[0:1, :] = jnp.exp(lgf * c)
        dec_ref[1:2, :] = jnp.exp(lgb * c)

    scale = d ** -0.5
    nt = (((1,), (1,)), ((), ()))
    sls = [pl.ds(i * c, c) for i in range(n)]

    for sl in sls:
        cs = cos_ref[sl, :]
        sn = sin_ref[sl, :]
        q = q_ref[sl, :].astype(F32)
        k = k_ref[sl, :].astype(F32)
        qr_ref[sl, :] = q * cs + pltpu.roll(q, d // 2, 1) * sn
        kr_ref[sl, :] = (k * cs + pltpu.roll(k, d // 2, 1) * sn) * scale

    for sl in sls:
        s = lax.dot_general(qr_ref[sl, :].astype(BF16), kr_ref[sl, :].astype(BF16), nt,
                            preferred_element_type=F32)
        p = (s * mask_ref[...]).astype(BF16)
        acc_ref[sl, :] = jnp.dot(p, v_ref[sl, :], preferred_element_type=F32)

    def kv_state(sl, kw_ref):
        kw = (kr_ref[sl, :] * kw_ref[...]).T.astype(BF16)
        return jnp.dot(kw, v_ref[sl, :], preferred_element_type=F32)

    def scan(order, qw_ref, kw_ref, dec):
        st = jnp.zeros((d, d), F32)
        for idx, ci in enumerate(order):
            sl = sls[ci]
            if idx > 0:
                qw = (qr_ref[sl, :] * qw_ref[...]).astype(BF16)
                acc_ref[sl, :] += jnp.dot(qw, st.astype(BF16), preferred_element_type=F32)
            if idx < n - 1:
                st = dec * st + kv_state(sl, kw_ref)

    scan(list(range(n)), qf_ref, kf_ref, dec_ref[0:1, :])
    scan(list(range(n - 1, -1, -1)), qb_ref, kb_ref, dec_ref[1:2, :])

    nw = nw_ref[0]
    for sl in sls:
        o = acc_ref[sl, :]
        on = o * lax.rsqrt(jnp.mean(o * o, axis=-1, keepdims=True) + EPS) * nw
        g = g_ref[sl, :].astype(F32)
        o_ref[sl, :] = (_silu(g) * on).astype(o_ref.dtype)


def _retention(proj, cos, sin, dec_f, dec_b, ret_norm_w, batch, seq):
    h, d = RET_HEADS, HEAD_DIM
    c = min(RET_CHUNK, seq)
    col = lambda off: pl.BlockSpec((seq, d), lambda hh, b: (b, off + hh))
    per_head = pl.BlockSpec((1, 1, d), lambda hh, b: (hh, 0, 0))
    table = pl.BlockSpec((seq, d), lambda hh, b: (0, 0))
    vm = lambda shape, dt=F32: pltpu.VMEM(shape, dt)
    return pl.pallas_call(
        functools.partial(_ret_kernel, chunk=c), grid=(h, batch),
        in_specs=[col(0), col(h), col(2 * h), col(3 * h), table, table, per_head, per_head, per_head],
        out_specs=pl.BlockSpec((seq, d), lambda hh, b: (b, hh)),
        out_shape=jax.ShapeDtypeStruct((batch * seq, h * d), BF16),
        scratch_shapes=[vm((c, c)), vm((c, d)), vm((c, d)), vm((c, d)), vm((c, d)), vm((8, d)),
                        vm((seq, d)), vm((seq, d)), vm((seq, d))],
        compiler_params=_params(2), name="retention",
    )(proj, proj, proj, proj, cos, sin, dec_f, dec_b, ret_norm_w)


def _mix_kernel(og_ref, cb_ref, cc_ref, cu_ref, ccp_ref, cup_ref, ccn_ref, cun_ref,
                gr0_ref, gr1_ref, gc0_ref, gc1_ref, cw_ref, wr_ref, wc_ref, o_ref, *, tiles_per_seq):
    i = pl.program_id(0)
    tm, dr = cc_ref.shape
    u = cc_ref[...].astype(F32) * cu_ref[...].astype(F32)
    pos = i % tiles_per_seq
    last = BF16_SUBLANES - 1
    u_before = ccp_ref[last:last + 1, :].astype(F32) * cup_ref[last:last + 1, :].astype(F32)
    u_before = jnp.where(pos == 0, 0.0, u_before)
    u_after = ccn_ref[0:1, :].astype(F32) * cun_ref[0:1, :].astype(F32)
    u_after = jnp.where(pos == tiles_per_seq - 1, 0.0, u_after)
    row = lax.broadcasted_iota(I32, u.shape, 0)
    u_prev = jnp.where(row == 0, u_before, pltpu.roll(u, 1, 0))
    u_next = jnp.where(row == tm - 1, u_after, pltpu.roll(u, tm - 1, 0))
    cw = cw_ref[...]
    y = cw[0:1, :] * u_prev + cw[1:2, :] * u + cw[2:3, :] * u_next
    z = (cb_ref[...].astype(F32) * y).astype(BF16)

    yr = jnp.dot(og_ref[...], wr_ref[...], preferred_element_type=F32)
    yc = jnp.dot(z, wc_ref[...], preferred_element_type=F32)
    for lo, gr_ref, gc_ref in ((0, gr0_ref, gc0_ref), (dr, gr1_ref, gc1_ref)):
        merged = (_sigmoid(gr_ref[...].astype(F32)) * yr[:, lo:lo + dr]
                  + _sigmoid(gc_ref[...].astype(F32)) * yc[:, lo:lo + dr])
        o_ref[:, lo:lo + dr] = merged.astype(o_ref.dtype)


def _mix(og, proj, conv_w, w_ret_o_b, w_conv_o_b, seq):
    t, dr = og.shape
    d = w_ret_o_b.shape[1]
    assert d == 2 * dr
    tm = min(512, seq)
    tiles_per_seq = seq // tm
    hb = tm // BF16_SUBLANES
    n_hblk = t // BF16_SUBLANES
    wide = lambda off: pl.BlockSpec((tm, dr), lambda i: (i, off))
    before = lambda off: pl.BlockSpec((BF16_SUBLANES, dr), lambda i: (jnp.maximum(i * hb - 1, 0), off))
    after = lambda off: pl.BlockSpec((BF16_SUBLANES, dr), lambda i: (jnp.minimum((i + 1) * hb, n_hblk - 1), off))
    whole = lambda shape: pl.BlockSpec(shape, lambda i: (0, 0))
    return pl.pallas_call(
        functools.partial(_mix_kernel, tiles_per_seq=tiles_per_seq), grid=(t // tm,),
        in_specs=[wide(0), wide(4), wide(5), wide(6), before(5), before(6), after(5), after(6),
                  wide(7), wide(8), wide(9), wide(10),
                  whole((3, dr)), whole((dr, d)), whole((dr, d))],
        out_specs=pl.BlockSpec((tm, d), lambda i: (i, 0)),
        out_shape=jax.ShapeDtypeStruct((t, d), BF16),
        compiler_params=_params(1), name="mix",
    )(og, proj, proj, proj, proj, proj, proj, proj, proj, proj, proj, proj, conv_w, w_ret_o_b, w_conv_o_b)


def _outproj_kernel(m_ref, x_ref, mod_ref, nw_ref, w_ref, rwt_ref, x1_ref, hp_ref, lg_ref, words_ref):
    y = jnp.dot(m_ref[...], w_ref[...], preferred_element_type=F32)
    m = mod_ref[0]
    x1 = x_ref[...] + m[2:3, :] * y
    x1_ref[...] = x1
    hn = x1 * lax.rsqrt(jnp.mean(x1 * x1, axis=-1, keepdims=True) + EPS) * nw_ref[...]
    h = hn * (1.0 + m[4:5, :]) + m[3:4, :]
    _store_tokens(hp_ref, h, words_ref)
    h_hi = h.astype(BF16)
    h_lo = (h - h_hi.astype(F32)).astype(BF16)
    rw = rwt_ref[...]
    r_hi = rw.astype(BF16)
    r_lo = (rw - r_hi.astype(F32)).astype(BF16)
    nt = (((1,), (1,)), ((), ()))
    dot = functools.partial(lax.dot_general, dimension_numbers=nt, preferred_element_type=F32)
    lg_ref[...] = dot(r_hi, h_hi) + dot(r_hi, h_lo) + dot(r_lo, h_hi)


def _outproj(merged, x2, mod3, norm_w, w_out_b, router_wt, seq):
    t, d = x2.shape
    e = router_wt.shape[0]
    tm = min(512, seq)
    tiles_per_seq = seq // tm
    return pl.pallas_call(
        _outproj_kernel, grid=(t // tm,),
        in_specs=[pl.BlockSpec((tm, d), lambda i: (i, 0)),
                  pl.BlockSpec((tm, d), lambda i: (i, 0)),
                  pl.BlockSpec((1, 6, d), lambda i: (i // tiles_per_seq, 0, 0)),
                  pl.BlockSpec((1, d), lambda i: (0, 0)),
                  pl.BlockSpec((d, d), lambda i: (0, 0)),
                  pl.BlockSpec((e, d), lambda i: (0, 0))],
        out_specs=[pl.BlockSpec((tm, d), lambda i: (i, 0)),
                   pl.BlockSpec((TOKEN_ROWS * tm, LANES), lambda i: (i, 0)),
                   pl.BlockSpec((e, tm), lambda i: (0, i))],
        out_shape=[jax.ShapeDtypeStruct((t, d), F32),
                   jax.ShapeDtypeStruct((TOKEN_ROWS * t, LANES), BF16),
                   jax.ShapeDtypeStruct((e, t), F32)],
        scratch_shapes=[_words_scratch(tm)],
        compiler_params=_params(1), name="outproj",
    )(merged, x2, mod3, norm_w, w_out_b, router_wt)


def _first_max(x, iota, sentinel):
    m = jnp.max(x, axis=0, keepdims=True)
    idx = jnp.min(jnp.where(x == m, iota, sentinel), axis=0, keepdims=True)
    return m, idx, iota == idx


def _route_tile(lg, bias, carry, place_base):
    e, tk = lg.shape
    s = _sigmoid(lg)
    biased = s + bias
    sub = lax.broadcasted_iota(I32, (GROUP_SIZE, tk), 0)
    group_rows = []
    for g in range(N_GROUPS):
        xg = biased[g * GROUP_SIZE:(g + 1) * GROUP_SIZE, :]
        m1, _, pick = _first_max(xg, sub, GROUP_SIZE)
        m2 = jnp.max(jnp.where(pick, NEG_INF, xg), axis=0, keepdims=True)
        group_rows.append(m1 + m2)
    gs = jnp.concatenate(group_rows, axis=0)
    gsub = lax.broadcasted_iota(I32, (N_GROUPS, tk), 0)
    sel = jnp.zeros((N_GROUPS, tk), F32)
    for _ in range(TOPK_GROUPS):
        _, _, pick = _first_max(gs, gsub, N_GROUPS)
        sel = jnp.where(pick, 1.0, sel)
        gs = jnp.where(pick, NEG_INF, gs)
    masked_rows = []
    for g in range(N_GROUPS):
        xg = biased[g * GROUP_SIZE:(g + 1) * GROUP_SIZE, :]
        masked_rows.append(jnp.where(sel[g:g + 1, :] > 0.5, xg, NEG_INF))
    masked = jnp.concatenate(masked_rows, axis=0)

    eio = lax.broadcasted_iota(I32, (e, tk), 0)
    chosen = jnp.zeros((e, tk), F32)
    ids, top_s = [], []
    for _ in range(TOP_K):
        _, idx, pick = _first_max(masked, eio, e)
        ids.append(idx)
        top_s.append(jnp.sum(jnp.where(pick, s, 0.0), axis=0, keepdims=True))
        chosen = jnp.where(pick, 1.0, chosen)
        masked = jnp.where(pick, NEG_INF, masked)
    total = top_s[0]
    for ts in top_s[1:]:
        total = total + ts

    before = (lax.broadcasted_iota(I32, (tk, tk), 0) < lax.broadcasted_iota(I32, (tk, tk), 1))
    upper = jnp.where(before, 1.0, 0.0).astype(BF16)
    local = jnp.dot(chosen.astype(BF16), upper, preferred_element_type=F32)
    tile_cnt = jnp.broadcast_to(jnp.sum(chosen, axis=1, keepdims=True), carry.shape)
    below = (lax.broadcasted_iota(I32, (e, e), 1) < lax.broadcasted_iota(I32, (e, e), 0))
    lower = jnp.where(below, 1.0, 0.0).astype(BF16)
    first = jnp.dot(lower, tile_cnt.astype(BF16), preferred_element_type=F32)
    rank = local + carry[:, 0:1]
    place = (local + first[:, 0:1] + place_base) * TOKEN_ROWS
    weights = [ts / total * ROUTED_SCALE for ts in top_s]
    pick_row = lambda k, v: jnp.sum(jnp.where(eio == ids[k], v, 0.0), axis=0, keepdims=True).astype(I32)
    ranks = [pick_row(k, rank) for k in range(TOP_K)]
    places = [pick_row(k, place) for k in range(TOP_K)]
    return ids, weights, ranks, places, tile_cnt


def _route_kernel(lg_ref, bias_ref, ids_ref, w_ref, rank_ref, lidx_ref, cnt_ref, tcarry_ref, tcnt_ref, carry_ref,
                  *, tile):
    i = pl.program_id(0)

    @pl.when(i == 0)
    def _():
        carry_ref[...] = jnp.zeros_like(carry_ref)

    carry = carry_ref[...]
    for j in range(lg_ref.shape[1] // tile):
        cols = slice(j * tile, (j + 1) * tile)
        place_base = float((j % 2) * tile * TOP_K)
        ids, weights, ranks, places, tile_cnt = _route_tile(lg_ref[:, cols], bias_ref[...], carry, place_base)
        for k in range(TOP_K):
            ids_ref[k:k + 1, cols] = ids[k]
            w_ref[k:k + 1, cols] = weights[k]
            rank_ref[k:k + 1, cols] = ranks[k]
            lidx_ref[k:k + 1, cols] = places[k]
        tcarry_ref[j] = carry.astype(I32)
        tcnt_ref[j] = tile_cnt.astype(I32)
        carry = carry + tile_cnt
    carry_ref[...] = carry
    cnt_ref[...] = carry.astype(I32)


def _route(logits_t, bias_col, tile):
    e, t = logits_t.shape
    tk = min(4 * tile, t)
    per_step = tk // tile
    assert per_step % 2 == 0 or t == tile
    row8 = lambda dt: jax.ShapeDtypeStruct((TOP_K, t), dt)
    blk8 = pl.BlockSpec((TOP_K, tk), lambda i: (0, i))
    per_tile = pl.BlockSpec((per_step, e, LANES), lambda i: (i, 0, 0))
    return pl.pallas_call(
        functools.partial(_route_kernel, tile=tile), grid=(t // tk,),
        in_specs=[pl.BlockSpec((e, tk), lambda i: (0, i)),
                  pl.BlockSpec((e, 1), lambda i: (0, 0))],
        out_specs=[blk8, blk8, blk8, blk8, pl.BlockSpec((e, LANES), lambda i: (0, 0)), per_tile, per_tile],
        out_shape=[row8(I32), row8(F32), row8(I32), row8(I32), jax.ShapeDtypeStruct((e, LANES), I32),
                   jax.ShapeDtypeStruct((t // tile, e, LANES), I32), jax.ShapeDtypeStruct((t // tile, e, LANES), I32)],
        scratch_shapes=[pltpu.VMEM((e, LANES), F32)],
        compiler_params=_params(1), name="route",
    )(logits_t, bias_col)


def _dispatch_kernel(pad_ref, cnt_ref, hp_ref, slot_ref, wsg_ref, wsu_ref, wsd_ref, xs_ref, sh_ref,
                     zero_ref, words_ref, sem, zsem, *, bm):
    i = pl.program_id(0)
    td = hp_ref.shape[0] // TOKEN_ROWS
    n_rows = xs_ref.shape[0] // TOKEN_ROWS
    n_blocks = n_rows // bm
    block = bm * TOKEN_ROWS

    @pl.when(i == 0)
    def _():
        zero_ref[...] = jnp.zeros_like(zero_ref)
        last_e = N_EXPERTS - 1
        used_blocks = pad_ref[last_e] // bm + (cnt_ref[last_e] + bm - 1) // bm

        def tail_copy(j):
            return pltpu.make_async_copy(zero_ref, xs_ref.at[pl.ds(pl.multiple_of(j * block, block), block)], zsem)

        def tail(j, c, wait):
            cp = tail_copy(j)
            cp.wait() if wait else cp.start()
            return c

        def pad(e, c, wait):
            row = pad_ref[e] + cnt_ref[e]
            n = (cnt_ref[e] + bm - 1) // bm * bm - cnt_ref[e]
            bit = bm // 2
            while bit:
                take = (n & bit) != 0
                size = bit * TOKEN_ROWS
                cp = pltpu.make_async_copy(
                    zero_ref.at[pl.ds(0, size)],
                    xs_ref.at[pl.ds(pl.multiple_of(row * TOKEN_ROWS, TOKEN_ROWS), size)], zsem)

                @pl.when(take)
                def _():
                    cp.wait() if wait else cp.start()

                row = row + jnp.where(take, bit, 0)
                bit //= 2
            return c

        for wait in (False, True):
            lax.fori_loop(used_blocks, n_blocks, functools.partial(tail, wait=wait), 0)
            lax.fori_loop(0, N_EXPERTS, functools.partial(pad, wait=wait), 0)

    def body(t, c):
        for k in range(TOP_K):
            slot = slot_ref[t * TOP_K + k]
            pltpu.make_async_copy(hp_ref.at[_token(t)], xs_ref.at[_token(slot)], sem).start(priority=k % 2)
        return c

    lax.fori_loop(0, td, body, 0)

    h = _load_tokens(hp_ref[...], words_ref).astype(BF16)
    sg = jnp.dot(h, wsg_ref[...], preferred_element_type=F32)
    su = jnp.dot(h, wsu_ref[...], preferred_element_type=F32)
    shared = jnp.dot((_silu(sg) * su).astype(BF16), wsd_ref[...], preferred_element_type=F32)
    sh_ref[...] = shared.astype(sh_ref.dtype)

    for k in range(TOP_K):
        pltpu.make_async_copy(hp_ref, xs_ref.at[pl.ds(0, td * TOKEN_ROWS)], sem).wait()


def _dispatch(pad_start, counts, hp, slots, wsg_b, wsu_b, wsd_b, n_rows, bm):
    t = hp.shape[0] // TOKEN_ROWS
    d, ds = wsg_b.shape
    td = min(512, t)
    whole = lambda shape: pl.BlockSpec(shape, lambda i, *_: (0, 0))
    grid_spec = pltpu.PrefetchScalarGridSpec(
        num_scalar_prefetch=2, grid=(t // td,),
        in_specs=[pl.BlockSpec((td * TOKEN_ROWS, LANES), lambda i, *_: (i, 0)),
                  pl.BlockSpec((td * TOP_K,), lambda i, *_: (i,), memory_space=pltpu.SMEM),
                  whole((d, ds)), whole((d, ds)), whole((ds, d))],
        out_specs=[pl.BlockSpec(memory_space=pl.ANY),
                   pl.BlockSpec((td, d), lambda i, *_: (i, 0))],
        scratch_shapes=[pltpu.VMEM((bm * TOKEN_ROWS, LANES), BF16), _words_scratch(td),
                        pltpu.SemaphoreType.DMA(()), pltpu.SemaphoreType.DMA(())])
    return pl.pallas_call(
        functools.partial(_dispatch_kernel, bm=bm), grid_spec=grid_spec,
        out_shape=[jax.ShapeDtypeStruct((n_rows * TOKEN_ROWS, LANES), BF16),
                   jax.ShapeDtypeStruct((t, d), BF16)],
        compiler_params=_params(1), name="dispatch",
    )(pad_start, counts, hp, slots, wsg_b, wsu_b, wsd_b)


def _expert_kernel(be_ref, first_ref, slot_ref, nxt_ref, rows_ref, nu_ref, x_ref, wg_hbm, wu_hbm, wd_hbm, o_ref,
                   words_ref, wg32, wu32, wd32, wg16, wu16, wd16, wsem):
    i = pl.program_id(0)

    def weight_copies(e, s):
        return (pltpu.make_async_copy(wg_hbm.at[e], wg32.at[s], wsem.at[s, 0]),
                pltpu.make_async_copy(wu_hbm.at[e], wu32.at[s], wsem.at[s, 1]),
                pltpu.make_async_copy(wd_hbm.at[e], wd32.at[s], wsem.at[s, 2]))

    @pl.when(i == 0)
    def _():
        for cp in weight_copies(be_ref[0], 0):
            cp.start()

    @pl.when(jnp.logical_and(first_ref[i] == 1, i < nu_ref[0]))
    def _():
        s = slot_ref[i]
        for cp in weight_copies(be_ref[i], s):
            cp.wait()

        @pl.when(nxt_ref[i] >= 0)
        def _():
            for cp in weight_copies(nxt_ref[i], 1 - s):
                cp.start()

        wg16[...] = wg32[s].astype(BF16)
        wu16[...] = wu32[s].astype(BF16)
        wd16[...] = wd32[s].astype(BF16)

    bm = x_ref.shape[0] // TOKEN_ROWS
    quarter = bm // EXPERT_BLOCK_PARTS
    used = jnp.where(i < nu_ref[0], (rows_ref[i] + quarter - 1) // quarter, 0)

    def swiglu_rows(n):
        x = _load_tokens(x_ref[pl.ds(0, n * TOKEN_ROWS), :], words_ref).astype(BF16)
        g = jnp.dot(x, wg16[...], preferred_element_type=F32)
        u = jnp.dot(x, wu16[...], preferred_element_type=F32)
        mid = (_silu(g) * u).astype(BF16)
        _store_tokens(o_ref, jnp.dot(mid, wd16[...], preferred_element_type=F32), words_ref)
        if n < bm:
            rest = (bm - n) * TOKEN_ROWS
            o_ref[pl.ds(n * TOKEN_ROWS, rest), :] = jnp.zeros((rest, LANES), o_ref.dtype)

    for parts in range(1, EXPERT_BLOCK_PARTS + 1):
        pl.when(used == parts)(functools.partial(swiglu_rows, parts * quarter))

    @pl.when(used == 0)
    def _():
        o_ref[...] = jnp.zeros_like(o_ref)


def _experts(blk_e, blk_first, blk_slot, blk_next, blk_rows, n_used, xs, w_gate, w_up, w_down, n_blocks, bm):
    _, d, de = w_gate.shape
    x_map = lambda i, be, fi, sl, nx, ro, nu: (jnp.minimum(i, nu[0] - 1), 0)
    hbm = pl.BlockSpec(memory_space=pl.ANY)
    grid_spec = pltpu.PrefetchScalarGridSpec(
        num_scalar_prefetch=6, grid=(n_blocks,),
        in_specs=[pl.BlockSpec((bm * TOKEN_ROWS, LANES), x_map), hbm, hbm, hbm],
        out_specs=pl.BlockSpec((bm * TOKEN_ROWS, LANES), lambda i, *_: (i, 0)),
        scratch_shapes=[_words_scratch(bm),
                        pltpu.VMEM((2, d, de), F32), pltpu.VMEM((2, d, de), F32), pltpu.VMEM((2, de, d), F32),
                        pltpu.VMEM((d, de), BF16), pltpu.VMEM((d, de), BF16), pltpu.VMEM((de, d), BF16),
                        pltpu.SemaphoreType.DMA((2, 3))])
    return pl.pallas_call(
        _expert_kernel, grid_spec=grid_spec,
        out_shape=jax.ShapeDtypeStruct((n_blocks * bm * TOKEN_ROWS, LANES), BF16),
        compiler_params=_params(1), name="experts",
    )(blk_e, blk_first, blk_slot, blk_next, blk_rows, n_used, xs, w_gate, w_up, w_down)


def _combine_kernel(gstart_ref, tcnt_ref, x1_ref, sh_ref, lidx_ref, w_ref, mod_ref, fw_ref, ys_ref, o_ref,
                    stage, acc_ref, sem, *, n_tiles):
    i = pl.program_id(0)
    tc = x1_ref.shape[0]
    n_e = N_EXPERTS
    tile_rows = tc * TOP_K

    def fetch(tile, buf):
        def per_expert(e, off):
            count = tcnt_ref[tile * n_e + e]
            row = gstart_ref[tile * n_e + e]
            bit = tc
            while bit:
                take = (count & bit) != 0
                size = bit * TOKEN_ROWS
                cp = pltpu.make_async_copy(
                    ys_ref.at[pl.ds(pl.multiple_of(row * TOKEN_ROWS, TOKEN_ROWS), size)],
                    stage.at[pl.ds(pl.multiple_of(off * TOKEN_ROWS, TOKEN_ROWS), size)], sem.at[buf])
                pl.when(take)(cp.start)
                step = jnp.where(take, bit, 0)
                row, off = row + step, off + step
                bit //= 2
            return off
        lax.fori_loop(0, n_e, per_expert, buf * tile_rows)

    @pl.when(i == 0)
    def _():
        fetch(0, 0)

    @pl.when(i + 1 < n_tiles)
    def _():
        fetch(i + 1, (i + 1) % 2)

    buf = i % 2
    half = pl.ds(pl.multiple_of(buf * tile_rows * TOKEN_ROWS, TOKEN_ROWS), tile_rows * TOKEN_ROWS)
    pltpu.make_async_copy(ys_ref.at[pl.ds(0, tile_rows * TOKEN_ROWS)], stage.at[half], sem.at[buf]).wait()

    def per_token(t, c):
        lo = hi = None
        for k in range(TOP_K):
            place = lidx_ref[t * TOP_K + k]
            wk = w_ref[t * TOP_K + k]
            words = pltpu.bitcast(stage[pl.ds(pl.multiple_of(place, TOKEN_ROWS), TOKEN_ROWS), :], jnp.uint32)
            a = wk * pltpu.unpack_elementwise(words, index=0, packed_dtype=BF16, unpacked_dtype=F32)
            b = wk * pltpu.unpack_elementwise(words, index=1, packed_dtype=BF16, unpacked_dtype=F32)
            lo, hi = (a, b) if k == 0 else (lo + a, hi + b)
        rows = pl.ds(pl.multiple_of(t * TOKEN_WORD_ROWS, TOKEN_WORD_ROWS), TOKEN_WORD_ROWS)
        acc_ref[0, rows, :] = lo
        acc_ref[1, rows, :] = hi
        return c

    lax.fori_loop(0, tc, per_token, 0, unroll=4)

    halves = [jnp.concatenate([acc_ref[half, pl.ds(s, tc, stride=TOKEN_WORD_ROWS), :]
                               for s in range(TOKEN_WORD_ROWS)], axis=1) for half in range(2)]
    routed = jnp.concatenate(halves, axis=1)
    x = x1_ref[...] + mod_ref[0][5:6, :] * (routed + sh_ref[...].astype(F32))
    o_ref[...] = x * lax.rsqrt(jnp.mean(x * x, axis=-1, keepdims=True) + EPS) * fw_ref[...]


def _combine(gstart, tcnt, x1, shared, lidx, w_flat, mod3, final_w, ys, seq, tc):
    t, d = x1.shape
    n_tiles = t // tc
    tiles_per_seq = seq // tc
    per_tile = lambda: pl.BlockSpec((tc * TOP_K,), lambda i, *_: (i,), memory_space=pltpu.SMEM)
    grid_spec = pltpu.PrefetchScalarGridSpec(
        num_scalar_prefetch=2, grid=(n_tiles,),
        in_specs=[pl.BlockSpec((tc, d), lambda i, *_: (i, 0)),
                  pl.BlockSpec((tc, d), lambda i, *_: (i, 0)),
                  per_tile(), per_tile(),
                  pl.BlockSpec((1, 6, d), lambda i, *_: (i // tiles_per_seq, 0, 0)),
                  pl.BlockSpec((1, d), lambda i, *_: (0, 0)),
                  pl.BlockSpec(memory_space=pl.ANY)],
        out_specs=pl.BlockSpec((tc, d), lambda i, *_: (i, 0)),
        scratch_shapes=[pltpu.VMEM((2 * tc * TOP_K * TOKEN_ROWS, LANES), BF16),
                        pltpu.VMEM((2, tc * TOKEN_WORD_ROWS, LANES), F32),
                        pltpu.SemaphoreType.DMA((2,))])
    return pl.pallas_call(
        functools.partial(_combine_kernel, n_tiles=n_tiles), grid_spec=grid_spec,
        out_shape=jax.ShapeDtypeStruct((t, d), F32),
        compiler_params=_params(1), name="combine",
    )(gstart, tcnt, x1, shared, lidx, w_flat, mod3, final_w, ys)


def _rope_tables(seq, d):
    half = d // 2
    inv = ROPE_BASE ** (-jnp.arange(half, dtype=F32) / half)
    ang = jnp.arange(seq, dtype=F32)[:, None] * inv[None, :]
    cos, sin = jnp.cos(ang), jnp.sin(ang)
    return jnp.concatenate([cos, cos], axis=1), jnp.concatenate([-sin, sin], axis=1)


def kernel(x, c, w_ada, b_ada, norm1_w, w_in, ret_decay_fwd, ret_decay_bwd, ret_norm_w, w_ret_o, conv_w,
           w_conv_o, w_out, norm2_w, router_w, router_bias, w_gate, w_up, w_down, ws_gate, ws_up, ws_down,
           final_norm_w):
    batch, seq, d = x.shape
    depth = w_ada.shape[0]
    t = batch * seq
    bm = EXPERT_BLOCK
    n_blocks = (t * TOP_K + N_EXPERTS * (bm - 1)) // bm
    n_rows = n_blocks * bm
    cos, sin = _rope_tables(seq, HEAD_DIM)
    c_pad = jnp.pad(c, ((0, BF16_SUBLANES - batch % BF16_SUBLANES), (0, 0)))
    x2 = x.reshape(t, d)

    assert depth == 1, "the final norm is fused into the last stage of a single layer"
    for l in range(depth):
        mod = _ada(c_pad, w_ada[l], b_ada[l][None, :])[:batch]
        mod3 = mod.reshape(batch, 6, d)
        proj = _inproj(x2, mod3, norm1_w[l][None, :], w_in[l], seq)
        lane_bcast = lambda v: jnp.broadcast_to(v[:, None, None], (RET_HEADS, 1, HEAD_DIM))
        og = _retention(proj, cos, sin, lane_bcast(ret_decay_fwd[l]), lane_bcast(ret_decay_bwd[l]),
                        ret_norm_w[l].reshape(RET_HEADS, 1, HEAD_DIM), batch, seq)
        merged = _mix(og, proj, conv_w[l], w_ret_o[l].astype(BF16), w_conv_o[l].astype(BF16), seq)
        x1, hp, logits_t = _outproj(merged, x2, mod3, norm2_w[l][None, :], w_out[l].astype(BF16),
                                    router_w[l].T, seq)
        ids_t, w_t, rank_t, lidx_t, cnt, tile_carry, tile_cnt = _route(logits_t, router_bias[l][:, None], COMBINE_TILE)

        counts = cnt[:, 0]
        nblk = (counts + bm - 1) // bm
        blk_end = jnp.cumsum(nblk)
        pad_start = ((blk_end - nblk) * bm).astype(I32)
        n_used = blk_end[-1:].astype(I32)
        blk_ids = jnp.arange(n_blocks, dtype=I32)
        blk_e = jnp.minimum(jnp.sum((blk_ids[:, None] >= blk_end[None, :]).astype(I32), axis=1),
                            N_EXPERTS - 1)

        blk_first = jnp.concatenate([jnp.ones((1,), I32), (blk_e[1:] != blk_e[:-1]).astype(I32)])
        blk_slot = (jnp.cumsum(blk_first) - 1) % 2
        after = blk_end[blk_e]
        blk_next = jnp.where(after < n_used[0], blk_e[jnp.minimum(after, n_blocks - 1)], -1).astype(I32)

        onehot = ids_t[:, :, None] == jnp.arange(N_EXPERTS, dtype=I32)
        slots_t = rank_t + jnp.sum(jnp.where(onehot, pad_start, 0), axis=-1)
        slots = slots_t.T.reshape(t * TOP_K)
        xs, shared = _dispatch(pad_start, counts, hp, slots, ws_gate[l].astype(BF16), ws_up[l].astype(BF16),
                               ws_down[l].astype(BF16), n_rows, bm)
        blk_rows = jnp.clip(counts[blk_e] - (blk_ids * bm - pad_start[blk_e]), 0, bm).astype(I32)
        ys = _experts(blk_e, blk_first, blk_slot.astype(I32), blk_next, blk_rows, n_used, xs,
                      w_gate[l], w_up[l], w_down[l], n_blocks, bm)
        gstart = (pad_start[None, :] + tile_carry[:, :, 0]).reshape(-1)
        x2 = _combine(gstart, tile_cnt[:, :, 0].reshape(-1), x1, shared, lidx_t.T.reshape(t * TOP_K),
                      w_t.T.reshape(t * TOP_K), mod3, final_norm_w[None, :], ys, seq, COMBINE_TILE)
    return x2.reshape(batch, seq, d)
```

```python
import functools

import jax
import jax.numpy as jnp
from jax import lax
from jax.experimental import pallas as pl
from jax.experimental.pallas import tpu as pltpu

F32 = jnp.float32
BF16 = jnp.bfloat16
I32 = jnp.int32

EPS = 1e-6
RET_HEADS = 8
HEAD_DIM = 128
ROPE_BASE = 10000.0
N_EXPERTS = 64
TOP_K = 8
N_GROUPS = 8
TOPK_GROUPS = 4
GROUP_SIZE = N_EXPERTS // N_GROUPS
ROUTED_SCALE = 2.5

V7X_VMEM_BYTES = 64 * 1024 * 1024
VMEM_LIMIT = V7X_VMEM_BYTES - 8 * 1024 * 1024
BF16_SUBLANES = 16

RET_CHUNK = 256
EXPERT_BLOCK = 512
EXPERT_BLOCK_PARTS = 4
COMBINE_TILE = 256
NEG_INF = float("-inf")


def _params(n_axes):
    return pltpu.CompilerParams(dimension_semantics=("arbitrary",) * n_axes,
                                vmem_limit_bytes=VMEM_LIMIT)


def _sigmoid(x):
    return 1.0 / (1.0 + jnp.exp(-x))


def _silu(x):
    return x * _sigmoid(x)


LANES = 128
TOKEN_WORD_ROWS = 8
TOKEN_ROWS = 2 * TOKEN_WORD_ROWS


def _store_tokens(ref, x, words_ref):
    n, m = x.shape[0], x.shape[1] // 2
    assert m == TOKEN_WORD_ROWS * LANES
    packed = pltpu.pack_elementwise([x[:, :m], x[:, m:]], packed_dtype=BF16)
    for s in range(TOKEN_WORD_ROWS):
        words_ref[pl.ds(s, n, stride=TOKEN_WORD_ROWS), :] = packed[:, s * LANES:(s + 1) * LANES]
    ref[pl.ds(0, n * TOKEN_ROWS), :] = pltpu.bitcast(words_ref[pl.ds(0, n * TOKEN_WORD_ROWS), :], BF16)


def _load_tokens(tiles, words_ref):
    n = tiles.shape[0] // TOKEN_ROWS
    words_ref[pl.ds(0, n * TOKEN_WORD_ROWS), :] = pltpu.bitcast(tiles, jnp.uint32)
    p = jnp.concatenate([words_ref[pl.ds(s, n, stride=TOKEN_WORD_ROWS), :]
                         for s in range(TOKEN_WORD_ROWS)], axis=1)
    a = pltpu.unpack_elementwise(p, index=0, packed_dtype=BF16, unpacked_dtype=F32)
    b = pltpu.unpack_elementwise(p, index=1, packed_dtype=BF16, unpacked_dtype=F32)
    return jnp.concatenate([a, b], axis=1)


def _token(row):
    return pl.ds(pl.multiple_of(row * TOKEN_ROWS, TOKEN_ROWS), TOKEN_ROWS)


def _words_scratch(n_tokens):
    return pltpu.VMEM((n_tokens * TOKEN_WORD_ROWS, LANES), jnp.uint32)


def _ada_kernel(c_ref, w_ref, b_ref, o_ref):
    s = _silu(c_ref[...]).astype(BF16)
    o_ref[...] = jnp.dot(s, w_ref[...].astype(BF16), preferred_element_type=F32) + b_ref[...]


def _ada(c_pad, w_ada, b_ada):
    m, d = c_pad.shape
    n = w_ada.shape[1]
    tn = 1024
    return pl.pallas_call(
        _ada_kernel, grid=(n // tn,),
        in_specs=[pl.BlockSpec((m, d), lambda j: (0, 0)),
                  pl.BlockSpec((d, tn), lambda j: (0, j)),
                  pl.BlockSpec((1, tn), lambda j: (0, j))],
        out_specs=pl.BlockSpec((m, tn), lambda j: (0, j)),
        out_shape=jax.ShapeDtypeStruct((m, n), F32),
        compiler_params=_params(1), name="ada")(c_pad, w_ada, b_ada)


def _inproj_kernel(x_ref, mod_ref, nw_ref, w_ref, o_ref, h_ref):
    @pl.when(pl.program_id(1) == 0)
    def _():
        x = x_ref[...]
        y = x * lax.rsqrt(jnp.mean(x * x, axis=-1, keepdims=True) + EPS) * nw_ref[...]
        m = mod_ref[0]
        h_ref[...] = (y * (1.0 + m[1:2, :]) + m[0:1, :]).astype(BF16)

    o_ref[...] = jnp.dot(h_ref[...], w_ref[...].astype(BF16), preferred_element_type=F32).astype(o_ref.dtype)


def _inproj(x2, mod3, norm_w, w_in, seq):
    t, d = x2.shape
    n = w_in.shape[1]
    tm, tn = min(1024, seq), 1024
    tiles_per_seq = seq // tm
    return pl.pallas_call(
        _inproj_kernel, grid=(t // tm, n // tn),
        in_specs=[pl.BlockSpec((tm, d), lambda i, j: (i, 0)),
                  pl.BlockSpec((1, 6, d), lambda i, j: (i // tiles_per_seq, 0, 0)),
                  pl.BlockSpec((1, d), lambda i, j: (0, 0)),
                  pl.BlockSpec((d, tn), lambda i, j: (0, j))],
        out_specs=pl.BlockSpec((tm, tn), lambda i, j: (i, j)),
        out_shape=jax.ShapeDtypeStruct((t, n), BF16),
        scratch_shapes=[pltpu.VMEM((tm, d), BF16)],
        compiler_params=_params(2), name="inproj")(x2, mod3, norm_w, w_in)


def _log_sigmoid(x):
    return jnp.minimum(x, 0.0) - jnp.log1p(jnp.exp(-jnp.abs(x)))


def _ret_kernel(q_ref, k_ref, v_ref, g_ref, cos_ref, sin_ref, df_ref, db_ref, nw_ref, o_ref,
                mask_ref, qf_ref, qb_ref, kf_ref, kb_ref, dec_ref, qr_ref, kr_ref, acc_ref, *, chunk):
    seq, d = q_ref.shape
    n = seq // chunk
    c = chunk

    @pl.when(pl.program_id(1) == 0)
    def _():
        lgf = _log_sigmoid(df_ref[0])
        lgb = _log_sigmoid(db_ref[0])
        ii = lax.broadcasted_iota(I32, (c, c), 0)
        jj = lax.broadcasted_iota(I32, (c, c), 1)
        diff = (ii - jj).astype(F32)
        lgf_c = jnp.concatenate([lgf] * (c // d), axis=1)
        lgb_c = jnp.concatenate([lgb] * (c // d), axis=1)
        mask_ref[...] = jnp.where(diff >= 0.0,
                                  jnp.exp(lgf_c * jnp.maximum(diff, 0.0)),
                                  jnp.exp(lgb_c * jnp.maximum(-diff, 0.0)))
        pos = lax.broadcasted_iota(I32, (c, d), 0).astype(F32)
        qf_ref[...] = jnp.exp(lgf * (pos + 1.0))
        qb_ref[...] = jnp.exp(lgb * (c - pos))
        kf_ref[...] = jnp.exp(lgf * (c - 1.0 - pos))
        kb_ref[...] = jnp.exp(lgb * pos)
        dec_ref[0:1, :] = jnp.exp(lgf * c)
        dec_ref[1:2, :] = jnp.exp(lgb * c)

    scale = d ** -0.5
    nt = (((1,), (1,)), ((), ()))
    sls = [pl.ds(i * c, c) for i in range(n)]

    for sl in sls:
        cs = cos_ref[sl, :]
        sn = sin_ref[sl, :]
        q = q_ref[sl, :].astype(F32)
        k = k_ref[sl, :].astype(F32)
        qr_ref[sl, :] = q * cs + pltpu.roll(q, d // 2, 1) * sn
        kr_ref[sl, :] = (k * cs + pltpu.roll(k, d // 2, 1) * sn) * scale

    for sl in sls:
        s = lax.dot_general(qr_ref[sl, :].astype(BF16), kr_ref[sl, :].astype(BF16), nt,
                            preferred_element_type=F32)
        p = (s * mask_ref[...]).astype(BF16)
        acc_ref[sl, :] = jnp.dot(p, v_ref[sl, :], preferred_element_type=F32)

    def kv_state(sl, kw_ref):
        kw = (kr_ref[sl, :] * kw_ref[...]).T.astype(BF16)
        return jnp.dot(kw, v_ref[sl, :], preferred_element_type=F32)

    def scan(order, qw_ref, kw_ref, dec):
        st = jnp.zeros((d, d), F32)
        for idx, ci in enumerate(order):
            sl = sls[ci]
            if idx > 0:
                qw = (qr_ref[sl, :] * qw_ref[...]).astype(BF16)
                acc_ref[sl, :] += jnp.dot(qw, st.astype(BF16), preferred_element_type=F32)
            if idx < n - 1:
                st = dec * st + kv_state(sl, kw_ref)

    scan(list(range(n)), qf_ref, kf_ref, dec_ref[0:1, :])
    scan(list(range(n - 1, -1, -1)), qb_ref, kb_ref, dec_ref[1:2, :])

    nw = nw_ref[0]
    for sl in sls:
        o = acc_ref[sl, :]
        on = o * lax.rsqrt(jnp.mean(o * o, axis=-1, keepdims=True) + EPS) * nw
        g = g_ref[sl, :].astype(F32)
        o_ref[sl, :] = (_silu(g) * on).astype(o_ref.dtype)


def _retention(proj, cos, sin, dec_f, dec_b, ret_norm_w, batch, seq):
    h, d = RET_HEADS, HEAD_DIM
    c = min(RET_CHUNK, seq)
    col = lambda off: pl.BlockSpec((seq, d), lambda hh, b: (b, off + hh))
    per_head = pl.BlockSpec((1, 1, d), lambda hh, b: (hh, 0, 0))
    table = pl.BlockSpec((seq, d), lambda hh, b: (0, 0))
    vm = lambda shape, dt=F32: pltpu.VMEM(shape, dt)
    return pl.pallas_call(
        functools.partial(_ret_kernel, chunk=c), grid=(h, batch),
        in_specs=[col(0), col(h), col(2 * h), col(3 * h), table, table, per_head, per_head, per_head],
        out_specs=pl.BlockSpec((seq, d), lambda hh, b: (b, hh)),
        out_shape=jax.ShapeDtypeStruct((batch * seq, h * d), BF16),
        scratch_shapes=[vm((c, c)), vm((c, d)), vm((c, d)), vm((c, d)), vm((c, d)), vm((8, d)),
                        vm((seq, d)), vm((seq, d)), vm((seq, d))],
        compiler_params=_params(2), name="retention",
    )(proj, proj, proj, proj, cos, sin, dec_f, dec_b, ret_norm_w)


def _mix_kernel(og_ref, cb_ref, cc_ref, cu_ref, ccp_ref, cup_ref, ccn_ref, cun_ref,
                gr0_ref, gr1_ref, gc0_ref, gc1_ref, cw_ref, wr_ref, wc_ref, o_ref, *, tiles_per_seq):
    i = pl.program_id(0)
    tm, dr = cc_ref.shape
    u = cc_ref[...].astype(F32) * cu_ref[...].astype(F32)
    pos = i % tiles_per_seq
    last = BF16_SUBLANES - 1
    u_before = ccp_ref[last:last + 1, :].astype(F32) * cup_ref[last:last + 1, :].astype(F32)
    u_before = jnp.where(pos == 0, 0.0, u_before)
    u_after = ccn_ref[0:1, :].astype(F32) * cun_ref[0:1, :].astype(F32)
    u_after = jnp.where(pos == tiles_per_seq - 1, 0.0, u_after)
    row = lax.broadcasted_iota(I32, u.shape, 0)
    u_prev = jnp.where(row == 0, u_before, pltpu.roll(u, 1, 0))
    u_next = jnp.where(row == tm - 1, u_after, pltpu.roll(u, tm - 1, 0))
    cw = cw_ref[...]
    y = cw[0:1, :] * u_prev + cw[1:2, :] * u + cw[2:3, :] * u_next
    z = (cb_ref[...].astype(F32) * y).astype(BF16)

    yr = jnp.dot(og_ref[...], wr_ref[...], preferred_element_type=F32)
    yc = jnp.dot(z, wc_ref[...], preferred_element_type=F32)
    for lo, gr_ref, gc_ref in ((0, gr0_ref, gc0_ref), (dr, gr1_ref, gc1_ref)):
        merged = (_sigmoid(gr_ref[...].astype(F32)) * yr[:, lo:lo + dr]
                  + _sigmoid(gc_ref[...].astype(F32)) * yc[:, lo:lo + dr])
        o_ref[:, lo:lo + dr] = merged.astype(o_ref.dtype)


def _mix(og, proj, conv_w, w_ret_o_b, w_conv_o_b, seq):
    t, dr = og.shape
    d = w_ret_o_b.shape[1]
    assert d == 2 * dr
    tm = min(512, seq)
    tiles_per_seq = seq // tm
    hb = tm // BF16_SUBLANES
    n_hblk = t // BF16_SUBLANES
    wide = lambda off: pl.BlockSpec((tm, dr), lambda i: (i, off))
    before = lambda off: pl.BlockSpec((BF16_SUBLANES, dr), lambda i: (jnp.maximum(i * hb - 1, 0), off))
    after = lambda off: pl.BlockSpec((BF16_SUBLANES, dr), lambda i: (jnp.minimum((i + 1) * hb, n_hblk - 1), off))
    whole = lambda shape: pl.BlockSpec(shape, lambda i: (0, 0))
    return pl.pallas_call(
        functools.partial(_mix_kernel, tiles_per_seq=tiles_per_seq), grid=(t // tm,),
        in_specs=[wide(0), wide(4), wide(5), wide(6), before(5), before(6), after(5), after(6),
                  wide(7), wide(8), wide(9), wide(10),
                  whole((3, dr)), whole((dr, d)), whole((dr, d))],
        out_specs=pl.BlockSpec((tm, d), lambda i: (i, 0)),
        out_shape=jax.ShapeDtypeStruct((t, d), BF16),
        compiler_params=_params(1), name="mix",
    )(og, proj, proj, proj, proj, proj, proj, proj, proj, proj, proj, proj, conv_w, w_ret_o_b, w_conv_o_b)


def _outproj_kernel(m_ref, x_ref, mod_ref, nw_ref, w_ref, rwt_ref, x1_ref, hp_ref, lg_ref, words_ref):
    y = jnp.dot(m_ref[...], w_ref[...], preferred_element_type=F32)
    m = mod_ref[0]
    x1 = x_ref[...] + m[2:3, :] * y
    x1_ref[...] = x1
    hn = x1 * lax.rsqrt(jnp.mean(x1 * x1, axis=-1, keepdims=True) + EPS) * nw_ref[...]
    h = hn * (1.0 + m[4:5, :]) + m[3:4, :]
    _store_tokens(hp_ref, h, words_ref)
    h_hi = h.astype(BF16)
    h_lo = (h - h_hi.astype(F32)).astype(BF16)
    rw = rwt_ref[...]
    r_hi = rw.astype(BF16)
    r_lo = (rw - r_hi.astype(F32)).astype(BF16)
    nt = (((1,), (1,)), ((), ()))
    dot = functools.partial(lax.dot_general, dimension_numbers=nt, preferred_element_type=F32)
    lg_ref[...] = dot(r_hi, h_hi) + dot(r_hi, h_lo) + dot(r_lo, h_hi)


def _outproj(merged, x2, mod3, norm_w, w_out_b, router_wt, seq):
    t, d = x2.shape
    e = router_wt.shape[0]
    tm = min(512, seq)
    tiles_per_seq = seq // tm
    return pl.pallas_call(
        _outproj_kernel, grid=(t // tm,),
        in_specs=[pl.BlockSpec((tm, d), lambda i: (i, 0)),
                  pl.BlockSpec((tm, d), lambda i: (i, 0)),
                  pl.BlockSpec((1, 6, d), lambda i: (i // tiles_per_seq, 0, 0)),
                  pl.BlockSpec((1, d), lambda i: (0, 0)),
                  pl.BlockSpec((d, d), lambda i: (0, 0)),
                  pl.BlockSpec((e, d), lambda i: (0, 0))],
        out_specs=[pl.BlockSpec((tm, d), lambda i: (i, 0)),
                   pl.BlockSpec((TOKEN_ROWS * tm, LANES), lambda i: (i, 0)),
                   pl.BlockSpec((e, tm), lambda i: (0, i))],
        out_shape=[jax.ShapeDtypeStruct((t, d), F32),
                   jax.ShapeDtypeStruct((TOKEN_ROWS * t, LANES), BF16),
                   jax.ShapeDtypeStruct((e, t), F32)],
        scratch_shapes=[_words_scratch(tm)],
        compiler_params=_params(1), name="outproj",
    )(merged, x2, mod3, norm_w, w_out_b, router_wt)


def _first_max(x, iota, sentinel):
    m = jnp.max(x, axis=0, keepdims=True)
    idx = jnp.min(jnp.where(x == m, iota, sentinel), axis=0, keepdims=True)
    return m, idx, iota == idx


def _route_tile(lg, bias, carry, place_base):
    e, tk = lg.shape
    s = _sigmoid(lg)
    biased = s + bias
    sub = lax.broadcasted_iota(I32, (GROUP_SIZE, tk), 0)
    group_rows = []
    for g in range(N_GROUPS):
        xg = biased[g * GROUP_SIZE:(g + 1) * GROUP_SIZE, :]
        m1, _, pick = _first_max(xg, sub, GROUP_SIZE)
        m2 = jnp.max(jnp.where(pick, NEG_INF, xg), axis=0, keepdims=True)
        group_rows.append(m1 + m2)
    gs = jnp.concatenate(group_rows, axis=0)
    gsub = lax.broadcasted_iota(I32, (N_GROUPS, tk), 0)
    sel = jnp.zeros((N_GROUPS, tk), F32)
    for _ in range(TOPK_GROUPS):
        _, _, pick = _first_max(gs, gsub, N_GROUPS)
        sel = jnp.where(pick, 1.0, sel)
        gs = jnp.where(pick, NEG_INF, gs)
    masked_rows = []
    for g in range(N_GROUPS):
        xg = biased[g * GROUP_SIZE:(g + 1) * GROUP_SIZE, :]
        masked_rows.append(jnp.where(sel[g:g + 1, :] > 0.5, xg, NEG_INF))
    masked = jnp.concatenate(masked_rows, axis=0)

    eio = lax.broadcasted_iota(I32, (e, tk), 0)
    chosen = jnp.zeros((e, tk), F32)
    ids, top_s = [], []
    for _ in range(TOP_K):
        _, idx, pick = _first_max(masked, eio, e)
        ids.append(idx)
        top_s.append(jnp.sum(jnp.where(pick, s, 0.0), axis=0, keepdims=True))
        chosen = jnp.where(pick, 1.0, chosen)
        masked = jnp.where(pick, NEG_INF, masked)
    total = top_s[0]
    for ts in top_s[1:]:
        total = total + ts

    before = (lax.broadcasted_iota(I32, (tk, tk), 0) < lax.broadcasted_iota(I32, (tk, tk), 1))
    upper = jnp.where(before, 1.0, 0.0).astype(BF16)
    local = jnp.dot(chosen.astype(BF16), upper, preferred_element_type=F32)
    tile_cnt = jnp.broadcast_to(jnp.sum(chosen, axis=1, keepdims=True), carry.shape)
    below = (lax.broadcasted_iota(I32, (e, e), 1) < lax.broadcasted_iota(I32, (e, e), 0))
    lower = jnp.where(below, 1.0, 0.0).astype(BF16)
    first = jnp.dot(lower, tile_cnt.astype(BF16), preferred_element_type=F32)
    rank = local + carry[:, 0:1]
    place = (local + first[:, 0:1] + place_base) * TOKEN_ROWS
    weights = [ts / total * ROUTED_SCALE for ts in top_s]
    pick_row = lambda k, v: jnp.sum(jnp.where(eio == ids[k], v, 0.0), axis=0, keepdims=True).astype(I32)
    ranks = [pick_row(k, rank) for k in range(TOP_K)]
    places = [pick_row(k, place) for k in range(TOP_K)]
    return ids, weights, ranks, places, tile_cnt


def _route_kernel(lg_ref, bias_ref, ids_ref, w_ref, rank_ref, lidx_ref, cnt_ref, tcarry_ref, tcnt_ref, carry_ref,
                  *, tile):
    i = pl.program_id(0)

    @pl.when(i == 0)
    def _():
        carry_ref[...] = jnp.zeros_like(carry_ref)

    carry = carry_ref[...]
    for j in range(lg_ref.shape[1] // tile):
        cols = slice(j * tile, (j + 1) * tile)
        place_base = float((j % 2) * tile * TOP_K)
        ids, weights, ranks, places, tile_cnt = _route_tile(lg_ref[:, cols], bias_ref[...], carry, place_base)
        for k in range(TOP_K):
            ids_ref[k:k + 1, cols] = ids[k]
            w_ref[k:k + 1, cols] = weights[k]
            rank_ref[k:k + 1, cols] = ranks[k]
            lidx_ref[k:k + 1, cols] = places[k]
        tcarry_ref[j] = carry.astype(I32)
        tcnt_ref[j] = tile_cnt.astype(I32)
        carry = carry + tile_cnt
    carry_ref[...] = carry
    cnt_ref[...] = carry.astype(I32)


def _route(logits_t, bias_col, tile):
    e, t = logits_t.shape
    tk = min(4 * tile, t)
    per_step = tk // tile
    assert per_step % 2 == 0 or t == tile
    row8 = lambda dt: jax.ShapeDtypeStruct((TOP_K, t), dt)
    blk8 = pl.BlockSpec((TOP_K, tk), lambda i: (0, i))
    per_tile = pl.BlockSpec((per_step, e, LANES), lambda i: (i, 0, 0))
    return pl.pallas_call(
        functools.partial(_route_kernel, tile=tile), grid=(t // tk,),
        in_specs=[pl.BlockSpec((e, tk), lambda i: (0, i)),
                  pl.BlockSpec((e, 1), lambda i: (0, 0))],
        out_specs=[blk8, blk8, blk8, blk8, pl.BlockSpec((e, LANES), lambda i: (0, 0)), per_tile, per_tile],
        out_shape=[row8(I32), row8(F32), row8(I32), row8(I32), jax.ShapeDtypeStruct((e, LANES), I32),
                   jax.ShapeDtypeStruct((t // tile, e, LANES), I32), jax.ShapeDtypeStruct((t // tile, e, LANES), I32)],
        scratch_shapes=[pltpu.VMEM((e, LANES), F32)],
        compiler_params=_params(1), name="route",
    )(logits_t, bias_col)


def _dispatch_kernel(pad_ref, cnt_ref, hp_ref, slot_ref, wsg_ref, wsu_ref, wsd_ref, xs_ref, sh_ref,
                     zero_ref, words_ref, sem, zsem, *, bm):
    i = pl.program_id(0)
    td = hp_ref.shape[0] // TOKEN_ROWS
    n_rows = xs_ref.shape[0] // TOKEN_ROWS
    n_blocks = n_rows // bm
    block = bm * TOKEN_ROWS

    @pl.when(i == 0)
    def _():
        zero_ref[...] = jnp.zeros_like(zero_ref)
        last_e = N_EXPERTS - 1
        used_blocks = pad_ref[last_e] // bm + (cnt_ref[last_e] + bm - 1) // bm

        def tail_copy(j):
            return pltpu.make_async_copy(zero_ref, xs_ref.at[pl.ds(pl.multiple_of(j * block, block), block)], zsem)

        def tail(j, c, wait):
            cp = tail_copy(j)
            cp.wait() if wait else cp.start()
            return c

        def pad(e, c, wait):
            row = pad_ref[e] + cnt_ref[e]
            n = (cnt_ref[e] + bm - 1) // bm * bm - cnt_ref[e]
            bit = bm // 2
            while bit:
                take = (n & bit) != 0
                size = bit * TOKEN_ROWS
                cp = pltpu.make_async_copy(
                    zero_ref.at[pl.ds(0, size)],
                    xs_ref.at[pl.ds(pl.multiple_of(row * TOKEN_ROWS, TOKEN_ROWS), size)], zsem)

                @pl.when(take)
                def _():
                    cp.wait() if wait else cp.start()

                row = row + jnp.where(take, bit, 0)
                bit //= 2
            return c

        for wait in (False, True):
            lax.fori_loop(used_blocks, n_blocks, functools.partial(tail, wait=wait), 0)
            lax.fori_loop(0, N_EXPERTS, functools.partial(pad, wait=wait), 0)

    def body(t, c):
        for k in range(TOP_K):
            slot = slot_ref[t * TOP_K + k]
            pltpu.make_async_copy(hp_ref.at[_token(t)], xs_ref.at[_token(slot)], sem).start(priority=k % 2)
        return c

    lax.fori_loop(0, td, body, 0)

    h = _load_tokens(hp_ref[...], words_ref).astype(BF16)
    sg = jnp.dot(h, wsg_ref[...], preferred_element_type=F32)
    su = jnp.dot(h, wsu_ref[...], preferred_element_type=F32)
    shared = jnp.dot((_silu(sg) * su).astype(BF16), wsd_ref[...], preferred_element_type=F32)
    sh_ref[...] = shared.astype(sh_ref.dtype)

    for k in range(TOP_K):
        pltpu.make_async_copy(hp_ref, xs_ref.at[pl.ds(0, td * TOKEN_ROWS)], sem).wait()


def _dispatch(pad_start, counts, hp, slots, wsg_b, wsu_b, wsd_b, n_rows, bm):
    t = hp.shape[0] // TOKEN_ROWS
    d, ds = wsg_b.shape
    td = min(512, t)
    whole = lambda shape: pl.BlockSpec(shape, lambda i, *_: (0, 0))
    grid_spec = pltpu.PrefetchScalarGridSpec(
        num_scalar_prefetch=2, grid=(t // td,),
        in_specs=[pl.BlockSpec((td * TOKEN_ROWS, LANES), lambda i, *_: (i, 0)),
                  pl.BlockSpec((td * TOP_K,), lambda i, *_: (i,), memory_space=pltpu.SMEM),
                  whole((d, ds)), whole((d, ds)), whole((ds, d))],
        out_specs=[pl.BlockSpec(memory_space=pl.ANY),
                   pl.BlockSpec((td, d), lambda i, *_: (i, 0))],
        scratch_shapes=[pltpu.VMEM((bm * TOKEN_ROWS, LANES), BF16), _words_scratch(td),
                        pltpu.SemaphoreType.DMA(()), pltpu.SemaphoreType.DMA(())])
    return pl.pallas_call(
        functools.partial(_dispatch_kernel, bm=bm), grid_spec=grid_spec,
        out_shape=[jax.ShapeDtypeStruct((n_rows * TOKEN_ROWS, LANES), BF16),
                   jax.ShapeDtypeStruct((t, d), BF16)],
        compiler_params=_params(1), name="dispatch",
    )(pad_start, counts, hp, slots, wsg_b, wsu_b, wsd_b)


def _expert_kernel(be_ref, first_ref, slot_ref, nxt_ref, rows_ref, nu_ref, x_ref, wg_hbm, wu_hbm, wd_hbm, o_ref,
                   words_ref, wg32, wu32, wd32, wg16, wu16, wd16, wsem):
    i = pl.program_id(0)

    def weight_copies(e, s):
        return (pltpu.make_async_copy(wg_hbm.at[e], wg32.at[s], wsem.at[s, 0]),
                pltpu.make_async_copy(wu_hbm.at[e], wu32.at[s], wsem.at[s, 1]),
                pltpu.make_async_copy(wd_hbm.at[e], wd32.at[s], wsem.at[s, 2]))

    @pl.when(i == 0)
    def _():
        for cp in weight_copies(be_ref[0], 0):
            cp.start()

    @pl.when(jnp.logical_and(first_ref[i] == 1, i < nu_ref[0]))
    def _():
        s = slot_ref[i]
        for cp in weight_copies(be_ref[i], s):
            cp.wait()

        @pl.when(nxt_ref[i] >= 0)
        def _():
            for cp in weight_copies(nxt_ref[i], 1 - s):
                cp.start()

        wg16[...] = wg32[s].astype(BF16)
        wu16[...] = wu32[s].astype(BF16)
        wd16[...] = wd32[s].astype(BF16)

    bm = x_ref.shape[0] // TOKEN_ROWS
    quarter = bm // EXPERT_BLOCK_PARTS
    used = jnp.where(i < nu_ref[0], (rows_ref[i] + quarter - 1) // quarter, 0)

    def swiglu_rows(n):
        x = _load_tokens(x_ref[pl.ds(0, n * TOKEN_ROWS), :], words_ref).astype(BF16)
        g = jnp.dot(x, wg16[...], preferred_element_type=F32)
        u = jnp.dot(x, wu16[...], preferred_element_type=F32)
        mid = (_silu(g) * u).astype(BF16)
        _store_tokens(o_ref, jnp.dot(mid, wd16[...], preferred_element_type=F32), words_ref)
        if n < bm:
            rest = (bm - n) * TOKEN_ROWS
            o_ref[pl.ds(n * TOKEN_ROWS, rest), :] = jnp.zeros((rest, LANES), o_ref.dtype)

    for parts in range(1, EXPERT_BLOCK_PARTS + 1):
        pl.when(used == parts)(functools.partial(swiglu_rows, parts * quarter))

    @pl.when(used == 0)
    def _():
        o_ref[...] = jnp.zeros_like(o_ref)


def _experts(blk_e, blk_first, blk_slot, blk_next, blk_rows, n_used, xs, w_gate, w_up, w_down, n_blocks, bm):
    _, d, de = w_gate.shape
    x_map = lambda i, be, fi, sl, nx, ro, nu: (jnp.minimum(i, nu[0] - 1), 0)
    hbm = pl.BlockSpec(memory_space=pl.ANY)
    grid_spec = pltpu.PrefetchScalarGridSpec(
        num_scalar_prefetch=6, grid=(n_blocks,),
        in_specs=[pl.BlockSpec((bm * TOKEN_ROWS, LANES), x_map), hbm, hbm, hbm],
        out_specs=pl.BlockSpec((bm * TOKEN_ROWS, LANES), lambda i, *_: (i, 0)),
        scratch_shapes=[_words_scratch(bm),
                        pltpu.VMEM((2, d, de), F32), pltpu.VMEM((2, d, de), F32), pltpu.VMEM((2, de, d), F32),
                        pltpu.VMEM((d, de), BF16), pltpu.VMEM((d, de), BF16), pltpu.VMEM((de, d), BF16),
                        pltpu.SemaphoreType.DMA((2, 3))])
    return pl.pallas_call(
        _expert_kernel, grid_spec=grid_spec,
        out_shape=jax.ShapeDtypeStruct((n_blocks * bm * TOKEN_ROWS, LANES), BF16),
        compiler_params=_params(1), name="experts",
    )(blk_e, blk_first, blk_slot, blk_next, blk_rows, n_used, xs, w_gate, w_up, w_down)


def _combine_kernel(gstart_ref, tcnt_ref, x1_ref, sh_ref, lidx_ref, w_ref, mod_ref, fw_ref, ys_ref, o_ref,
                    stage, acc_ref, sem, *, n_tiles):
    i = pl.program_id(0)
    tc = x1_ref.shape[0]
    n_e = N_EXPERTS
    tile_rows = tc * TOP_K

    def fetch(tile, buf):
        def per_expert(e, off):
            count = tcnt_ref[tile * n_e + e]
            row = gstart_ref[tile * n_e + e]
            bit = tc
            while bit:
                take = (count & bit) != 0
                size = bit * TOKEN_ROWS
                cp = pltpu.make_async_copy(
                    ys_ref.at[pl.ds(pl.multiple_of(row * TOKEN_ROWS, TOKEN_ROWS), size)],
                    stage.at[pl.ds(pl.multiple_of(off * TOKEN_ROWS, TOKEN_ROWS), size)], sem.at[buf])
                pl.when(take)(cp.start)
                step = jnp.where(take, bit, 0)
                row, off = row + step, off + step
                bit //= 2
            return off
        lax.fori_loop(0, n_e, per_expert, buf * tile_rows)

    @pl.when(i == 0)
    def _():
        fetch(0, 0)

    @pl.when(i + 1 < n_tiles)
    def _():
        fetch(i + 1, (i + 1) % 2)

    buf = i % 2
    half = pl.ds(pl.multiple_of(buf * tile_rows * TOKEN_ROWS, TOKEN_ROWS), tile_rows * TOKEN_ROWS)
    pltpu.make_async_copy(ys_ref.at[pl.ds(0, tile_rows * TOKEN_ROWS)], stage.at[half], sem.at[buf]).wait()

    def per_token(t, c):
        lo = hi = None
        for k in range(TOP_K):
            place = lidx_ref[t * TOP_K + k]
            wk = w_ref[t * TOP_K + k]
            words = pltpu.bitcast(stage[pl.ds(pl.multiple_of(place, TOKEN_ROWS), TOKEN_ROWS), :], jnp.uint32)
            a = wk * pltpu.unpack_elementwise(words, index=0, packed_dtype=BF16, unpacked_dtype=F32)
            b = wk * pltpu.unpack_elementwise(words, index=1, packed_dtype=BF16, unpacked_dtype=F32)
            lo, hi = (a, b) if k == 0 else (lo + a, hi + b)
        rows = pl.ds(pl.multiple_of(t * TOKEN_WORD_ROWS, TOKEN_WORD_ROWS), TOKEN_WORD_ROWS)
        acc_ref[0, rows, :] = lo
        acc_ref[1, rows, :] = hi
        return c

    lax.fori_loop(0, tc, per_token, 0, unroll=8)

    halves = [jnp.concatenate([acc_ref[half, pl.ds(s, tc, stride=TOKEN_WORD_ROWS), :]
                               for s in range(TOKEN_WORD_ROWS)], axis=1) for half in range(2)]
    routed = jnp.concatenate(halves, axis=1)
    x = x1_ref[...] + mod_ref[0][5:6, :] * (routed + sh_ref[...].astype(F32))
    o_ref[...] = x * lax.rsqrt(jnp.mean(x * x, axis=-1, keepdims=True) + EPS) * fw_ref[...]


def _combine(gstart, tcnt, x1, shared, lidx, w_flat, mod3, final_w, ys, seq, tc):
    t, d = x1.shape
    n_tiles = t // tc
    tiles_per_seq = seq // tc
    per_tile = lambda: pl.BlockSpec((tc * TOP_K,), lambda i, *_: (i,), memory_space=pltpu.SMEM)
    grid_spec = pltpu.PrefetchScalarGridSpec(
        num_scalar_prefetch=2, grid=(n_tiles,),
        in_specs=[pl.BlockSpec((tc, d), lambda i, *_: (i, 0)),
                  pl.BlockSpec((tc, d), lambda i, *_: (i, 0)),
                  per_tile(), per_tile(),
                  pl.BlockSpec((1, 6, d), lambda i, *_: (i // tiles_per_seq, 0, 0)),
                  pl.BlockSpec((1, d), lambda i, *_: (0, 0)),
                  pl.BlockSpec(memory_space=pl.ANY)],
        out_specs=pl.BlockSpec((tc, d), lambda i, *_: (i, 0)),
        scratch_shapes=[pltpu.VMEM((2 * tc * TOP_K * TOKEN_ROWS, LANES), BF16),
                        pltpu.VMEM((2, tc * TOKEN_WORD_ROWS, LANES), F32),
                        pltpu.SemaphoreType.DMA((2,))])
    return pl.pallas_call(
        functools.partial(_combine_kernel, n_tiles=n_tiles), grid_spec=grid_spec,
        out_shape=jax.ShapeDtypeStruct((t, d), F32),
        compiler_params=_params(1), name="combine",
    )(gstart, tcnt, x1, shared, lidx, w_flat, mod3, final_w, ys)


def _rope_tables(seq, d):
    half = d // 2
    inv = ROPE_BASE ** (-jnp.arange(half, dtype=F32) / half)
    ang = jnp.arange(seq, dtype=F32)[:, None] * inv[None, :]
    cos, sin = jnp.cos(ang), jnp.sin(ang)
    return jnp.concatenate([cos, cos], axis=1), jnp.concatenate([-sin, sin], axis=1)


def kernel(x, c, w_ada, b_ada, norm1_w, w_in, ret_decay_fwd, ret_decay_bwd, ret_norm_w, w_ret_o, conv_w,
           w_conv_o, w_out, norm2_w, router_w, router_bias, w_gate, w_up, w_down, ws_gate, ws_up, ws_down,
           final_norm_w):
    batch, seq, d = x.shape
    depth = w_ada.shape[0]
    t = batch * seq
    bm = EXPERT_BLOCK
    n_blocks = (t * TOP_K + N_EXPERTS * (bm - 1)) // bm
    n_rows = n_blocks * bm
    cos, sin = _rope_tables(seq, HEAD_DIM)
    c_pad = jnp.pad(c, ((0, BF16_SUBLANES - batch % BF16_SUBLANES), (0, 0)))
    x2 = x.reshape(t, d)

    assert depth == 1, "the final norm is fused into the last stage of a single layer"
    for l in range(depth):
        mod = _ada(c_pad, w_ada[l], b_ada[l][None, :])[:batch]
        mod3 = mod.reshape(batch, 6, d)
        proj = _inproj(x2, mod3, norm1_w[l][None, :], w_in[l], seq)
        lane_bcast = lambda v: jnp.broadcast_to(v[:, None, None], (RET_HEADS, 1, HEAD_DIM))
        og = _retention(proj, cos, sin, lane_bcast(ret_decay_fwd[l]), lane_bcast(ret_decay_bwd[l]),
                        ret_norm_w[l].reshape(RET_HEADS, 1, HEAD_DIM), batch, seq)
        merged = _mix(og, proj, conv_w[l], w_ret_o[l].astype(BF16), w_conv_o[l].astype(BF16), seq)
        x1, hp, logits_t = _outproj(merged, x2, mod3, norm2_w[l][None, :], w_out[l].astype(BF16),
                                    router_w[l].T, seq)
        ids_t, w_t, rank_t, lidx_t, cnt, tile_carry, tile_cnt = _route(logits_t, router_bias[l][:, None], COMBINE_TILE)

        counts = cnt[:, 0]
        nblk = (counts + bm - 1) // bm
        blk_end = jnp.cumsum(nblk)
        pad_start = ((blk_end - nblk) * bm).astype(I32)
        n_used = blk_end[-1:].astype(I32)
        blk_ids = jnp.arange(n_blocks, dtype=I32)
        blk_e = jnp.minimum(jnp.sum((blk_ids[:, None] >= blk_end[None, :]).astype(I32), axis=1),
                            N_EXPERTS - 1)

        blk_first = jnp.concatenate([jnp.ones((1,), I32), (blk_e[1:] != blk_e[:-1]).astype(I32)])
        blk_slot = (jnp.cumsum(blk_first) - 1) % 2
        after = blk_end[blk_e]
        blk_next = jnp.where(after < n_used[0], blk_e[jnp.minimum(after, n_blocks - 1)], -1).astype(I32)

        onehot = ids_t[:, :, None] == jnp.arange(N_EXPERTS, dtype=I32)
        slots_t = rank_t + jnp.sum(jnp.where(onehot, pad_start, 0), axis=-1)
        slots = slots_t.T.reshape(t * TOP_K)
        xs, shared = _dispatch(pad_start, counts, hp, slots, ws_gate[l].astype(BF16), ws_up[l].astype(BF16),
                               ws_down[l].astype(BF16), n_rows, bm)
        blk_rows = jnp.clip(counts[blk_e] - (blk_ids * bm - pad_start[blk_e]), 0, bm).astype(I32)
        ys = _experts(blk_e, blk_first, blk_slot.astype(I32), blk_next, blk_rows, n_used, xs,
                      w_gate[l], w_up[l], w_down[l], n_blocks, bm)
        gstart = (pad_start[None, :] + tile_carry[:, :, 0]).reshape(-1)
        x2 = _combine(gstart, tile_cnt[:, :, 0].reshape(-1), x1, shared, lidx_t.T.reshape(t * TOP_K),
                      w_t.T.reshape(t * TOP_K), mod3, final_norm_w[None, :], ys, seq, COMBINE_TILE)
    return x2.reshape(batch, seq, d)
```
